```python
import math
import jax
import jax.numpy as jnp
from jax import lax
import numpy as np

D_MODEL = 4096
BATCH = 32
SEQ = 256
DEPTH = 2
DEC_BATCH = 2
DEC_SEQ = 4096
PAST_LEN = 256

GRID_W = 64
W_A = D_MODEL // 4
W_B = D_MODEL // 4
W_C = D_MODEL // 4
W_D = D_MODEL - W_A - W_B - W_C
HEAD_A = 64
H_A = W_A // HEAD_A
D_DECAY_LORA = max(32, int(round(W_A ** 0.5 * 1.8 / 32)) * 32)
D_AAA_LORA = max(32, int(round(W_A ** 0.5 * 1.8 / 32)) * 32)
D_GATE_LORA = max(32, int(round(W_A ** 0.8 * 0.6 / 32)) * 32)
COLS_A = 3 * W_A + D_DECAY_LORA + D_AAA_LORA + D_GATE_LORA
SPLIT_A = (W_A, 2 * W_A, 3 * W_A, 3 * W_A + D_DECAY_LORA, 3 * W_A + D_DECAY_LORA + D_AAA_LORA)
DECAY_SCALE = math.exp(-0.5)
GN_EPS = 64e-5
POOL_WINDOWS = (2, 4, 8, 16)
POOL_GROUP = W_B // len(POOL_WINDOWS)
CONV_K = 31
CHUNK = 128
H_D = 8
HEAD_D = W_D // H_D
IN_COLS = COLS_A + W_B + 2 * W_C + 2 * W_D
SPLIT_GROUPS = (COLS_A, COLS_A + W_B, COLS_A + W_B + 2 * W_C)
D_FF = ((8 * D_MODEL // 3 + 255) // 256) * 256
FFN_K = 3
ALPHA = (2.0 * DEPTH) ** 0.25
BETA = (8.0 * DEPTH) ** -0.25
LN_EPS = 1e-5
N_MOD = 6

kernel_name = 'hybrid_rwkv7_pool_conformer_gmlp_diffusion_step'


def layer_norm(x, g, b, eps=LN_EPS):
    xf = x.astype(jnp.float32)
    xc = xf - jnp.mean(xf, -1, keepdims=True)
    var = jnp.mean(xc * xc, -1, keepdims=True)
    return (xc * lax.rsqrt(var + eps)).astype(x.dtype) * g + b


def shift_prev(f):
    return jnp.pad(f, ((0, 0), (1, 0), (0, 0)))[:, :-1]


def shift_next(f):
    return jnp.pad(f, ((0, 0), (0, 1), (0, 0)))[:, 1:]


def dwconv_same(x, w, b):
    k_taps = w.shape[0]
    t = x.shape[1]
    r = k_taps // 2
    xp = jnp.pad(x, ((0, 0), (r, r), (0, 0)))
    y = xp[:, 0:t] * w[0] + b
    for j in range(1, k_taps):
        y = y + xp[:, j:j + t] * w[j]
    return y


def grid_transpose(z, rows, cols):
    b, t, ch = z.shape
    return z.reshape(b, rows, cols, ch).transpose(0, 2, 1, 3).reshape(b, t, ch)


def wkv_scan(r, w, k, v, kk, a, s0, reverse):
    def step(S, inp):
        r_t, w_t, k_t, v_t, kk_t, a_t = inp
        sa = jnp.einsum('bhij,bhj->bhi', S, -kk_t)
        S = (S * w_t[:, :, None, :] + sa[..., None] * (kk_t * a_t)[:, :, None, :]
             + v_t[..., None] * k_t[:, :, None, :])
        return S, jnp.einsum('bhij,bhj->bhi', S, r_t)
    xs = tuple(jnp.moveaxis(z, 1, 0) for z in (r, w, k, v, kk, a))
    s_fin, ys = lax.scan(step, s0, xs, reverse=reverse)
    return jnp.moveaxis(ys, 0, 1), s_fin


def rwkv7_group(fA, P, s0, rows):
    dt = fA.dtype
    if rows is not None:
        fA = grid_transpose(fA, rows, GRID_W)
    fA = fA + P['mu_prev'] * (shift_prev(fA) - fA) + P['mu_next'] * (shift_next(fA) - fA)
    b, t, _ = fA.shape
    r, k, v, xw, xa, xg = jnp.split(fA, SPLIT_A, axis=-1)

    def heads(z):
        return z.astype(jnp.float32).reshape(b, t, H_A, HEAD_A)

    g = jax.nn.sigmoid(xg) @ P['g_up']
    kk = heads(k * P['k_k'])
    kk = kk * lax.rsqrt(jnp.sum(kk * kk, -1, keepdims=True) + 1e-12)
    rh = heads(r)
    vh = heads(v)
    wx = jnp.tanh(xw)
    ys, bonus, finals = [], [], []
    for d, rev in enumerate((False, True)):
        w = jnp.exp(-DECAY_SCALE * jax.nn.sigmoid((P['w0'][d] + wx @ P['w_up'][d]).astype(jnp.float32)))
        a = jax.nn.sigmoid(P['a0'][d] + xa @ P['a_up'][d])
        kd = k * (1 + (a - 1) * P['k_a'])
        kdh = heads(kd)
        y_d, s_d = wkv_scan(rh, heads(w), kdh, vh, kk, heads(a), s0[:, d].astype(jnp.float32), rev)
        ys.append(y_d)
        finals.append(s_d)
        bonus.append(jnp.sum(rh * kdh * P['r_k'].astype(jnp.float32), -1, keepdims=True) * vh)
    y = ys[0] + ys[1]
    yc = y - jnp.mean(y, -1, keepdims=True)
    y = yc * lax.rsqrt(jnp.mean(yc * yc, -1, keepdims=True) + GN_EPS)
    y = (y.reshape(b, t, W_A) * P['rwkv_ln_g'] + P['rwkv_ln_b']
         + (bonus[0] + bonus[1]).reshape(b, t, W_A))
    out = y.astype(dt) * g
    if rows is not None:
        out = grid_transpose(out, GRID_W, rows)
    return out, jnp.stack(finals, axis=1)


def pool_group(fB, P):
    b, t, _ = fB.shape
    cs = jnp.pad(jnp.cumsum(fB.astype(jnp.float32), axis=1), ((0, 0), (1, 0), (0, 0)))
    pos = jnp.arange(t)
    groups = []
    for gi, win in enumerate(POOL_WINDOWS):
        lo = jnp.clip(pos - win // 2, 0, t)
        hi = jnp.clip(pos + win // 2, 0, t)
        sl = slice(gi * POOL_GROUP, (gi + 1) * POOL_GROUP)
        seg = cs[:, :, sl]
        mean = (seg[:, hi] - seg[:, lo]) / (hi - lo).astype(jnp.float32)[None, :, None]
        groups.append(mean - fB[:, :, sl].astype(jnp.float32))
    pooled = jnp.stack(groups, axis=2).astype(fB.dtype)
    y = jnp.einsum('btgc,gcd->btgd', pooled, P['pool_w']) + P['pool_b']
    return y.reshape(b, t, W_B) * P['pool_scale']


def conformer_conv_group(fC, P):
    a, gt = jnp.split(fC, 2, axis=-1)
    h = a * jax.nn.sigmoid(gt)
    h = dwconv_same(h, P['conf_conv_w'], P['conf_conv_b'])
    return jax.nn.silu(layer_norm(h, P['conf_ln_g'], P['conf_ln_b']))


def chunk_gmlp_group(fD, P):
    b, t, _ = fD.shape
    u, v = jnp.split(jax.nn.gelu(fD), 2, axis=-1)
    v = layer_norm(v, P['gmlp_ln_g'], P['gmlp_ln_b'])
    vc = v.reshape(b, t // CHUNK, CHUNK, H_D, HEAD_D)
    s = jnp.einsum('hpq,bnqhc->bnphc', P['gmlp_ws'], vc) + P['gmlp_bs'].T[:, :, None]
    return u * s.reshape(b, t, W_D)


def conv_ffn(h, P):
    z = h @ P['ffn_w_up']
    z = dwconv_same(z, P['ffn_conv_w'], P['ffn_conv_b'])
    a, bv = jnp.split(z, 2, axis=-1)
    return (jax.nn.silu(a) * bv) @ P['ffn_w_down']


def modulation(cvec, P):
    m = jax.nn.silu(cvec) @ P['w_ada'] + P['b_ada']
    return m.reshape(cvec.shape[0], N_MOD, D_MODEL)


def trunk_layer(x, mod, P, s0, rows):
    sh1, sc1, g1, sh2, sc2, g2 = (mod[:, i][:, None, :] for i in range(N_MOD))
    h = x * (1 + sc1) + sh1
    f = h @ P['w_in']
    fA, fB, fC, fD = jnp.split(f, SPLIT_GROUPS, axis=-1)
    yA, s_fin = rwkv7_group(fA, P, s0, rows)
    y = jnp.concatenate([yA, pool_group(fB, P), conformer_conv_group(fC, P), chunk_gmlp_group(fD, P)], axis=-1) @ P['w_out']
    x = layer_norm(ALPHA * x + g1 * y, P['ln1_g'], P['ln1_b'])
    h = x * (1 + sc2) + sh2
    x = layer_norm(ALPHA * x + g2 * conv_ffn(h, P), P['ln2_g'], P['ln2_b'])
    return x, s_fin


def setup_inputs(seed: int = 0) -> dict:
    key = jax.random.key(seed)
    ks = iter(jax.random.split(key, 48))
    L = DEPTH
    D = D_MODEL

    def nrm(shape, s):
        return jax.random.normal(next(ks), shape, jnp.float32) * s

    def uni(shape, lo, hi):
        return jax.random.uniform(next(ks), shape, jnp.float32, lo, hi)

    return {
        'x_prompt': nrm((BATCH, SEQ, D), 1.0),
        'x_sample': nrm((DEC_BATCH, DEC_SEQ, D), 1.0),
        'c': nrm((DEC_BATCH, D), 1.0),
        'state_rwkv': nrm((DEC_BATCH, DEPTH, 2, H_A, HEAD_A, HEAD_A), 0.3),
        'c_ctx': nrm((D,), 1.0),
        'w_ada': nrm((L, D, N_MOD * D), 0.5 * D ** -0.5),
        'b_ada': nrm((L, N_MOD * D), 0.02),
        'w_in': nrm((L, D, IN_COLS), D ** -0.5),
        'mu_prev': uni((L, COLS_A), 0.0, 0.5),
        'mu_next': uni((L, COLS_A), 0.0, 0.5),
        'rwkv_g_up': nrm((L, D_GATE_LORA, W_A), D_GATE_LORA ** -0.5),
        'rwkv_w0': nrm((L, 2, W_A), 1.0),
        'rwkv_w_up': nrm((L, 2, D_DECAY_LORA, W_A), D_DECAY_LORA ** -0.5),
        'rwkv_a0': nrm((L, 2, W_A), 0.5),
        'rwkv_a_up': nrm((L, 2, D_AAA_LORA, W_A), D_AAA_LORA ** -0.5),
        'rwkv_k_k': 0.85 + nrm((L, W_A), 0.05),
        'rwkv_k_a': 1.0 + nrm((L, W_A), 0.05),
        'rwkv_r_k': nrm((L, H_A, HEAD_A), 0.1),
        'rwkv_ln_g': 1.0 + nrm((L, W_A), 0.05),
        'rwkv_ln_b': nrm((L, W_A), 0.02),
        'pool_w': nrm((L, len(POOL_WINDOWS), POOL_GROUP, POOL_GROUP), POOL_GROUP ** -0.5),
        'pool_b': nrm((L, len(POOL_WINDOWS), POOL_GROUP), 0.02),
        'pool_scale': 1.0 + nrm((L, W_B), 0.05),
        'conf_conv_w': nrm((L, CONV_K, W_C), CONV_K ** -0.5),
        'conf_conv_b': nrm((L, W_C), 0.02),
        'conf_ln_g': 1.0 + nrm((L, W_C), 0.05),
        'conf_ln_b': nrm((L, W_C), 0.02),
        'gmlp_ln_g': 1.0 + nrm((L, W_D), 0.05),
        'gmlp_ln_b': nrm((L, W_D), 0.02),
        'gmlp_ws': nrm((L, H_D, CHUNK, CHUNK), CHUNK ** -0.5),
        'gmlp_bs': 1.0 + nrm((L, H_D, CHUNK), 0.1),
        'w_out': nrm((L, D, D), BETA * D ** -0.5),
        'ln1_g': 1.0 + nrm((L, D), 0.05),
        'ln1_b': nrm((L, D), 0.02),
        'ffn_w_up': nrm((L, D, 2 * D_FF), D ** -0.5),
        'ffn_conv_w': nrm((L, FFN_K, 2 * D_FF), FFN_K ** -0.5),
        'ffn_conv_b': nrm((L, 2 * D_FF), 0.02),
        'ffn_w_down': nrm((L, D_FF, D), BETA * D_FF ** -0.5),
        'ln2_g': 1.0 + nrm((L, D), 0.05),
        'ln2_b': nrm((L, D), 0.02),
    }


def reference(x_prompt, x_sample, c, state_rwkv, c_ctx, w_ada, b_ada, w_in, mu_prev, mu_next,
              rwkv_g_up, rwkv_w0, rwkv_w_up, rwkv_a0, rwkv_a_up, rwkv_k_k, rwkv_k_a, rwkv_r_k,
              rwkv_ln_g, rwkv_ln_b, pool_w, pool_b, pool_scale, conf_conv_w, conf_conv_b,
              conf_ln_g, conf_ln_b, gmlp_ln_g, gmlp_ln_b, gmlp_ws, gmlp_bs, w_out, ln1_g, ln1_b,
              ffn_w_up, ffn_conv_w, ffn_conv_b, ffn_w_down, ln2_g, ln2_b):
    rows = x_sample.shape[1] // GRID_W
    s_ctx0 = jnp.zeros((x_prompt.shape[0], 2, H_A, HEAD_A, HEAD_A), jnp.float32)
    h_ctx = x_prompt
    h_lat = x_sample
    new_states = []
    for l in range(DEPTH):
        P = {
            'w_ada': w_ada[l], 'b_ada': b_ada[l], 'w_in': w_in[l],
            'mu_prev': mu_prev[l], 'mu_next': mu_next[l], 'g_up': rwkv_g_up[l],
            'w0': rwkv_w0[l], 'w_up': rwkv_w_up[l], 'a0': rwkv_a0[l], 'a_up': rwkv_a_up[l],
            'k_k': rwkv_k_k[l], 'k_a': rwkv_k_a[l], 'r_k': rwkv_r_k[l],
            'rwkv_ln_g': rwkv_ln_g[l], 'rwkv_ln_b': rwkv_ln_b[l],
            'pool_w': pool_w[l], 'pool_b': pool_b[l], 'pool_scale': pool_scale[l],
            'conf_conv_w': conf_conv_w[l], 'conf_conv_b': conf_conv_b[l],
            'conf_ln_g': conf_ln_g[l], 'conf_ln_b': conf_ln_b[l],
            'gmlp_ln_g': gmlp_ln_g[l], 'gmlp_ln_b': gmlp_ln_b[l],
            'gmlp_ws': gmlp_ws[l], 'gmlp_bs': gmlp_bs[l], 'w_out': w_out[l],
            'ln1_g': ln1_g[l], 'ln1_b': ln1_b[l], 'ffn_w_up': ffn_w_up[l],
            'ffn_conv_w': ffn_conv_w[l], 'ffn_conv_b': ffn_conv_b[l],
            'ffn_w_down': ffn_w_down[l], 'ln2_g': ln2_g[l], 'ln2_b': ln2_b[l],
        }
        h_ctx, s_fin = trunk_layer(h_ctx, modulation(c_ctx[None, :], P), P, s_ctx0, None)
        new_states.append(s_fin)
        h_lat, _ = trunk_layer(h_lat, modulation(c, P), P, state_rwkv[:, l], rows if l % 2 == 1 else None)
    new_state_rwkv = jnp.stack(new_states, axis=1).astype(x_prompt.dtype)
    return (h_ctx, h_lat, new_state_rwkv)
```

```python
import functools
import math

import jax
import jax.numpy as jnp
from jax import lax
from jax.experimental import pallas as pl
from jax.experimental.pallas import tpu as pltpu

D_MODEL = 4096
DEPTH = 2
GRID_W = 64
W_A = D_MODEL // 4
W_B = D_MODEL // 4
W_C = D_MODEL // 4
W_D = D_MODEL - W_A - W_B - W_C
HEAD_A = 64
H_A = W_A // HEAD_A
D_DECAY_LORA = max(32, int(round(W_A ** 0.5 * 1.8 / 32)) * 32)
D_AAA_LORA = max(32, int(round(W_A ** 0.5 * 1.8 / 32)) * 32)
D_GATE_LORA = max(32, int(round(W_A ** 0.8 * 0.6 / 32)) * 32)
COLS_A = 3 * W_A + D_DECAY_LORA + D_AAA_LORA + D_GATE_LORA
DECAY_SCALE = math.exp(-0.5)
GN_EPS = 64e-5
POOL_WINDOWS = (2, 4, 8, 16)
POOL_GROUP = W_B // len(POOL_WINDOWS)
CONV_K = 31
CHUNK = 128
H_D = 8
HEAD_D = W_D // H_D
IN_COLS = COLS_A + W_B + 2 * W_C + 2 * W_D
D_FF = ((8 * D_MODEL // 3 + 255) // 256) * 256
FFN_K = 3
ALPHA = (2.0 * DEPTH) ** 0.25
LN_EPS = 1e-5
N_MOD = 6

LANES = 128
SUBLANES = 8
VMEM_LIMIT_BYTES = 56 * 1024 * 1024

SCAN_TB = 128
SCAN_UNROLL = 16


def _mm_kernel(x_ref, w_ref, o_ref, acc_ref, *, nk):
    k = pl.program_id(2)
    part = jnp.dot(x_ref[...], w_ref[...], preferred_element_type=jnp.float32)
    if nk == 1:
        o_ref[...] = part.astype(o_ref.dtype)
    else:
        @pl.when(k == 0)
        def _():
            acc_ref[...] = part

        @pl.when(k > 0)
        def _():
            acc_ref[...] += part

        @pl.when(k == nk - 1)
        def _():
            o_ref[...] = acc_ref[...].astype(o_ref.dtype)


def _matmul(x, w, out_dtype, tm, tn, tk):
    m, kdim = x.shape
    _, n = w.shape
    assert m % tm == 0 and n % tn == 0 and kdim % tk == 0
    nk = kdim // tk
    return pl.pallas_call(
        functools.partial(_mm_kernel, nk=nk),
        grid=(m // tm, n // tn, nk),
        in_specs=[
            pl.BlockSpec((tm, tk), lambda i, j, k: (i, k)),
            pl.BlockSpec((tk, tn), lambda i, j, k: (k, j)),
        ],
        out_specs=pl.BlockSpec((tm, tn), lambda i, j, k: (i, j)),
        out_shape=jax.ShapeDtypeStruct((m, n), out_dtype),
        name="dense_mm",
        scratch_shapes=[pltpu.VMEM((tm, tn), jnp.float32)],
        compiler_params=pltpu.CompilerParams(
            dimension_semantics=("parallel", "parallel", "arbitrary"),
            vmem_limit_bytes=VMEM_LIMIT_BYTES,
        ),
    )(x, w)


def _scan_kernel(rf, vf, kkf, wf, kdf, bf, rb, vb, kkb, wb, kdb, bb, s0_ref,
                 yf_ref, yb_ref, sfin_ref, st_ref, xt_ref, *, tb_len, unroll, nblk):
    tb = pl.program_id(2)

    @pl.when(tb == 0)
    def _():
        st_ref[...] = s0_ref[0, :, 0]

    for a, x in enumerate((kkf, wf, bf, kdf, rf, kkb, wb, bb, kdb, rb)):
        xt_ref[a] = x[0].T

    lane = lax.broadcasted_iota(jnp.int32, (HEAD_A, LANES), 1)
    first_head = lane < HEAD_A

    def col_tile(xt, t):
        ca = xt[0:HEAD_A, t:t + 1]
        cb = xt[HEAD_A:2 * HEAD_A, t:t + 1]
        return jnp.where(first_head, ca, cb)

    def chain_step(h, xts, vrow, t):
        kk_t, w_t, b_t, kd_t, r_t = (col_tile(x, t) for x in xts)
        sa = -jnp.sum(h * kk_t, axis=0, keepdims=True)
        h = h * w_t + sa * b_t + vrow * kd_t
        y = jnp.sum(h * r_t, axis=0, keepdims=True)
        return h, y

    def sub(q, carry):
        hf, hb = carry
        off_f = pl.multiple_of(q * unroll, unroll)
        off_b = pl.multiple_of(tb_len - (q + 1) * unroll, unroll)
        xts_f = tuple(pltpu.roll(xt_ref[a], (tb_len - off_f) % tb_len, axis=1) for a in range(5))
        xts_b = tuple(pltpu.roll(xt_ref[5 + a], (tb_len - off_b) % tb_len, axis=1) for a in range(5))
        v_f = vf[0, pl.ds(off_f, unroll), :]
        v_b = vb[0, pl.ds(off_b, unroll), :]
        ys_f = []
        ys_b = []
        for t in range(unroll):
            hf, y = chain_step(hf, xts_f, v_f[t:t + 1, :], t)
            ys_f.append(y)
            tr = unroll - 1 - t
            hb, y = chain_step(hb, xts_b, v_b[tr:tr + 1, :], tr)
            ys_b.append(y)
        yf_ref[0, pl.ds(off_f, unroll), :] = jnp.concatenate(ys_f, axis=0)
        yb_ref[0, pl.ds(off_b, unroll), :] = jnp.concatenate(ys_b[::-1], axis=0)
        return hf, hb

    hf, hb = lax.fori_loop(0, tb_len // unroll, sub, (st_ref[0], st_ref[1]))
    st_ref[0] = hf
    st_ref[1] = hb

    @pl.when(tb == nblk - 1)
    def _():
        sfin_ref[0, 0, 0] = hf
        sfin_ref[0, 1, 0] = hb


def _rwkv_scan(r, v, kk, w2, kd2, b2, s0):
    bsz, t_len, _ = r.shape
    npair = H_A // 2
    tb_len = SCAN_TB
    nblk = t_len // tb_len
    s0p = s0.reshape(bsz, 2, npair, 2, HEAD_A, HEAD_A).transpose(0, 1, 2, 5, 3, 4)
    s0p = s0p.reshape(bsz, 2, npair, HEAD_A, 2 * HEAD_A)

    def fwd3(b, p, t):
        return (b, t, p)

    def bwd3(b, p, t):
        return (b, nblk - 1 - t, p)

    blk3 = (1, tb_len, LANES)
    sh3 = lambda a, d: a[d]
    in_specs = (
        [pl.BlockSpec(blk3, fwd3)] * 6 + [pl.BlockSpec(blk3, bwd3)] * 6
        + [pl.BlockSpec((1, 2, 1, HEAD_A, LANES), lambda b, p, t: (b, 0, p, 0, 0))]
    )
    out_specs = [
        pl.BlockSpec(blk3, fwd3),
        pl.BlockSpec(blk3, bwd3),
        pl.BlockSpec((1, 2, 1, HEAD_A, LANES), lambda b, p, t: (b, 0, p, 0, 0)),
    ]
    out_shape = [
        jax.ShapeDtypeStruct((bsz, t_len, W_A), jnp.float32),
        jax.ShapeDtypeStruct((bsz, t_len, W_A), jnp.float32),
        jax.ShapeDtypeStruct((bsz, 2, npair, HEAD_A, LANES), jnp.float32),
    ]
    yf, yb, sfin = pl.pallas_call(
        functools.partial(_scan_kernel, tb_len=tb_len, unroll=SCAN_UNROLL, nblk=nblk),
        grid=(bsz, npair, nblk),
        in_specs=in_specs,
        out_specs=out_specs,
        out_shape=out_shape,
        name="rwkv_scan",
        scratch_shapes=[pltpu.VMEM((2, HEAD_A, LANES), jnp.float32),
                        pltpu.VMEM((10, LANES, tb_len), jnp.float32)],
        compiler_params=pltpu.CompilerParams(
            dimension_semantics=("parallel", "parallel", "arbitrary"),
            vmem_limit_bytes=VMEM_LIMIT_BYTES,
        ),
    )(r, v, kk, sh3(w2, 0), sh3(kd2, 0), sh3(b2, 0),
      r, v, kk, sh3(w2, 1), sh3(kd2, 1), sh3(b2, 1), s0p)
    sfin = sfin.reshape(bsz, 2, npair, HEAD_A, 2, HEAD_A).transpose(0, 1, 2, 4, 5, 3)
    return yf, yb, sfin.reshape(bsz, 2, H_A, HEAD_A, HEAD_A)


def _layer_norm(x, g, b, eps=LN_EPS):
    xc = x - jnp.mean(x, -1, keepdims=True)
    var = jnp.mean(xc * xc, -1, keepdims=True)
    return (xc * lax.rsqrt(var + eps)) * g + b


def _shift_prev(f):
    return jnp.pad(f, ((0, 0), (1, 0), (0, 0)))[:, :-1]


def _shift_next(f):
    return jnp.pad(f, ((0, 0), (0, 1), (0, 0)))[:, 1:]


def _dwconv_same(x, w, b):
    k_taps = w.shape[0]
    t = x.shape[1]
    r = k_taps // 2
    xp = jnp.pad(x, ((0, 0), (r, r), (0, 0)))
    y = xp[:, 0:t] * w[0] + b
    for j in range(1, k_taps):
        y = y + xp[:, j:j + t] * w[j]
    return y


def _grid_transpose(z, rows, cols):
    b, t, ch = z.shape
    return z.reshape(b, rows, cols, ch).transpose(0, 2, 1, 3).reshape(b, t, ch)


def _rwkv7_group(f_rkv, f_lora, P, s0, rows):
    fA = jnp.concatenate([f_rkv, f_lora], axis=-1)
    if rows is not None:
        fA = _grid_transpose(fA, rows, GRID_W)
    fA = fA + P['mu_prev'] * (_shift_prev(fA) - fA) + P['mu_next'] * (_shift_next(fA) - fA)
    b, t, _ = fA.shape
    r = fA[..., 0:W_A]
    k = fA[..., W_A:2 * W_A]
    v = fA[..., 2 * W_A:3 * W_A]
    o = 3 * W_A
    xw = fA[..., o:o + D_DECAY_LORA]
    xa = fA[..., o + D_DECAY_LORA:o + D_DECAY_LORA + D_AAA_LORA]
    xg = fA[..., o + D_DECAY_LORA + D_AAA_LORA:COLS_A]

    def heads(z):
        return z.reshape(b, t, H_A, HEAD_A)

    hp = lax.Precision.HIGHEST
    g = jnp.dot(jax.nn.sigmoid(xg), P['g_up'], precision=hp)
    kk = heads(k * P['k_k'])
    kk = kk * lax.rsqrt(jnp.sum(kk * kk, -1, keepdims=True) + 1e-12)
    kk = kk.reshape(b, t, W_A)
    wx = jnp.tanh(xw)
    ws, kds, bs, bonus = [], [], [], 0.0
    for d in range(2):
        w = jnp.exp(-DECAY_SCALE * jax.nn.sigmoid(P['w0'][d] + jnp.dot(wx, P['w_up'][d], precision=hp)))
        a = jax.nn.sigmoid(P['a0'][d] + jnp.dot(xa, P['a_up'][d], precision=hp))
        kd = k * (1 + (a - 1) * P['k_a'])
        ws.append(w)
        kds.append(kd)
        bs.append(kk * a)
        bonus = bonus + (jnp.sum(heads(r) * heads(kd) * P['r_k'], -1, keepdims=True) * heads(v)).reshape(b, t, W_A)
    yf, yb, s_fin = _rwkv_scan(r, v, kk, jnp.stack(ws), jnp.stack(kds), jnp.stack(bs), s0)
    y = heads(yf + yb)
    yc = y - jnp.mean(y, -1, keepdims=True)
    y = yc * lax.rsqrt(jnp.mean(yc * yc, -1, keepdims=True) + GN_EPS)
    y = y.reshape(b, t, W_A) * P['rwkv_ln_g'] + P['rwkv_ln_b'] + bonus
    out = y * g
    if rows is not None:
        out = _grid_transpose(out, GRID_W, rows)
    return out, s_fin


def _pool_group(fB, P):
    b, t, _ = fB.shape
    cs = jnp.pad(jnp.cumsum(fB, axis=1), ((0, 0), (1, 0), (0, 0)))
    pos = jnp.arange(t)
    groups = []
    for gi, win in enumerate(POOL_WINDOWS):
        lo = jnp.clip(pos - win // 2, 0, t)
        hi = jnp.clip(pos + win // 2, 0, t)
        sl = slice(gi * POOL_GROUP, (gi + 1) * POOL_GROUP)
        seg = cs[:, :, sl]
        mean = (seg[:, hi] - seg[:, lo]) / (hi - lo).astype(jnp.float32)[None, :, None]
        groups.append(mean - fB[:, :, sl])
    pooled = jnp.stack(groups, axis=2)
    y = jnp.einsum('btgc,gcd->btgd', pooled, P['pool_w'], precision=lax.Precision.HIGHEST) + P['pool_b']
    return y.reshape(b, t, W_B) * P['pool_scale']


def _conformer_group(fC, P):
    a, gt = jnp.split(fC, 2, axis=-1)
    h = a * jax.nn.sigmoid(gt)
    h = _dwconv_same(h, P['conf_conv_w'], P['conf_conv_b'])
    return jax.nn.silu(_layer_norm(h, P['conf_ln_g'], P['conf_ln_b']))


def _gmlp_group(fD, P):
    b, t, _ = fD.shape
    u, v = jnp.split(jax.nn.gelu(fD), 2, axis=-1)
    v = _layer_norm(v, P['gmlp_ln_g'], P['gmlp_ln_b'])
    vc = v.reshape(b, t // CHUNK, CHUNK, H_D, HEAD_D)
    s = jnp.einsum('hpq,bnqhc->bnphc', P['gmlp_ws'], vc, precision=lax.Precision.HIGHEST) + P['gmlp_bs'].T[:, :, None]
    return u * s.reshape(b, t, W_D)


def _modulation(cvec, P):
    m = jnp.dot(jax.nn.silu(cvec), P['w_ada'], precision=lax.Precision.HIGHEST) + P['b_ada']
    return m.reshape(cvec.shape[0], N_MOD, D_MODEL)


def _mix_groups(f, P, s0, rows):
    o = 3 * W_A
    f_rkv = f[..., 0:o]
    fB = f[..., o:o + W_B]
    fC = f[..., o + W_B:o + W_B + 2 * W_C]
    fD = f[..., o + W_B + 2 * W_C:o + W_B + 2 * W_C + 2 * W_D]
    lo = o + W_B + 2 * W_C + 2 * W_D
    f_lora = f[..., lo:lo + (COLS_A - 3 * W_A)]
    yA, s_fin = _rwkv7_group(f_rkv, f_lora, P, s0, rows)
    y = jnp.concatenate([yA, _pool_group(fB, P), _conformer_group(fC, P), _gmlp_group(fD, P)], axis=-1)
    return y, s_fin


IN_PAD = ((IN_COLS + 511) // 512) * 512
FF_PAD = ((D_FF + 1023) // 1024) * 1024


def _pack_w_in(w_in):
    o = 3 * W_A
    parts = [w_in[:, 0:o], w_in[:, COLS_A:], w_in[:, o:COLS_A]]
    w = jnp.concatenate(parts, axis=1)
    return jnp.pad(w, ((0, 0), (0, IN_PAD - IN_COLS))).astype(jnp.bfloat16)


def kernel(x_prompt, x_sample, c, state_rwkv, c_ctx, w_ada, b_ada, w_in, mu_prev, mu_next, rwkv_g_up, rwkv_w0, rwkv_w_up, rwkv_a0, rwkv_a_up, rwkv_k_k, rwkv_k_a, rwkv_r_k, rwkv_ln_g, rwkv_ln_b, pool_w, pool_b, pool_scale, conf_conv_w, conf_conv_b, conf_ln_g, conf_ln_b, gmlp_ln_g, gmlp_ln_b, gmlp_ws, gmlp_bs, w_out, ln1_g, ln1_b, ffn_w_up, ffn_conv_w, ffn_conv_b, ffn_w_down, ln2_g, ln2_b):
    bc, tc, d = x_prompt.shape
    bl, tl, _ = x_sample.shape
    nc = bc * tc
    rows = tl // GRID_W
    s_ctx0 = jnp.zeros((bc, 2, H_A, HEAD_A, HEAD_A), jnp.float32)
    h_ctx = x_prompt
    h_lat = x_sample
    new_states = []
    o = 3 * W_A
    for l in range(DEPTH):
        P = {
            'w_ada': w_ada[l], 'b_ada': b_ada[l],
            'mu_prev': mu_prev[l], 'mu_next': mu_next[l], 'g_up': rwkv_g_up[l],
            'w0': rwkv_w0[l], 'w_up': rwkv_w_up[l], 'a0': rwkv_a0[l], 'a_up': rwkv_a_up[l],
            'k_k': rwkv_k_k[l], 'k_a': rwkv_k_a[l], 'r_k': rwkv_r_k[l],
            'rwkv_ln_g': rwkv_ln_g[l], 'rwkv_ln_b': rwkv_ln_b[l],
            'pool_w': pool_w[l], 'pool_b': pool_b[l], 'pool_scale': pool_scale[l],
            'conf_conv_w': conf_conv_w[l], 'conf_conv_b': conf_conv_b[l],
            'conf_ln_g': conf_ln_g[l], 'conf_ln_b': conf_ln_b[l],
            'gmlp_ln_g': gmlp_ln_g[l], 'gmlp_ln_b': gmlp_ln_b[l],
            'gmlp_ws': gmlp_ws[l], 'gmlp_bs': gmlp_bs[l],
        }
        w_in_p = _pack_w_in(w_in[l])
        w_out_b = w_out[l].astype(jnp.bfloat16)
        w_up_b = ffn_w_up[l].astype(jnp.bfloat16)
        w_dn_b = jnp.pad(ffn_w_down[l], ((0, FF_PAD - D_FF), (0, 0))).astype(jnp.bfloat16)

        mod = _modulation(jnp.concatenate([c_ctx[None, :], c], axis=0), P)
        mod_c = mod[0:1]
        mod_l = mod[1:]

        def mods(m, i):
            return m[:, i][:, None, :]

        hc = h_ctx * (1 + mods(mod_c, 1)) + mods(mod_c, 0)
        hl = h_lat * (1 + mods(mod_l, 1)) + mods(mod_l, 0)
        hcat = jnp.concatenate([hc.reshape(nc, d), hl.reshape(bl * tl, d)], axis=0).astype(jnp.bfloat16)
        f = _matmul(hcat, w_in_p, jnp.float32, 1024, 512, D_MODEL)
        yc, s_fin = _mix_groups(f[:nc].reshape(bc, tc, IN_PAD), P, s_ctx0, None)
        new_states.append(s_fin)
        yl, _ = _mix_groups(f[nc:].reshape(bl, tl, IN_PAD), P, state_rwkv[:, l], rows if l % 2 == 1 else None)
        ycat = jnp.concatenate([yc.reshape(nc, d), yl.reshape(bl * tl, d)], axis=0).astype(jnp.bfloat16)
        y = _matmul(ycat, w_out_b, jnp.float32, 1024, 1024, D_MODEL)
        h_ctx = _layer_norm(ALPHA * h_ctx + mods(mod_c, 2) * y[:nc].reshape(bc, tc, d), ln1_g[l], ln1_b[l])
        h_lat = _layer_norm(ALPHA * h_lat + mods(mod_l, 2) * y[nc:].reshape(bl, tl, d), ln1_g[l], ln1_b[l])

        hc = h_ctx * (1 + mods(mod_c, 4)) + mods(mod_c, 3)
        hl = h_lat * (1 + mods(mod_l, 4)) + mods(mod_l, 3)
        hcat = jnp.concatenate([hc.reshape(nc, d), hl.reshape(bl * tl, d)], axis=0).astype(jnp.bfloat16)
        z = _matmul(hcat, w_up_b, jnp.float32, 1024, 512, D_MODEL)

        def gate(zp):
            zp = _dwconv_same(zp, ffn_conv_w[l], ffn_conv_b[l])
            a, bv = jnp.split(zp, 2, axis=-1)
            return jax.nn.silu(a) * bv

        gc = gate(z[:nc].reshape(bc, tc, 2 * D_FF)).reshape(nc, D_FF)
        gl = gate(z[nc:].reshape(bl, tl, 2 * D_FF)).reshape(bl * tl, D_FF)
        gcat = jnp.pad(jnp.concatenate([gc, gl], axis=0), ((0, 0), (0, FF_PAD - D_FF))).astype(jnp.bfloat16)
        y = _matmul(gcat, w_dn_b, jnp.float32, 1024, 1024, FF_PAD // 4)
        h_ctx = _layer_norm(ALPHA * h_ctx + mods(mod_c, 5) * y[:nc].reshape(bc, tc, d), ln2_g[l], ln2_b[l])
        h_lat = _layer_norm(ALPHA * h_lat + mods(mod_l, 5) * y[nc:].reshape(bl, tl, d), ln2_g[l], ln2_b[l])

    new_state_rwkv = jnp.stack(new_states, axis=1).astype(x_prompt.dtype)
    return (h_ctx, h_lat, new_state_rwkv)
```

```python
import functools
import math

import numpy as np
import jax
import jax.numpy as jnp
from jax import lax
from jax.experimental import pallas as pl
from jax.experimental.pallas import tpu as pltpu

D_MODEL = 4096
DEPTH = 2
GRID_W = 64
W_A = D_MODEL // 4
W_B = D_MODEL // 4
W_C = D_MODEL // 4
W_D = D_MODEL - W_A - W_B - W_C
HEAD_A = 64
H_A = W_A // HEAD_A
D_DECAY_LORA = max(32, int(round(W_A ** 0.5 * 1.8 / 32)) * 32)
D_AAA_LORA = max(32, int(round(W_A ** 0.5 * 1.8 / 32)) * 32)
D_GATE_LORA = max(32, int(round(W_A ** 0.8 * 0.6 / 32)) * 32)
COLS_A = 3 * W_A + D_DECAY_LORA + D_AAA_LORA + D_GATE_LORA
DECAY_SCALE = math.exp(-0.5)
GN_EPS = 64e-5
POOL_WINDOWS = (2, 4, 8, 16)
POOL_GROUP = W_B // len(POOL_WINDOWS)
CONV_K = 31
CHUNK = 128
H_D = 8
HEAD_D = W_D // H_D
IN_COLS = COLS_A + W_B + 2 * W_C + 2 * W_D
D_FF = ((8 * D_MODEL // 3 + 255) // 256) * 256
FFN_K = 3
ALPHA = (2.0 * DEPTH) ** 0.25
LN_EPS = 1e-5
N_MOD = 6

LANES = 128
SUBLANES = 8
VMEM_LIMIT_BYTES = 56 * 1024 * 1024

SCAN_TB = 128
SCAN_SUB = 32
SCAN_PARTS = 3
N_KEYVEC = 5


def _mm_kernel(x_ref, w_ref, o_ref, acc_ref, *, nk):
    k = pl.program_id(2)
    part = jnp.dot(x_ref[...], w_ref[...], preferred_element_type=jnp.float32)
    if nk == 1:
        o_ref[...] = part.astype(o_ref.dtype)
    else:
        @pl.when(k == 0)
        def _():
            acc_ref[...] = part

        @pl.when(k > 0)
        def _():
            acc_ref[...] += part

        @pl.when(k == nk - 1)
        def _():
            o_ref[...] = acc_ref[...].astype(o_ref.dtype)


def _matmul(x, w, out_dtype, tm, tn, tk):
    m, kdim = x.shape
    _, n = w.shape
    assert m % tm == 0 and n % tn == 0 and kdim % tk == 0
    nk = kdim // tk
    return pl.pallas_call(
        functools.partial(_mm_kernel, nk=nk),
        grid=(m // tm, n // tn, nk),
        in_specs=[
            pl.BlockSpec((tm, tk), lambda i, j, k: (i, k)),
            pl.BlockSpec((tk, tn), lambda i, j, k: (k, j)),
        ],
        out_specs=pl.BlockSpec((tm, tn), lambda i, j, k: (i, j)),
        out_shape=jax.ShapeDtypeStruct((m, n), out_dtype),
        name="dense_mm",
        scratch_shapes=[pltpu.VMEM((tm, tn), jnp.float32)],
        compiler_params=pltpu.CompilerParams(
            dimension_semantics=("parallel", "parallel", "arbitrary"),
            vmem_limit_bytes=VMEM_LIMIT_BYTES,
        ),
    )(x, w)


def _sel_table():
    c = np.arange(2 * LANES)[:, None]
    n = np.arange(2 * LANES)[None, :]
    e = np.zeros((SCAN_SUB // 2, 2 * LANES, 2 * LANES), np.float32)
    for p in range(SCAN_SUB // 2):
        t = 2 * p + n // LANES
        e[p] = (((c % LANES) < SCAN_PARTS * SCAN_SUB) & ((c % SCAN_SUB) == t)
                & ((c // LANES) == ((n % LANES) // HEAD_A)))
    return jnp.asarray(e, jnp.bfloat16)


def _split3_rows(x):
    hi = x.astype(jnp.bfloat16).astype(jnp.float32)
    r1 = x - hi
    mid = r1.astype(jnp.bfloat16).astype(jnp.float32)
    lo = r1 - mid
    return jnp.concatenate([hi, mid, lo, jnp.zeros_like(x)], axis=0)


def _scan_kernel(*refs, nblk):
    key_refs = (refs[0:5], refs[6:11])
    v_refs = (refs[5], refs[11])
    e_ref, s0_ref = refs[12], refs[13]
    y_refs = (refs[14], refs[15])
    sfin_ref, st_ref, lhs_ref = refs[16], refs[17], refs[18]
    tb = pl.program_id(2)

    @pl.when(tb == 0)
    def _():
        st_ref[...] = s0_ref[0, :, 0]

    lane = lax.broadcasted_iota(jnp.int32, (SCAN_SUB, LANES), 1)
    first_head = lane < HEAD_A

    def build_lhs(off):
        for c in range(2):
            xs = [ref[0, pl.ds(off, SCAN_SUB), :] for ref in key_refs[c]]
            for pi, (i1, i2) in enumerate(((0, 1), (2, 3), (4, None))):
                x1 = xs[i1]
                xr1 = pltpu.roll(x1, HEAD_A, axis=1)
                if i2 is None:
                    comb_a, comb_b = x1, xr1
                else:
                    x2 = xs[i2]
                    xr2 = pltpu.roll(x2, HEAD_A, axis=1)
                    comb_a = jnp.where(first_head, x1, xr2)
                    comb_b = jnp.where(first_head, xr1, x2)
                mt_a = _split3_rows(comb_a).T.astype(jnp.bfloat16)
                mt_b = _split3_rows(comb_b).T.astype(jnp.bfloat16)
                r0 = (c * N_KEYVEC + 2 * pi) * HEAD_A
                lhs_ref[r0:r0 + HEAD_A, 0:LANES] = mt_a[0:HEAD_A]
                lhs_ref[r0:r0 + HEAD_A, LANES:2 * LANES] = mt_b[0:HEAD_A]
                if i2 is not None:
                    lhs_ref[r0 + HEAD_A:r0 + 2 * HEAD_A, 0:LANES] = mt_a[HEAD_A:2 * HEAD_A]
                    lhs_ref[r0 + HEAD_A:r0 + 2 * HEAD_A, LANES:2 * LANES] = mt_b[HEAD_A:2 * HEAD_A]

    def sub(q, carry):
        hs = list(carry)
        off = pl.multiple_of(q * SCAN_SUB, SCAN_SUB)
        build_lhs(off)
        vs = [v_refs[c][0, pl.ds(off, SCAN_SUB), :] for c in range(2)]
        ys = ([], [])
        for p in range(SCAN_SUB // 2):
            tiles = jnp.dot(lhs_ref[...], e_ref[p], preferred_element_type=jnp.float32)
            for s in range(2):
                t = 2 * p + s
                for c in range(2):
                    base = c * N_KEYVEC * HEAD_A
                    kk_t, w_t, b_t, kd_t, r_t = (
                        tiles[base + k * HEAD_A:base + (k + 1) * HEAD_A, s * LANES:(s + 1) * LANES]
                        for k in range(N_KEYVEC))
                    h = hs[c]
                    sa = -jnp.sum(h * kk_t, axis=0, keepdims=True)
                    h = h * w_t + sa * b_t + vs[c][t:t + 1, :] * kd_t
                    ys[c].append(jnp.sum(h * r_t, axis=0, keepdims=True))
                    hs[c] = h
        for c in range(2):
            y_refs[c][0, pl.ds(off, SCAN_SUB), :] = jnp.concatenate(ys[c], axis=0)
        return tuple(hs)

    hf, hb = lax.fori_loop(0, SCAN_TB // SCAN_SUB, sub, (st_ref[0], st_ref[1]))
    st_ref[0] = hf
    st_ref[1] = hb

    @pl.when(tb == nblk - 1)
    def _():
        sfin_ref[0, 0, 0] = hf
        sfin_ref[0, 1, 0] = hb


def _rwkv_scan(r, v, kk, w2, kd2, b2, s0):
    bsz, t_len, _ = r.shape
    npair = H_A // 2
    nblk = t_len // SCAN_TB
    s0p = s0.reshape(bsz, 2, npair, 2, HEAD_A, HEAD_A).transpose(0, 1, 2, 5, 3, 4)
    s0p = s0p.reshape(bsz, 2, npair, HEAD_A, 2 * HEAD_A)
    flip = lambda a: jnp.flip(a, axis=1)
    ins = [kk, w2[0], b2[0], kd2[0], r, v,
           flip(kk), flip(w2[1]), flip(b2[1]), flip(kd2[1]), flip(r), flip(v)]
    blk3 = (1, SCAN_TB, LANES)
    idx3 = lambda b, p, t: (b, t, p)
    st_spec = pl.BlockSpec((1, 2, 1, HEAD_A, LANES), lambda b, p, t: (b, 0, p, 0, 0))
    e_tab = _sel_table()
    in_specs = ([pl.BlockSpec(blk3, idx3)] * 12
                + [pl.BlockSpec(e_tab.shape, lambda b, p, t: (0, 0, 0)), st_spec])
    out_specs = [pl.BlockSpec(blk3, idx3), pl.BlockSpec(blk3, idx3), st_spec]
    out_shape = [
        jax.ShapeDtypeStruct((bsz, t_len, W_A), jnp.float32),
        jax.ShapeDtypeStruct((bsz, t_len, W_A), jnp.float32),
        jax.ShapeDtypeStruct((bsz, 2, npair, HEAD_A, LANES), jnp.float32),
    ]
    yf, yb, sfin = pl.pallas_call(
        functools.partial(_scan_kernel, nblk=nblk),
        grid=(bsz, npair, nblk),
        in_specs=in_specs,
        out_specs=out_specs,
        out_shape=out_shape,
        name="rwkv_scan",
        scratch_shapes=[pltpu.VMEM((2, HEAD_A, LANES), jnp.float32),
                        pltpu.VMEM((2 * N_KEYVEC * HEAD_A, 2 * LANES), jnp.bfloat16)],
        compiler_params=pltpu.CompilerParams(
            dimension_semantics=("parallel", "parallel", "arbitrary"),
            vmem_limit_bytes=VMEM_LIMIT_BYTES,
        ),
    )(*ins, e_tab, s0p)
    sfin = sfin.reshape(bsz, 2, npair, HEAD_A, 2, HEAD_A).transpose(0, 1, 2, 4, 5, 3)
    return yf, flip(yb), sfin.reshape(bsz, 2, H_A, HEAD_A, HEAD_A)


def _layer_norm(x, g, b, eps=LN_EPS):
    xc = x - jnp.mean(x, -1, keepdims=True)
    var = jnp.mean(xc * xc, -1, keepdims=True)
    return (xc * lax.rsqrt(var + eps)) * g + b


def _shift_prev(f):
    return jnp.pad(f, ((0, 0), (1, 0), (0, 0)))[:, :-1]


def _shift_next(f):
    return jnp.pad(f, ((0, 0), (0, 1), (0, 0)))[:, 1:]


def _dwconv_same(x, w, b):
    k_taps = w.shape[0]
    t = x.shape[1]
    r = k_taps // 2
    xp = jnp.pad(x, ((0, 0), (r, r), (0, 0)))
    y = xp[:, 0:t] * w[0] + b
    for j in range(1, k_taps):
        y = y + xp[:, j:j + t] * w[j]
    return y


def _grid_transpose(z, rows, cols):
    b, t, ch = z.shape
    return z.reshape(b, rows, cols, ch).transpose(0, 2, 1, 3).reshape(b, t, ch)


def _rwkv7_group(f_rkv, f_lora, P, s0, rows):
    fA = jnp.concatenate([f_rkv, f_lora], axis=-1)
    if rows is not None:
        fA = _grid_transpose(fA, rows, GRID_W)
    fA = fA + P['mu_prev'] * (_shift_prev(fA) - fA) + P['mu_next'] * (_shift_next(fA) - fA)
    b, t, _ = fA.shape
    r = fA[..., 0:W_A]
    k = fA[..., W_A:2 * W_A]
    v = fA[..., 2 * W_A:3 * W_A]
    o = 3 * W_A
    xw = fA[..., o:o + D_DECAY_LORA]
    xa = fA[..., o + D_DECAY_LORA:o + D_DECAY_LORA + D_AAA_LORA]
    xg = fA[..., o + D_DECAY_LORA + D_AAA_LORA:COLS_A]

    def heads(z):
        return z.reshape(b, t, H_A, HEAD_A)

    hp = lax.Precision.HIGHEST
    g = jnp.dot(jax.nn.sigmoid(xg), P['g_up'], precision=hp)
    kk = heads(k * P['k_k'])
    kk = kk * lax.rsqrt(jnp.sum(kk * kk, -1, keepdims=True) + 1e-12)
    kk = kk.reshape(b, t, W_A)
    wx = jnp.tanh(xw)
    ws, kds, bs, bonus = [], [], [], 0.0
    for d in range(2):
        w = jnp.exp(-DECAY_SCALE * jax.nn.sigmoid(P['w0'][d] + jnp.dot(wx, P['w_up'][d], precision=hp)))
        a = jax.nn.sigmoid(P['a0'][d] + jnp.dot(xa, P['a_up'][d], precision=hp))
        kd = k * (1 + (a - 1) * P['k_a'])
        ws.append(w)
        kds.append(kd)
        bs.append(kk * a)
        bonus = bonus + (jnp.sum(heads(r) * heads(kd) * P['r_k'], -1, keepdims=True) * heads(v)).reshape(b, t, W_A)
    yf, yb, s_fin = _rwkv_scan(r, v, kk, jnp.stack(ws), jnp.stack(kds), jnp.stack(bs), s0)
    y = heads(yf + yb)
    yc = y - jnp.mean(y, -1, keepdims=True)
    y = yc * lax.rsqrt(jnp.mean(yc * yc, -1, keepdims=True) + GN_EPS)
    y = y.reshape(b, t, W_A) * P['rwkv_ln_g'] + P['rwkv_ln_b'] + bonus
    out = y * g
    if rows is not None:
        out = _grid_transpose(out, GRID_W, rows)
    return out, s_fin


def _pool_group(fB, P):
    b, t, _ = fB.shape
    cs = jnp.pad(jnp.cumsum(fB, axis=1), ((0, 0), (1, 0), (0, 0)))
    pos = jnp.arange(t)
    groups = []
    for gi, win in enumerate(POOL_WINDOWS):
        lo = jnp.clip(pos - win // 2, 0, t)
        hi = jnp.clip(pos + win // 2, 0, t)
        sl = slice(gi * POOL_GROUP, (gi + 1) * POOL_GROUP)
        seg = cs[:, :, sl]
        mean = (seg[:, hi] - seg[:, lo]) / (hi - lo).astype(jnp.float32)[None, :, None]
        groups.append(mean - fB[:, :, sl])
    pooled = jnp.stack(groups, axis=2)
    y = jnp.einsum('btgc,gcd->btgd', pooled, P['pool_w'], precision=lax.Precision.HIGHEST) + P['pool_b']
    return y.reshape(b, t, W_B) * P['pool_scale']


def _conformer_group(fC, P):
    a, gt = jnp.split(fC, 2, axis=-1)
    h = a * jax.nn.sigmoid(gt)
    h = _dwconv_same(h, P['conf_conv_w'], P['conf_conv_b'])
    return jax.nn.silu(_layer_norm(h, P['conf_ln_g'], P['conf_ln_b']))


def _gmlp_group(fD, P):
    b, t, _ = fD.shape
    u, v = jnp.split(jax.nn.gelu(fD), 2, axis=-1)
    v = _layer_norm(v, P['gmlp_ln_g'], P['gmlp_ln_b'])
    vc = v.reshape(b, t // CHUNK, CHUNK, H_D, HEAD_D)
    s = jnp.einsum('hpq,bnqhc->bnphc', P['gmlp_ws'], vc, precision=lax.Precision.HIGHEST) + P['gmlp_bs'].T[:, :, None]
    return u * s.reshape(b, t, W_D)


def _modulation(cvec, P):
    m = jnp.dot(jax.nn.silu(cvec), P['w_ada'], precision=lax.Precision.HIGHEST) + P['b_ada']
    return m.reshape(cvec.shape[0], N_MOD, D_MODEL)


def _mix_groups(f, P, s0, rows):
    o = 3 * W_A
    f_rkv = f[..., 0:o]
    fB = f[..., o:o + W_B]
    fC = f[..., o + W_B:o + W_B + 2 * W_C]
    fD = f[..., o + W_B + 2 * W_C:o + W_B + 2 * W_C + 2 * W_D]
    lo = o + W_B + 2 * W_C + 2 * W_D
    f_lora = f[..., lo:lo + (COLS_A - 3 * W_A)]
    yA, s_fin = _rwkv7_group(f_rkv, f_lora, P, s0, rows)
    y = jnp.concatenate([yA, _pool_group(fB, P), _conformer_group(fC, P), _gmlp_group(fD, P)], axis=-1)
    return y, s_fin


IN_PAD = ((IN_COLS + 511) // 512) * 512
FF_PAD = ((D_FF + 1023) // 1024) * 1024


def _pack_w_in(w_in):
    o = 3 * W_A
    parts = [w_in[:, 0:o], w_in[:, COLS_A:], w_in[:, o:COLS_A]]
    w = jnp.concatenate(parts, axis=1)
    return jnp.pad(w, ((0, 0), (0, IN_PAD - IN_COLS))).astype(jnp.bfloat16)


def kernel(x_prompt, x_sample, c, state_rwkv, c_ctx, w_ada, b_ada, w_in, mu_prev, mu_next, rwkv_g_up, rwkv_w0, rwkv_w_up, rwkv_a0, rwkv_a_up, rwkv_k_k, rwkv_k_a, rwkv_r_k, rwkv_ln_g, rwkv_ln_b, pool_w, pool_b, pool_scale, conf_conv_w, conf_conv_b, conf_ln_g, conf_ln_b, gmlp_ln_g, gmlp_ln_b, gmlp_ws, gmlp_bs, w_out, ln1_g, ln1_b, ffn_w_up, ffn_conv_w, ffn_conv_b, ffn_w_down, ln2_g, ln2_b):
    bc, tc, d = x_prompt.shape
    bl, tl, _ = x_sample.shape
    nc = bc * tc
    rows = tl // GRID_W
    s_ctx0 = jnp.zeros((bc, 2, H_A, HEAD_A, HEAD_A), jnp.float32)
    h_ctx = x_prompt
    h_lat = x_sample
    new_states = []
    for l in range(DEPTH):
        P = {
            'w_ada': w_ada[l], 'b_ada': b_ada[l],
            'mu_prev': mu_prev[l], 'mu_next': mu_next[l], 'g_up': rwkv_g_up[l],
            'w0': rwkv_w0[l], 'w_up': rwkv_w_up[l], 'a0': rwkv_a0[l], 'a_up': rwkv_a_up[l],
            'k_k': rwkv_k_k[l], 'k_a': rwkv_k_a[l], 'r_k': rwkv_r_k[l],
            'rwkv_ln_g': rwkv_ln_g[l], 'rwkv_ln_b': rwkv_ln_b[l],
            'pool_w': pool_w[l], 'pool_b': pool_b[l], 'pool_scale': pool_scale[l],
            'conf_conv_w': conf_conv_w[l], 'conf_conv_b': conf_conv_b[l],
            'conf_ln_g': conf_ln_g[l], 'conf_ln_b': conf_ln_b[l],
            'gmlp_ln_g': gmlp_ln_g[l], 'gmlp_ln_b': gmlp_ln_b[l],
            'gmlp_ws': gmlp_ws[l], 'gmlp_bs': gmlp_bs[l],
        }
        w_in_p = _pack_w_in(w_in[l])
        w_out_b = w_out[l].astype(jnp.bfloat16)
        w_up_b = ffn_w_up[l].astype(jnp.bfloat16)
        w_dn_b = jnp.pad(ffn_w_down[l], ((0, FF_PAD - D_FF), (0, 0))).astype(jnp.bfloat16)

        mod = _modulation(jnp.concatenate([c_ctx[None, :], c], axis=0), P)
        mod_c = mod[0:1]
        mod_l = mod[1:]

        def mods(m, i):
            return m[:, i][:, None, :]

        hc = h_ctx * (1 + mods(mod_c, 1)) + mods(mod_c, 0)
        hl = h_lat * (1 + mods(mod_l, 1)) + mods(mod_l, 0)
        hcat = jnp.concatenate([hc.reshape(nc, d), hl.reshape(bl * tl, d)], axis=0).astype(jnp.bfloat16)
        f = _matmul(hcat, w_in_p, jnp.float32, 1024, 512, D_MODEL)
        yc, s_fin = _mix_groups(f[:nc].reshape(bc, tc, IN_PAD), P, s_ctx0, None)
        new_states.append(s_fin)
        yl, _ = _mix_groups(f[nc:].reshape(bl, tl, IN_PAD), P, state_rwkv[:, l], rows if l % 2 == 1 else None)
        ycat = jnp.concatenate([yc.reshape(nc, d), yl.reshape(bl * tl, d)], axis=0).astype(jnp.bfloat16)
        y = _matmul(ycat, w_out_b, jnp.float32, 1024, 1024, D_MODEL)
        h_ctx = _layer_norm(ALPHA * h_ctx + mods(mod_c, 2) * y[:nc].reshape(bc, tc, d), ln1_g[l], ln1_b[l])
        h_lat = _layer_norm(ALPHA * h_lat + mods(mod_l, 2) * y[nc:].reshape(bl, tl, d), ln1_g[l], ln1_b[l])

        hc = h_ctx * (1 + mods(mod_c, 4)) + mods(mod_c, 3)
        hl = h_lat * (1 + mods(mod_l, 4)) + mods(mod_l, 3)
        hcat = jnp.concatenate([hc.reshape(nc, d), hl.reshape(bl * tl, d)], axis=0).astype(jnp.bfloat16)
        z = _matmul(hcat, w_up_b, jnp.float32, 1024, 512, D_MODEL)

        def gate(zp):
            zp = _dwconv_same(zp, ffn_conv_w[l], ffn_conv_b[l])
            a, bv = jnp.split(zp, 2, axis=-1)
            return jax.nn.silu(a) * bv

        gc = gate(z[:nc].reshape(bc, tc, 2 * D_FF)).reshape(nc, D_FF)
        gl = gate(z[nc:].reshape(bl, tl, 2 * D_FF)).reshape(bl * tl, D_FF)
        gcat = jnp.pad(jnp.concatenate([gc, gl], axis=0), ((0, 0), (0, FF_PAD - D_FF))).astype(jnp.bfloat16)
        y = _matmul(gcat, w_dn_b, jnp.float32, 1024, 1024, FF_PAD // 4)
        h_ctx = _layer_norm(ALPHA * h_ctx + mods(mod_c, 5) * y[:nc].reshape(bc, tc, d), ln2_g[l], ln2_b[l])
        h_lat = _layer_norm(ALPHA * h_lat + mods(mod_l, 5) * y[nc:].reshape(bl, tl, d), ln2_g[l], ln2_b[l])

    new_state_rwkv = jnp.stack(new_states, axis=1).astype(x_prompt.dtype)
    return (h_ctx, h_lat, new_state_rwkv)
```

```python
import functools
import math

import numpy as np
import jax
import jax.numpy as jnp
from jax import lax
from jax.experimental import pallas as pl
from jax.experimental.pallas import tpu as pltpu

D_MODEL = 4096
DEPTH = 2
GRID_W = 64
W_A = D_MODEL // 4
W_B = D_MODEL // 4
W_C = D_MODEL // 4
W_D = D_MODEL - W_A - W_B - W_C
HEAD_A = 64
H_A = W_A // HEAD_A
D_DECAY_LORA = max(32, int(round(W_A ** 0.5 * 1.8 / 32)) * 32)
D_AAA_LORA = max(32, int(round(W_A ** 0.5 * 1.8 / 32)) * 32)
D_GATE_LORA = max(32, int(round(W_A ** 0.8 * 0.6 / 32)) * 32)
COLS_A = 3 * W_A + D_DECAY_LORA + D_AAA_LORA + D_GATE_LORA
DECAY_SCALE = math.exp(-0.5)
GN_EPS = 64e-5
POOL_WINDOWS = (2, 4, 8, 16)
POOL_GROUP = W_B // len(POOL_WINDOWS)
CONV_K = 31
CHUNK = 128
H_D = 8
HEAD_D = W_D // H_D
IN_COLS = COLS_A + W_B + 2 * W_C + 2 * W_D
D_FF = ((8 * D_MODEL // 3 + 255) // 256) * 256
FFN_K = 3
ALPHA = (2.0 * DEPTH) ** 0.25
LN_EPS = 1e-5
N_MOD = 6
IN_PAD = ((IN_COLS + 511) // 512) * 512
FF_PAD = ((D_FF + 1023) // 1024) * 1024

LANES = 128
SUBLANES = 8
VMEM_LIMIT_BYTES = 56 * 1024 * 1024

SCAN_TB = 128
SCAN_SUB = 32
SCAN_PARTS = 3
N_KEYVEC = 5

ROW_TILE = 256
FF_CHUNK = 256


def _mm_kernel(x_ref, w_ref, o_ref, acc_ref, *, nk):
    k = pl.program_id(2)
    part = jnp.dot(x_ref[...], w_ref[...], preferred_element_type=jnp.float32)
    if nk == 1:
        o_ref[...] = part.astype(o_ref.dtype)
    else:
        @pl.when(k == 0)
        def _():
            acc_ref[...] = part

        @pl.when(k > 0)
        def _():
            acc_ref[...] += part

        @pl.when(k == nk - 1)
        def _():
            o_ref[...] = acc_ref[...].astype(o_ref.dtype)


def _matmul(x, w, out_dtype, tm, tn, tk):
    m, kdim = x.shape
    _, n = w.shape
    assert m % tm == 0 and n % tn == 0 and kdim % tk == 0
    nk = kdim // tk
    return pl.pallas_call(
        functools.partial(_mm_kernel, nk=nk),
        grid=(m // tm, n // tn, nk),
        in_specs=[
            pl.BlockSpec((tm, tk), lambda i, j, k: (i, k)),
            pl.BlockSpec((tk, tn), lambda i, j, k: (k, j)),
        ],
        out_specs=pl.BlockSpec((tm, tn), lambda i, j, k: (i, j)),
        out_shape=jax.ShapeDtypeStruct((m, n), out_dtype),
        name="dense_mm",
        scratch_shapes=[pltpu.VMEM((tm, tn), jnp.float32)],
        compiler_params=pltpu.CompilerParams(
            dimension_semantics=("parallel", "parallel", "arbitrary"),
            vmem_limit_bytes=VMEM_LIMIT_BYTES,
        ),
    )(x, w)


def _sel_table():
    c = np.arange(2 * LANES)[:, None]
    n = np.arange(2 * LANES)[None, :]
    e = np.zeros((SCAN_SUB // 2, 2 * LANES, 2 * LANES), np.float32)
    for p in range(SCAN_SUB // 2):
        t = 2 * p + n // LANES
        e[p] = (((c % LANES) < SCAN_PARTS * SCAN_SUB) & ((c % SCAN_SUB) == t)
                & ((c // LANES) == ((n % LANES) // HEAD_A)))
    return jnp.asarray(e, jnp.bfloat16)


def _split3_rows(x):
    hi = x.astype(jnp.bfloat16).astype(jnp.float32)
    r1 = x - hi
    mid = r1.astype(jnp.bfloat16).astype(jnp.float32)
    lo = r1 - mid
    return jnp.concatenate([hi, mid, lo, jnp.zeros_like(x)], axis=0)


def _scan_kernel(*refs, nblk):
    key_refs = (refs[0:5], refs[6:11])
    v_refs = (refs[5], refs[11])
    e_ref, s0_ref = refs[12], refs[13]
    y_refs = (refs[14], refs[15])
    sfin_ref, st_ref, lhs_ref = refs[16], refs[17], refs[18]
    tb = pl.program_id(2)

    @pl.when(tb == 0)
    def _():
        st_ref[...] = s0_ref[0, :, 0]

    lane = lax.broadcasted_iota(jnp.int32, (SCAN_SUB, LANES), 1)
    first_head = lane < HEAD_A

    def build_lhs(offs):
        for c in range(2):
            xs = [ref[0, pl.ds(offs[c], SCAN_SUB), :] for ref in key_refs[c]]
            for pi, (i1, i2) in enumerate(((0, 1), (2, 3), (4, None))):
                x1 = xs[i1]
                xr1 = pltpu.roll(x1, HEAD_A, axis=1)
                if i2 is None:
                    comb_a, comb_b = x1, xr1
                else:
                    x2 = xs[i2]
                    xr2 = pltpu.roll(x2, HEAD_A, axis=1)
                    comb_a = jnp.where(first_head, x1, xr2)
                    comb_b = jnp.where(first_head, xr1, x2)
                mt_a = _split3_rows(comb_a).T.astype(jnp.bfloat16)
                mt_b = _split3_rows(comb_b).T.astype(jnp.bfloat16)
                r0 = (c * N_KEYVEC + 2 * pi) * HEAD_A
                lhs_ref[r0:r0 + HEAD_A, 0:LANES] = mt_a[0:HEAD_A]
                lhs_ref[r0:r0 + HEAD_A, LANES:2 * LANES] = mt_b[0:HEAD_A]
                if i2 is not None:
                    lhs_ref[r0 + HEAD_A:r0 + 2 * HEAD_A, 0:LANES] = mt_a[HEAD_A:2 * HEAD_A]
                    lhs_ref[r0 + HEAD_A:r0 + 2 * HEAD_A, LANES:2 * LANES] = mt_b[HEAD_A:2 * HEAD_A]

    def sub(q, carry):
        hs = list(carry)
        offs = (pl.multiple_of(q * SCAN_SUB, SCAN_SUB),
                pl.multiple_of(SCAN_TB - (q + 1) * SCAN_SUB, SCAN_SUB))
        build_lhs(offs)
        vs = [v_refs[c][0, pl.ds(offs[c], SCAN_SUB), :] for c in range(2)]
        ys = ([], [])
        half = N_KEYVEC * HEAD_A
        npairs = SCAN_SUB // 2
        for p in range(npairs):
            tiles = (jnp.dot(lhs_ref[0:half, :], e_ref[p], preferred_element_type=jnp.float32),
                     jnp.dot(lhs_ref[half:2 * half, :], e_ref[npairs - 1 - p], preferred_element_type=jnp.float32))
            for s in range(2):
                u = 2 * p + s
                for c in range(2):
                    col = s if c == 0 else 1 - s
                    row = u if c == 0 else SCAN_SUB - 1 - u
                    kk_t, w_t, b_t, kd_t, r_t = (
                        tiles[c][k * HEAD_A:(k + 1) * HEAD_A, col * LANES:(col + 1) * LANES]
                        for k in range(N_KEYVEC))
                    h = hs[c]
                    sa = -jnp.sum(h * kk_t, axis=0, keepdims=True)
                    h = h * w_t + sa * b_t + vs[c][row:row + 1, :] * kd_t
                    ys[c].append(jnp.sum(h * r_t, axis=0, keepdims=True))
                    hs[c] = h
        y_refs[0][0, pl.ds(offs[0], SCAN_SUB), :] = jnp.concatenate(ys[0], axis=0)
        y_refs[1][0, pl.ds(offs[1], SCAN_SUB), :] = jnp.concatenate(ys[1][::-1], axis=0)
        return tuple(hs)

    hf, hb = lax.fori_loop(0, SCAN_TB // SCAN_SUB, sub, (st_ref[0], st_ref[1]))
    st_ref[0] = hf
    st_ref[1] = hb

    @pl.when(tb == nblk - 1)
    def _():
        sfin_ref[0, 0, 0] = hf
        sfin_ref[0, 1, 0] = hb


def _rwkv_scan(r, v, kk, w2, kd2, b2, s0):
    bsz, t_len, _ = r.shape
    npair = H_A // 2
    nblk = t_len // SCAN_TB
    s0p = s0.reshape(bsz, 2, npair, 2, HEAD_A, HEAD_A).transpose(0, 1, 2, 5, 3, 4)
    s0p = s0p.reshape(bsz, 2, npair, HEAD_A, 2 * HEAD_A)
    ins = [kk, w2[0], b2[0], kd2[0], r, v, kk, w2[1], b2[1], kd2[1], r, v]
    blk3 = (1, SCAN_TB, LANES)
    idx3 = lambda b, p, t: (b, t, p)
    idx3b = lambda b, p, t: (b, nblk - 1 - t, p)
    st_spec = pl.BlockSpec((1, 2, 1, HEAD_A, LANES), lambda b, p, t: (b, 0, p, 0, 0))
    e_tab = _sel_table()
    in_specs = ([pl.BlockSpec(blk3, idx3)] * 6 + [pl.BlockSpec(blk3, idx3b)] * 6
                + [pl.BlockSpec(e_tab.shape, lambda b, p, t: (0, 0, 0)), st_spec])
    out_specs = [pl.BlockSpec(blk3, idx3), pl.BlockSpec(blk3, idx3b), st_spec]
    out_shape = [
        jax.ShapeDtypeStruct((bsz, t_len, W_A), jnp.float32),
        jax.ShapeDtypeStruct((bsz, t_len, W_A), jnp.float32),
        jax.ShapeDtypeStruct((bsz, 2, npair, HEAD_A, LANES), jnp.float32),
    ]
    yf, yb, sfin = pl.pallas_call(
        functools.partial(_scan_kernel, nblk=nblk),
        grid=(bsz, npair, nblk),
        in_specs=in_specs,
        out_specs=out_specs,
        out_shape=out_shape,
        name="rwkv_scan",
        scratch_shapes=[pltpu.VMEM((2, HEAD_A, LANES), jnp.float32),
                        pltpu.VMEM((2 * N_KEYVEC * HEAD_A, 2 * LANES), jnp.bfloat16)],
        compiler_params=pltpu.CompilerParams(
            dimension_semantics=("parallel", "parallel", "arbitrary"),
            vmem_limit_bytes=VMEM_LIMIT_BYTES,
        ),
    )(*ins, e_tab, s0p)
    sfin = sfin.reshape(bsz, 2, npair, HEAD_A, 2, HEAD_A).transpose(0, 1, 2, 4, 5, 3)
    return yf, yb, sfin.reshape(bsz, 2, H_A, HEAD_A, HEAD_A)


class TokenLayout:
    def __init__(self, bc, tc, bl, tl):
        assert tc % ROW_TILE == 0 and tl % ROW_TILE == 0
        self.n_rows = bc * tc + bl * tl
        self.n_tiles = self.n_rows // ROW_TILE
        self.ctx_tiles = bc * tc // ROW_TILE
        self.ctx_seq_tiles = tc // ROW_TILE
        self.lat_seq_tiles = tl // ROW_TILE

    def group(self, i):
        return jnp.where(i < self.ctx_tiles, 0, 1 + (i - self.ctx_tiles) // self.lat_seq_tiles)

    def pos_in_seq(self, i):
        return jnp.where(i < self.ctx_tiles, i % self.ctx_seq_tiles, (i - self.ctx_tiles) % self.lat_seq_tiles)

    def seq_tiles(self, i):
        return jnp.where(i < self.ctx_tiles, self.ctx_seq_tiles, self.lat_seq_tiles)


def _modulate_kernel(x_ref, m_ref, h_ref):
    x = x_ref[...]
    h_ref[...] = (x * (1 + m_ref[0, 0:1, :]) + m_ref[0, 1:2, :]).astype(h_ref.dtype)


def _modulate(x, m2, lay):
    n, d = x.shape
    return pl.pallas_call(
        _modulate_kernel,
        grid=(lay.n_tiles,),
        in_specs=[pl.BlockSpec((ROW_TILE, d), lambda i: (i, 0)),
                  pl.BlockSpec((1, 2, d), lambda i: (lay.group(i), 0, 0))],
        out_specs=pl.BlockSpec((ROW_TILE, d), lambda i: (i, 0)),
        out_shape=jax.ShapeDtypeStruct((n, d), jnp.bfloat16),
        name="modulate",
        compiler_params=pltpu.CompilerParams(dimension_semantics=("parallel",),
                                             vmem_limit_bytes=VMEM_LIMIT_BYTES),
    )(x, m2)


def _ln_residual_kernel(y_ref, x_ref, m_ref, g_ref, b_ref, xo_ref, *h_ref):
    v = ALPHA * x_ref[...] + m_ref[0, 0:1, :] * y_ref[...]
    vc = v - jnp.mean(v, -1, keepdims=True)
    var = jnp.mean(vc * vc, -1, keepdims=True)
    xn = (vc * lax.rsqrt(var + LN_EPS)) * g_ref[...] + b_ref[...]
    xo_ref[...] = xn
    if h_ref:
        h_ref[0][...] = (xn * (1 + m_ref[0, 1:2, :]) + m_ref[0, 2:3, :]).astype(jnp.bfloat16)


def _ln_residual(y, x, m3, g, b, lay, with_h):
    n, d = x.shape
    row = pl.BlockSpec((ROW_TILE, d), lambda i: (i, 0))
    vec = pl.BlockSpec((1, d), lambda i: (0, 0))
    out_shape = [jax.ShapeDtypeStruct((n, d), jnp.float32)]
    out_specs = [row]
    if with_h:
        out_shape.append(jax.ShapeDtypeStruct((n, d), jnp.bfloat16))
        out_specs.append(row)
    return pl.pallas_call(
        _ln_residual_kernel,
        grid=(lay.n_tiles,),
        in_specs=[row, row, pl.BlockSpec((1, 3, d), lambda i: (lay.group(i), 0, 0)), vec, vec],
        out_specs=out_specs,
        out_shape=out_shape,
        name="ln_residual",
        compiler_params=pltpu.CompilerParams(dimension_semantics=("parallel",),
                                             vmem_limit_bytes=VMEM_LIMIT_BYTES),
    )(y, x, m3, g.reshape(1, d), b.reshape(1, d))


def _ffn_gate_kernel(zc_ref, zp_ref, zn_ref, w_ref, b_ref, o_ref, *, lay):
    i = pl.program_id(0)
    pos = lay.pos_in_seq(i)
    has_prev = pos > 0
    has_next = pos < lay.seq_tiles(i) - 1
    row = lax.broadcasted_iota(jnp.int32, (ROW_TILE, FF_CHUNK), 0)
    n_chunks = D_FF // FF_CHUNK

    def conv(col):
        sl = pl.ds(col, FF_CHUNK)
        z = zc_ref[:, sl].astype(jnp.float32)
        zprev_row = jnp.where(has_prev, zp_ref[SUBLANES - 1:SUBLANES, sl].astype(jnp.float32), 0.0)
        znext_row = jnp.where(has_next, zn_ref[0:1, sl].astype(jnp.float32), 0.0)
        zprev = jnp.where(row == 0, zprev_row, pltpu.roll(z, 1, axis=0))
        znext = jnp.where(row == ROW_TILE - 1, znext_row, pltpu.roll(z, ROW_TILE - 1, axis=0))
        return zprev * w_ref[0:1, sl] + b_ref[0:1, sl] + z * w_ref[1:2, sl] + znext * w_ref[2:3, sl]

    def body(j, carry):
        ca = pl.multiple_of(j * FF_CHUNK, FF_CHUNK)
        cb = pl.multiple_of(D_FF + j * FF_CHUNK, FF_CHUNK)
        a = conv(ca)
        bv = conv(cb)
        o_ref[:, pl.ds(ca, FF_CHUNK)] = (a * jax.nn.sigmoid(a) * bv).astype(o_ref.dtype)
        return carry

    lax.fori_loop(0, n_chunks, body, 0)
    o_ref[:, D_FF:FF_PAD] = jnp.zeros((ROW_TILE, FF_PAD - D_FF), o_ref.dtype)


def _ffn_gate(z, conv_w, conv_b, lay):
    n, c2 = z.shape
    rt8 = ROW_TILE // SUBLANES
    last8 = n // SUBLANES - 1
    return pl.pallas_call(
        functools.partial(_ffn_gate_kernel, lay=lay),
        grid=(lay.n_tiles,),
        in_specs=[pl.BlockSpec((ROW_TILE, c2), lambda i: (i, 0)),
                  pl.BlockSpec((SUBLANES, c2), lambda i: (jnp.maximum(i * rt8 - 1, 0), 0)),
                  pl.BlockSpec((SUBLANES, c2), lambda i: (jnp.minimum((i + 1) * rt8, last8), 0)),
                  pl.BlockSpec((3, c2), lambda i: (0, 0)),
                  pl.BlockSpec((1, c2), lambda i: (0, 0))],
        out_specs=pl.BlockSpec((ROW_TILE, FF_PAD), lambda i: (i, 0)),
        out_shape=jax.ShapeDtypeStruct((n, FF_PAD), jnp.bfloat16),
        name="ffn_gate",
        compiler_params=pltpu.CompilerParams(dimension_semantics=("parallel",),
                                             vmem_limit_bytes=VMEM_LIMIT_BYTES),
    )(z, z, z, conv_w, conv_b.reshape(1, c2))


def _layer_norm(x, g, b, eps=LN_EPS):
    xc = x - jnp.mean(x, -1, keepdims=True)
    var = jnp.mean(xc * xc, -1, keepdims=True)
    return (xc * lax.rsqrt(var + eps)) * g + b


def _shift_prev(f):
    return jnp.pad(f, ((0, 0), (1, 0), (0, 0)))[:, :-1]


def _shift_next(f):
    return jnp.pad(f, ((0, 0), (0, 1), (0, 0)))[:, 1:]


def _dwconv_same(x, w, b):
    k_taps = w.shape[0]
    t = x.shape[1]
    r = k_taps // 2
    xp = jnp.pad(x, ((0, 0), (r, r), (0, 0)))
    y = xp[:, 0:t] * w[0] + b
    for j in range(1, k_taps):
        y = y + xp[:, j:j + t] * w[j]
    return y


def _grid_transpose(z, rows, cols):
    b, t, ch = z.shape
    return z.reshape(b, rows, cols, ch).transpose(0, 2, 1, 3).reshape(b, t, ch)


def _rwkv7_group(f_rkv, f_lora, P, s0, rows):
    fA = jnp.concatenate([f_rkv, f_lora], axis=-1)
    if rows is not None:
        fA = _grid_transpose(fA, rows, GRID_W)
    fA = fA + P['mu_prev'] * (_shift_prev(fA) - fA) + P['mu_next'] * (_shift_next(fA) - fA)
    b, t, _ = fA.shape
    r = fA[..., 0:W_A]
    k = fA[..., W_A:2 * W_A]
    v = fA[..., 2 * W_A:3 * W_A]
    o = 3 * W_A
    xw = fA[..., o:o + D_DECAY_LORA]
    xa = fA[..., o + D_DECAY_LORA:o + D_DECAY_LORA + D_AAA_LORA]
    xg = fA[..., o + D_DECAY_LORA + D_AAA_LORA:COLS_A]

    def heads(z):
        return z.reshape(b, t, H_A, HEAD_A)

    hp = lax.Precision.HIGHEST
    g = jnp.dot(jax.nn.sigmoid(xg), P['g_up'], precision=hp)
    kk = heads(k * P['k_k'])
    kk = kk * lax.rsqrt(jnp.sum(kk * kk, -1, keepdims=True) + 1e-12)
    kk = kk.reshape(b, t, W_A)
    wx = jnp.tanh(xw)
    ws, kds, bs, bonus = [], [], [], 0.0
    for d in range(2):
        w = jnp.exp(-DECAY_SCALE * jax.nn.sigmoid(P['w0'][d] + jnp.dot(wx, P['w_up'][d], precision=hp)))
        a = jax.nn.sigmoid(P['a0'][d] + jnp.dot(xa, P['a_up'][d], precision=hp))
        kd = k * (1 + (a - 1) * P['k_a'])
        ws.append(w)
        kds.append(kd)
        bs.append(kk * a)
        bonus = bonus + (jnp.sum(heads(r) * heads(kd) * P['r_k'], -1, keepdims=True) * heads(v)).reshape(b, t, W_A)
    yf, yb, s_fin = _rwkv_scan(r, v, kk, jnp.stack(ws), jnp.stack(kds), jnp.stack(bs), s0)
    y = heads(yf + yb)
    yc = y - jnp.mean(y, -1, keepdims=True)
    y = yc * lax.rsqrt(jnp.mean(yc * yc, -1, keepdims=True) + GN_EPS)
    y = y.reshape(b, t, W_A) * P['rwkv_ln_g'] + P['rwkv_ln_b'] + bonus
    out = y * g
    if rows is not None:
        out = _grid_transpose(out, GRID_W, rows)
    return out, s_fin


def _pool_group(fB, P):
    b, t, _ = fB.shape
    cs = jnp.pad(jnp.cumsum(fB, axis=1), ((0, 0), (1, 0), (0, 0)))
    pos = jnp.arange(t)
    groups = []
    for gi, win in enumerate(POOL_WINDOWS):
        lo = jnp.clip(pos - win // 2, 0, t)
        hi = jnp.clip(pos + win // 2, 0, t)
        sl = slice(gi * POOL_GROUP, (gi + 1) * POOL_GROUP)
        seg = cs[:, :, sl]
        mean = (seg[:, hi] - seg[:, lo]) / (hi - lo).astype(jnp.float32)[None, :, None]
        groups.append(mean - fB[:, :, sl])
    pooled = jnp.stack(groups, axis=2)
    y = jnp.einsum('btgc,gcd->btgd', pooled, P['pool_w'], precision=lax.Precision.HIGHEST) + P['pool_b']
    return y.reshape(b, t, W_B) * P['pool_scale']


def _conformer_group(fC, P):
    a, gt = jnp.split(fC, 2, axis=-1)
    h = a * jax.nn.sigmoid(gt)
    h = _dwconv_same(h, P['conf_conv_w'], P['conf_conv_b'])
    return jax.nn.silu(_layer_norm(h, P['conf_ln_g'], P['conf_ln_b']))


def _gmlp_group(fD, P):
    b, t, _ = fD.shape
    u, v = jnp.split(jax.nn.gelu(fD), 2, axis=-1)
    v = _layer_norm(v, P['gmlp_ln_g'], P['gmlp_ln_b'])
    vc = v.reshape(b, t // CHUNK, CHUNK, H_D, HEAD_D)
    s = jnp.einsum('hpq,bnqhc->bnphc', P['gmlp_ws'], vc, precision=lax.Precision.HIGHEST) + P['gmlp_bs'].T[:, :, None]
    return u * s.reshape(b, t, W_D)


def _modulation(cvec, P):
    m = jnp.dot(jax.nn.silu(cvec), P['w_ada'], precision=lax.Precision.HIGHEST) + P['b_ada']
    return m.reshape(cvec.shape[0], N_MOD, D_MODEL)


def _mix_groups(f, P, s0, rows):
    o = 3 * W_A
    f_rkv = f[..., 0:o]
    fB = f[..., o:o + W_B]
    fC = f[..., o + W_B:o + W_B + 2 * W_C]
    fD = f[..., o + W_B + 2 * W_C:o + W_B + 2 * W_C + 2 * W_D]
    lo = o + W_B + 2 * W_C + 2 * W_D
    f_lora = f[..., lo:lo + (COLS_A - 3 * W_A)]
    yA, s_fin = _rwkv7_group(f_rkv, f_lora, P, s0, rows)
    y = jnp.concatenate([yA, _pool_group(fB, P), _conformer_group(fC, P), _gmlp_group(fD, P)], axis=-1)
    return y, s_fin


def _pack_w_in(w_in):
    o = 3 * W_A
    parts = [w_in[:, 0:o], w_in[:, COLS_A:], w_in[:, o:COLS_A]]
    w = jnp.concatenate(parts, axis=1)
    return jnp.pad(w, ((0, 0), (0, IN_PAD - IN_COLS))).astype(jnp.bfloat16)


def kernel(x_prompt, x_sample, c, state_rwkv, c_ctx, w_ada, b_ada, w_in, mu_prev, mu_next, rwkv_g_up, rwkv_w0, rwkv_w_up, rwkv_a0, rwkv_a_up, rwkv_k_k, rwkv_k_a, rwkv_r_k, rwkv_ln_g, rwkv_ln_b, pool_w, pool_b, pool_scale, conf_conv_w, conf_conv_b, conf_ln_g, conf_ln_b, gmlp_ln_g, gmlp_ln_b, gmlp_ws, gmlp_bs, w_out, ln1_g, ln1_b, ffn_w_up, ffn_conv_w, ffn_conv_b, ffn_w_down, ln2_g, ln2_b):
    bc, tc, d = x_prompt.shape
    bl, tl, _ = x_sample.shape
    nc = bc * tc
    rows = tl // GRID_W
    lay = TokenLayout(bc, tc, bl, tl)
    s_ctx0 = jnp.zeros((bc, 2, H_A, HEAD_A, HEAD_A), jnp.float32)
    cvec = jnp.concatenate([c_ctx[None, :], c], axis=0)
    mods = [_modulation(cvec, {'w_ada': w_ada[l], 'b_ada': b_ada[l]}) for l in range(DEPTH)]
    x = jnp.concatenate([x_prompt.reshape(nc, d), x_sample.reshape(bl * tl, d)], axis=0)
    h = _modulate(x, jnp.stack([mods[0][:, 1], mods[0][:, 0]], axis=1), lay)
    new_states = []
    for l in range(DEPTH):
        P = {
            'mu_prev': mu_prev[l], 'mu_next': mu_next[l], 'g_up': rwkv_g_up[l],
            'w0': rwkv_w0[l], 'w_up': rwkv_w_up[l], 'a0': rwkv_a0[l], 'a_up': rwkv_a_up[l],
            'k_k': rwkv_k_k[l], 'k_a': rwkv_k_a[l], 'r_k': rwkv_r_k[l],
            'rwkv_ln_g': rwkv_ln_g[l], 'rwkv_ln_b': rwkv_ln_b[l],
            'pool_w': pool_w[l], 'pool_b': pool_b[l], 'pool_scale': pool_scale[l],
            'conf_conv_w': conf_conv_w[l], 'conf_conv_b': conf_conv_b[l],
            'conf_ln_g': conf_ln_g[l], 'conf_ln_b': conf_ln_b[l],
            'gmlp_ln_g': gmlp_ln_g[l], 'gmlp_ln_b': gmlp_ln_b[l],
            'gmlp_ws': gmlp_ws[l], 'gmlp_bs': gmlp_bs[l],
        }
        w_in_p = _pack_w_in(w_in[l])
        w_out_b = w_out[l].astype(jnp.bfloat16)
        w_up_b = ffn_w_up[l].astype(jnp.bfloat16)
        w_dn_b = jnp.pad(ffn_w_down[l], ((0, FF_PAD - D_FF), (0, 0))).astype(jnp.bfloat16)

        m = mods[l]

        f = _matmul(h, w_in_p, jnp.float32, 1024, 512, D_MODEL)
        yc, s_fin = _mix_groups(f[:nc].reshape(bc, tc, IN_PAD), P, s_ctx0, None)
        new_states.append(s_fin)
        yl, _ = _mix_groups(f[nc:].reshape(bl, tl, IN_PAD), P, state_rwkv[:, l], rows if l % 2 == 1 else None)
        ycat = jnp.concatenate([yc.reshape(nc, d), yl.reshape(bl * tl, d)], axis=0).astype(jnp.bfloat16)
        y = _matmul(ycat, w_out_b, jnp.float32, 1024, 1024, D_MODEL)
        x, h = _ln_residual(y, x, jnp.stack([m[:, 2], m[:, 4], m[:, 3]], axis=1), ln1_g[l], ln1_b[l], lay, True)

        z = _matmul(h, w_up_b, jnp.bfloat16, 1024, 512, D_MODEL)
        g = _ffn_gate(z, ffn_conv_w[l], ffn_conv_b[l], lay)
        y = _matmul(g, w_dn_b, jnp.float32, 1024, 1024, FF_PAD // 4)
        if l + 1 < DEPTH:
            mn = mods[l + 1]
            x, h = _ln_residual(y, x, jnp.stack([m[:, 5], mn[:, 1], mn[:, 0]], axis=1), ln2_g[l], ln2_b[l], lay, True)
        else:
            (x,) = _ln_residual(y, x, jnp.stack([m[:, 5], m[:, 5], m[:, 5]], axis=1), ln2_g[l], ln2_b[l], lay, False)

    new_state_rwkv = jnp.stack(new_states, axis=1).astype(x_prompt.dtype)
    return (x[:nc].reshape(bc, tc, d), x[nc:].reshape(bl, tl, d), new_state_rwkv)
```

```python
import functools
import math

import numpy as np
import jax
import jax.numpy as jnp
from jax import lax
from jax.experimental import pallas as pl
from jax.experimental.pallas import tpu as pltpu

D_MODEL = 4096
DEPTH = 2
GRID_W = 64
W_A = D_MODEL // 4
W_B = D_MODEL // 4
W_C = D_MODEL // 4
W_D = D_MODEL - W_A - W_B - W_C
HEAD_A = 64
H_A = W_A // HEAD_A
D_DECAY_LORA = max(32, int(round(W_A ** 0.5 * 1.8 / 32)) * 32)
D_AAA_LORA = max(32, int(round(W_A ** 0.5 * 1.8 / 32)) * 32)
D_GATE_LORA = max(32, int(round(W_A ** 0.8 * 0.6 / 32)) * 32)
COLS_A = 3 * W_A + D_DECAY_LORA + D_AAA_LORA + D_GATE_LORA
DECAY_SCALE = math.exp(-0.5)
GN_EPS = 64e-5
POOL_WINDOWS = (2, 4, 8, 16)
POOL_GROUP = W_B // len(POOL_WINDOWS)
CONV_K = 31
CHUNK = 128
H_D = 8
HEAD_D = W_D // H_D
IN_COLS = COLS_A + W_B + 2 * W_C + 2 * W_D
D_FF = ((8 * D_MODEL // 3 + 255) // 256) * 256
FFN_K = 3
ALPHA = (2.0 * DEPTH) ** 0.25
LN_EPS = 1e-5
N_MOD = 6
IN_PAD = ((IN_COLS + 511) // 512) * 512
FF_PAD = ((D_FF + 1023) // 1024) * 1024

LANES = 128
SUBLANES = 8
VMEM_LIMIT_BYTES = 56 * 1024 * 1024

SCAN_TB = 128
SCAN_SUB = 32
SCAN_PARTS = 3
N_KEYVEC = 5

ROW_TILE = 256
FF_CHUNK = 256
CONV_HALO = 16
POOL_HALO = 8
CONV_ROWS = 32
COL_B, COL_CA, COL_CG, COL_DU, COL_DV, LORA_COL = 3, 4, 5, 6, 7, 8


def _mm_kernel(x_ref, w_ref, o_ref, acc_ref, *, nk):
    k = pl.program_id(2)
    part = jnp.dot(x_ref[...], w_ref[...], preferred_element_type=jnp.float32)
    if nk == 1:
        o_ref[...] = part.astype(o_ref.dtype)
    else:
        @pl.when(k == 0)
        def _():
            acc_ref[...] = part

        @pl.when(k > 0)
        def _():
            acc_ref[...] += part

        @pl.when(k == nk - 1)
        def _():
            o_ref[...] = acc_ref[...].astype(o_ref.dtype)


def _matmul(x, w, out_dtype, tm, tn, tk):
    m, kdim = x.shape
    _, n = w.shape
    assert m % tm == 0 and n % tn == 0 and kdim % tk == 0
    nk = kdim // tk
    return pl.pallas_call(
        functools.partial(_mm_kernel, nk=nk),
        grid=(m // tm, n // tn, nk),
        in_specs=[
            pl.BlockSpec((tm, tk), lambda i, j, k: (i, k)),
            pl.BlockSpec((tk, tn), lambda i, j, k: (k, j)),
        ],
        out_specs=pl.BlockSpec((tm, tn), lambda i, j, k: (i, j)),
        out_shape=jax.ShapeDtypeStruct((m, n), out_dtype),
        name="dense_mm",
        scratch_shapes=[pltpu.VMEM((tm, tn), jnp.float32)],
        compiler_params=pltpu.CompilerParams(
            dimension_semantics=("parallel", "parallel", "arbitrary"),
            vmem_limit_bytes=VMEM_LIMIT_BYTES,
        ),
    )(x, w)


def _sel_table():
    c = np.arange(2 * LANES)[:, None]
    n = np.arange(2 * LANES)[None, :]
    e = np.zeros((SCAN_SUB // 2, 2 * LANES, 2 * LANES), np.float32)
    for p in range(SCAN_SUB // 2):
        t = 2 * p + n // LANES
        e[p] = (((c % LANES) < SCAN_PARTS * SCAN_SUB) & ((c % SCAN_SUB) == t)
                & ((c // LANES) == ((n % LANES) // HEAD_A)))
    return jnp.asarray(e, jnp.bfloat16)


def _split3_rows(x):
    hi = x.astype(jnp.bfloat16).astype(jnp.float32)
    r1 = x - hi
    mid = r1.astype(jnp.bfloat16).astype(jnp.float32)
    lo = r1 - mid
    return jnp.concatenate([hi, mid, lo, jnp.zeros_like(x)], axis=0)


def _scan_kernel(*refs, nblk):
    key_refs = (refs[0:5], refs[6:11])
    v_refs = (refs[5], refs[11])
    e_ref, s0_ref = refs[12], refs[13]
    y_refs = (refs[14], refs[15])
    sfin_ref, st_ref, lhs_ref = refs[16], refs[17], refs[18]
    tb = pl.program_id(2)

    @pl.when(tb == 0)
    def _():
        st_ref[...] = s0_ref[0, :, 0]

    lane = lax.broadcasted_iota(jnp.int32, (SCAN_SUB, LANES), 1)
    first_head = lane < HEAD_A

    def build_lhs(offs):
        for c in range(2):
            xs = [ref[0, pl.ds(offs[c], SCAN_SUB), :] for ref in key_refs[c]]
            for pi, (i1, i2) in enumerate(((0, 1), (2, 3), (4, None))):
                x1 = xs[i1]
                xr1 = pltpu.roll(x1, HEAD_A, axis=1)
                if i2 is None:
                    comb_a, comb_b = x1, xr1
                else:
                    x2 = xs[i2]
                    xr2 = pltpu.roll(x2, HEAD_A, axis=1)
                    comb_a = jnp.where(first_head, x1, xr2)
                    comb_b = jnp.where(first_head, xr1, x2)
                mt_a = _split3_rows(comb_a).T.astype(jnp.bfloat16)
                mt_b = _split3_rows(comb_b).T.astype(jnp.bfloat16)
                r0 = (c * N_KEYVEC + 2 * pi) * HEAD_A
                lhs_ref[r0:r0 + HEAD_A, 0:LANES] = mt_a[0:HEAD_A]
                lhs_ref[r0:r0 + HEAD_A, LANES:2 * LANES] = mt_b[0:HEAD_A]
                if i2 is not None:
                    lhs_ref[r0 + HEAD_A:r0 + 2 * HEAD_A, 0:LANES] = mt_a[HEAD_A:2 * HEAD_A]
                    lhs_ref[r0 + HEAD_A:r0 + 2 * HEAD_A, LANES:2 * LANES] = mt_b[HEAD_A:2 * HEAD_A]

    def sub(q, carry):
        hs = list(carry)
        offs = (pl.multiple_of(q * SCAN_SUB, SCAN_SUB),
                pl.multiple_of(SCAN_TB - (q + 1) * SCAN_SUB, SCAN_SUB))
        build_lhs(offs)
        vs = [v_refs[c][0, pl.ds(offs[c], SCAN_SUB), :] for c in range(2)]
        ys = ([], [])
        half = N_KEYVEC * HEAD_A
        npairs = SCAN_SUB // 2
        for p in range(npairs):
            tiles = (jnp.dot(lhs_ref[0:half, :], e_ref[p], preferred_element_type=jnp.float32),
                     jnp.dot(lhs_ref[half:2 * half, :], e_ref[npairs - 1 - p], preferred_element_type=jnp.float32))
            for s in range(2):
                u = 2 * p + s
                for c in range(2):
                    col = s if c == 0 else 1 - s
                    row = u if c == 0 else SCAN_SUB - 1 - u
                    kk_t, w_t, b_t, kd_t, r_t = (
                        tiles[c][k * HEAD_A:(k + 1) * HEAD_A, col * LANES:(col + 1) * LANES]
                        for k in range(N_KEYVEC))
                    h = hs[c]
                    sa = -jnp.sum(h * kk_t, axis=0, keepdims=True)
                    h = h * w_t + sa * b_t + vs[c][row:row + 1, :] * kd_t
                    ys[c].append(jnp.sum(h * r_t, axis=0, keepdims=True))
                    hs[c] = h
        y_refs[0][0, pl.ds(offs[0], SCAN_SUB), :] = jnp.concatenate(ys[0], axis=0)
        y_refs[1][0, pl.ds(offs[1], SCAN_SUB), :] = jnp.concatenate(ys[1][::-1], axis=0)
        return tuple(hs)

    hf, hb = lax.fori_loop(0, SCAN_TB // SCAN_SUB, sub, (st_ref[0], st_ref[1]))
    st_ref[0] = hf
    st_ref[1] = hb

    @pl.when(tb == nblk - 1)
    def _():
        sfin_ref[0, 0, 0] = hf
        sfin_ref[0, 1, 0] = hb


def _rwkv_scan(r, v, kk, w2, kd2, b2, s0):
    bsz, t_len, _ = r.shape
    npair = H_A // 2
    nblk = t_len // SCAN_TB
    s0p = s0.reshape(bsz, 2, npair, 2, HEAD_A, HEAD_A).transpose(0, 1, 2, 5, 3, 4)
    s0p = s0p.reshape(bsz, 2, npair, HEAD_A, 2 * HEAD_A)
    ins = [kk, w2[0], b2[0], kd2[0], r, v, kk, w2[1], b2[1], kd2[1], r, v]
    blk3 = (1, SCAN_TB, LANES)
    idx3 = lambda b, p, t: (b, t, p)
    idx3b = lambda b, p, t: (b, nblk - 1 - t, p)
    st_spec = pl.BlockSpec((1, 2, 1, HEAD_A, LANES), lambda b, p, t: (b, 0, p, 0, 0))
    e_tab = _sel_table()
    in_specs = ([pl.BlockSpec(blk3, idx3)] * 6 + [pl.BlockSpec(blk3, idx3b)] * 6
                + [pl.BlockSpec(e_tab.shape, lambda b, p, t: (0, 0, 0)), st_spec])
    out_specs = [pl.BlockSpec(blk3, idx3), pl.BlockSpec(blk3, idx3b), st_spec]
    out_shape = [
        jax.ShapeDtypeStruct((bsz, t_len, W_A), jnp.float32),
        jax.ShapeDtypeStruct((bsz, t_len, W_A), jnp.float32),
        jax.ShapeDtypeStruct((bsz, 2, npair, HEAD_A, LANES), jnp.float32),
    ]
    yf, yb, sfin = pl.pallas_call(
        functools.partial(_scan_kernel, nblk=nblk),
        grid=(bsz, npair, nblk),
        in_specs=in_specs,
        out_specs=out_specs,
        out_shape=out_shape,
        name="rwkv_scan",
        scratch_shapes=[pltpu.VMEM((2, HEAD_A, LANES), jnp.float32),
                        pltpu.VMEM((2 * N_KEYVEC * HEAD_A, 2 * LANES), jnp.bfloat16)],
        compiler_params=pltpu.CompilerParams(
            dimension_semantics=("parallel", "parallel", "arbitrary"),
            vmem_limit_bytes=VMEM_LIMIT_BYTES,
        ),
    )(*ins, e_tab, s0p)
    sfin = sfin.reshape(bsz, 2, npair, HEAD_A, 2, HEAD_A).transpose(0, 1, 2, 4, 5, 3)
    return yf, yb, sfin.reshape(bsz, 2, H_A, HEAD_A, HEAD_A)


class TokenLayout:
    def __init__(self, bc, tc, bl, tl):
        assert tc % ROW_TILE == 0 and tl % ROW_TILE == 0
        self.n_rows = bc * tc + bl * tl
        self.n_tiles = self.n_rows // ROW_TILE
        self.ctx_tiles = bc * tc // ROW_TILE
        self.ctx_seq_tiles = tc // ROW_TILE
        self.lat_seq_tiles = tl // ROW_TILE

    def group(self, i):
        return jnp.where(i < self.ctx_tiles, 0, 1 + (i - self.ctx_tiles) // self.lat_seq_tiles)

    def pos_in_seq(self, i):
        return jnp.where(i < self.ctx_tiles, i % self.ctx_seq_tiles, (i - self.ctx_tiles) % self.lat_seq_tiles)

    def seq_tiles(self, i):
        return jnp.where(i < self.ctx_tiles, self.ctx_seq_tiles, self.lat_seq_tiles)


def _modulate_kernel(x_ref, m_ref, h_ref):
    x = x_ref[...]
    h_ref[...] = (x * (1 + m_ref[0, 0:1, :]) + m_ref[0, 1:2, :]).astype(h_ref.dtype)


def _modulate(x, m2, lay):
    n, d = x.shape
    return pl.pallas_call(
        _modulate_kernel,
        grid=(lay.n_tiles,),
        in_specs=[pl.BlockSpec((ROW_TILE, d), lambda i: (i, 0)),
                  pl.BlockSpec((1, 2, d), lambda i: (lay.group(i), 0, 0))],
        out_specs=pl.BlockSpec((ROW_TILE, d), lambda i: (i, 0)),
        out_shape=jax.ShapeDtypeStruct((n, d), jnp.bfloat16),
        name="modulate",
        compiler_params=pltpu.CompilerParams(dimension_semantics=("parallel",),
                                             vmem_limit_bytes=VMEM_LIMIT_BYTES),
    )(x, m2)


def _ln_residual_kernel(y_ref, x_ref, m_ref, g_ref, b_ref, xo_ref, *h_ref):
    v = ALPHA * x_ref[...] + m_ref[0, 0:1, :] * y_ref[...]
    vc = v - jnp.mean(v, -1, keepdims=True)
    var = jnp.mean(vc * vc, -1, keepdims=True)
    xn = (vc * lax.rsqrt(var + LN_EPS)) * g_ref[...] + b_ref[...]
    xo_ref[...] = xn
    if h_ref:
        h_ref[0][...] = (xn * (1 + m_ref[0, 1:2, :]) + m_ref[0, 2:3, :]).astype(jnp.bfloat16)


def _ln_residual(y, x, m3, g, b, lay, with_h):
    n, d = x.shape
    row = pl.BlockSpec((ROW_TILE, d), lambda i: (i, 0))
    vec = pl.BlockSpec((1, d), lambda i: (0, 0))
    out_shape = [jax.ShapeDtypeStruct((n, d), jnp.float32)]
    out_specs = [row]
    if with_h:
        out_shape.append(jax.ShapeDtypeStruct((n, d), jnp.bfloat16))
        out_specs.append(row)
    return pl.pallas_call(
        _ln_residual_kernel,
        grid=(lay.n_tiles,),
        in_specs=[row, row, pl.BlockSpec((1, 3, d), lambda i: (lay.group(i), 0, 0)), vec, vec],
        out_specs=out_specs,
        out_shape=out_shape,
        name="ln_residual",
        compiler_params=pltpu.CompilerParams(dimension_semantics=("parallel",),
                                             vmem_limit_bytes=VMEM_LIMIT_BYTES),
    )(y, x, m3, g.reshape(1, d), b.reshape(1, d))


def _ffn_gate_kernel(zc_ref, zp_ref, zn_ref, w_ref, b_ref, o_ref, *, lay):
    i = pl.program_id(0)
    pos = lay.pos_in_seq(i)
    has_prev = pos > 0
    has_next = pos < lay.seq_tiles(i) - 1
    row = lax.broadcasted_iota(jnp.int32, (ROW_TILE, FF_CHUNK), 0)
    n_chunks = D_FF // FF_CHUNK

    def conv(col):
        sl = pl.ds(col, FF_CHUNK)
        z = zc_ref[:, sl].astype(jnp.float32)
        zprev_row = jnp.where(has_prev, zp_ref[SUBLANES - 1:SUBLANES, sl].astype(jnp.float32), 0.0)
        znext_row = jnp.where(has_next, zn_ref[0:1, sl].astype(jnp.float32), 0.0)
        zprev = jnp.where(row == 0, zprev_row, pltpu.roll(z, 1, axis=0))
        znext = jnp.where(row == ROW_TILE - 1, znext_row, pltpu.roll(z, ROW_TILE - 1, axis=0))
        return zprev * w_ref[0:1, sl] + b_ref[0:1, sl] + z * w_ref[1:2, sl] + znext * w_ref[2:3, sl]

    def body(j, carry):
        ca = pl.multiple_of(j * FF_CHUNK, FF_CHUNK)
        cb = pl.multiple_of(D_FF + j * FF_CHUNK, FF_CHUNK)
        a = conv(ca)
        bv = conv(cb)
        o_ref[:, pl.ds(ca, FF_CHUNK)] = (a * jax.nn.sigmoid(a) * bv).astype(o_ref.dtype)
        return carry

    lax.fori_loop(0, n_chunks, body, 0)
    o_ref[:, D_FF:FF_PAD] = jnp.zeros((ROW_TILE, FF_PAD - D_FF), o_ref.dtype)


def _ffn_gate(z, conv_w, conv_b, lay):
    n, c2 = z.shape
    rt8 = ROW_TILE // SUBLANES
    last8 = n // SUBLANES - 1
    return pl.pallas_call(
        functools.partial(_ffn_gate_kernel, lay=lay),
        grid=(lay.n_tiles,),
        in_specs=[pl.BlockSpec((ROW_TILE, c2), lambda i: (i, 0)),
                  pl.BlockSpec((SUBLANES, c2), lambda i: (jnp.maximum(i * rt8 - 1, 0), 0)),
                  pl.BlockSpec((SUBLANES, c2), lambda i: (jnp.minimum((i + 1) * rt8, last8), 0)),
                  pl.BlockSpec((3, c2), lambda i: (0, 0)),
                  pl.BlockSpec((1, c2), lambda i: (0, 0))],
        out_specs=pl.BlockSpec((ROW_TILE, FF_PAD), lambda i: (i, 0)),
        out_shape=jax.ShapeDtypeStruct((n, FF_PAD), jnp.bfloat16),
        name="ffn_gate",
        compiler_params=pltpu.CompilerParams(dimension_semantics=("parallel",),
                                             vmem_limit_bytes=VMEM_LIMIT_BYTES),
    )(z, z, z, conv_w, conv_b.reshape(1, c2))


def _cparams():
    return pltpu.CompilerParams(dimension_semantics=("parallel",), vmem_limit_bytes=VMEM_LIMIT_BYTES)


def _halo_specs(n_rows, halo, width, col):
    per_tile = ROW_TILE // halo
    last = n_rows // halo - 1
    prev = pl.BlockSpec((halo, width), lambda i: (jnp.maximum(i * per_tile - 1, 0), col))
    nxt = pl.BlockSpec((halo, width), lambda i: (jnp.minimum((i + 1) * per_tile, last), col))
    return prev, nxt


def _seq_edges(lay):
    i = pl.program_id(0)
    pos = lay.pos_in_seq(i)
    return pos, pos > 0, pos < lay.seq_tiles(i) - 1


def _row_layer_norm(x, g, b):
    xc = x - jnp.mean(x, -1, keepdims=True)
    var = jnp.mean(xc * xc, -1, keepdims=True)
    return (xc * lax.rsqrt(var + LN_EPS)) * g + b


def _conformer_kernel(a_ref, g_ref, ap_ref, gp_ref, an_ref, gn_ref, w_ref, b_ref, lg_ref, lb_ref,
                      o_ref, hext_ref, y_ref, *, lay):
    _, has_prev, has_next = _seq_edges(lay)
    glu = lambda a, g: a * jax.nn.sigmoid(g)
    hext_ref[0:CONV_HALO] = jnp.where(has_prev, glu(ap_ref[...], gp_ref[...]), 0.0)
    hext_ref[CONV_HALO:CONV_HALO + ROW_TILE] = glu(a_ref[...], g_ref[...])
    hext_ref[CONV_HALO + ROW_TILE:] = jnp.where(has_next, glu(an_ref[...], gn_ref[...]), 0.0)
    first = CONV_HALO - CONV_K // 2
    for rc in range(ROW_TILE // CONV_ROWS):
        base = first + rc * CONV_ROWS
        acc = hext_ref[base:base + CONV_ROWS] * w_ref[0:1] + b_ref[...]
        for j in range(1, CONV_K):
            acc = acc + hext_ref[base + j:base + j + CONV_ROWS] * w_ref[j:j + 1]
        y_ref[rc * CONV_ROWS:(rc + 1) * CONV_ROWS] = acc
    y = _row_layer_norm(y_ref[...], lg_ref[...], lb_ref[...])
    o_ref[...] = (y * jax.nn.sigmoid(y)).astype(o_ref.dtype)


def _conformer_group(f, conv_w, conv_b, ln_g, ln_b, lay):
    n = f.shape[0]
    cur = lambda col: pl.BlockSpec((ROW_TILE, W_C), lambda i: (i, col))
    pa, na = _halo_specs(n, CONV_HALO, W_C, COL_CA)
    pg, ng = _halo_specs(n, CONV_HALO, W_C, COL_CG)
    vec = pl.BlockSpec((1, W_C), lambda i: (0, 0))
    return pl.pallas_call(
        functools.partial(_conformer_kernel, lay=lay),
        grid=(lay.n_tiles,),
        in_specs=[cur(COL_CA), cur(COL_CG), pa, pg, na, ng,
                  pl.BlockSpec((CONV_K, W_C), lambda i: (0, 0)), vec, vec, vec],
        out_specs=pl.BlockSpec((ROW_TILE, W_C), lambda i: (i, 0)),
        out_shape=jax.ShapeDtypeStruct((n, W_C), jnp.bfloat16),
        name="conformer_group",
        scratch_shapes=[pltpu.VMEM((ROW_TILE + 2 * CONV_HALO, W_C), jnp.float32),
                        pltpu.VMEM((ROW_TILE, W_C), jnp.float32)],
        compiler_params=_cparams(),
    )(f, f, f, f, f, f, conv_w, conv_b.reshape(1, W_C), ln_g.reshape(1, W_C), ln_b.reshape(1, W_C))


def _gmlp_kernel(u_ref, v_ref, lg_ref, lb_ref, ws_ref, bias_ref, o_ref):
    v = _row_layer_norm(jax.nn.gelu(v_ref[...]), lg_ref[...], lb_ref[...]).astype(jnp.bfloat16)
    for ch in range(ROW_TILE // CHUNK):
        rows = slice(ch * CHUNK, (ch + 1) * CHUNK)
        for h in range(H_D):
            cols = slice(h * HEAD_D, (h + 1) * HEAD_D)
            s = jnp.dot(ws_ref[h], v[rows, cols], preferred_element_type=jnp.float32) + bias_ref[:, cols]
            o_ref[rows, cols] = (jax.nn.gelu(u_ref[rows, cols]) * s).astype(o_ref.dtype)


def _gmlp_group(f, ln_g, ln_b, ws, bs, lay):
    n = f.shape[0]
    vec = pl.BlockSpec((1, W_D), lambda i: (0, 0))
    bias = jnp.repeat(bs.T, HEAD_D, axis=1)
    return pl.pallas_call(
        _gmlp_kernel,
        grid=(lay.n_tiles,),
        in_specs=[pl.BlockSpec((ROW_TILE, W_D), lambda i: (i, COL_DU)),
                  pl.BlockSpec((ROW_TILE, W_D), lambda i: (i, COL_DV)),
                  vec, vec,
                  pl.BlockSpec((H_D, CHUNK, CHUNK), lambda i: (0, 0, 0)),
                  pl.BlockSpec((CHUNK, W_D), lambda i: (0, 0))],
        out_specs=pl.BlockSpec((ROW_TILE, W_D), lambda i: (i, 0)),
        out_shape=jax.ShapeDtypeStruct((n, W_D), jnp.bfloat16),
        name="gmlp_group",
        compiler_params=_cparams(),
    )(f, f, ln_g.reshape(1, W_D), ln_b.reshape(1, W_D), ws.astype(jnp.bfloat16), bias)


def _pool_kernel(x_ref, xp_ref, xn_ref, w_ref, b_ref, sc_ref, o_ref, hext_ref, *, lay):
    pos, has_prev, has_next = _seq_edges(lay)
    hext_ref[0:POOL_HALO] = jnp.where(has_prev, xp_ref[...], 0.0)
    hext_ref[POOL_HALO:POOL_HALO + ROW_TILE] = x_ref[...]
    hext_ref[POOL_HALO + ROW_TILE:] = jnp.where(has_next, xn_ref[...], 0.0)
    t = pos * ROW_TILE + lax.broadcasted_iota(jnp.int32, (ROW_TILE, POOL_GROUP), 0)
    t_len = lay.seq_tiles(pl.program_id(0)) * ROW_TILE
    for gi, win in enumerate(POOL_WINDOWS):
        cols = slice(gi * POOL_GROUP, (gi + 1) * POOL_GROUP)
        half = win // 2
        ssum = hext_ref[POOL_HALO - half:POOL_HALO - half + ROW_TILE, cols]
        for dlt in range(-half + 1, half):
            ssum = ssum + hext_ref[POOL_HALO + dlt:POOL_HALO + dlt + ROW_TILE, cols]
        cnt = (jnp.minimum(t + half, t_len) - jnp.maximum(t - half, 0)).astype(jnp.float32)
        pooled = (ssum / cnt - x_ref[:, cols]).astype(jnp.bfloat16)
        y = jnp.dot(pooled, w_ref[gi], preferred_element_type=jnp.float32) + b_ref[:, cols]
        o_ref[:, cols] = (y * sc_ref[:, cols]).astype(o_ref.dtype)


def _pool_group(f, pool_w, pool_b, pool_scale, lay):
    n = f.shape[0]
    prev, nxt = _halo_specs(n, POOL_HALO, W_B, COL_B)
    vec = pl.BlockSpec((1, W_B), lambda i: (0, 0))
    return pl.pallas_call(
        functools.partial(_pool_kernel, lay=lay),
        grid=(lay.n_tiles,),
        in_specs=[pl.BlockSpec((ROW_TILE, W_B), lambda i: (i, COL_B)), prev, nxt,
                  pl.BlockSpec(pool_w.shape, lambda i: (0, 0, 0)), vec, vec],
        out_specs=pl.BlockSpec((ROW_TILE, W_B), lambda i: (i, 0)),
        out_shape=jax.ShapeDtypeStruct((n, W_B), jnp.bfloat16),
        name="pool_group",
        scratch_shapes=[pltpu.VMEM((ROW_TILE + 2 * POOL_HALO, W_B), jnp.float32)],
        compiler_params=_cparams(),
    )(f, f, f, pool_w.astype(jnp.bfloat16), pool_b.reshape(1, W_B), pool_scale.reshape(1, W_B))


def _mm4_kernel(y0, y1, y2, y3, w_ref, o_ref):
    wk = w_ref.shape[0] // 4
    acc = jnp.dot(y0[...], w_ref[0:wk, :], preferred_element_type=jnp.float32)
    for k, y in enumerate((y1, y2, y3), start=1):
        acc = acc + jnp.dot(y[...], w_ref[k * wk:(k + 1) * wk, :], preferred_element_type=jnp.float32)
    o_ref[...] = acc


def _matmul4(ys, w, tm, tn):
    m = ys[0].shape[0]
    kdim, n = w.shape
    yspec = pl.BlockSpec((tm, kdim // 4), lambda i, j: (i, 0))
    return pl.pallas_call(
        _mm4_kernel,
        grid=(m // tm, n // tn),
        in_specs=[yspec] * 4 + [pl.BlockSpec((kdim, tn), lambda i, j: (0, j))],
        out_specs=pl.BlockSpec((tm, tn), lambda i, j: (i, j)),
        out_shape=jax.ShapeDtypeStruct((m, n), jnp.float32),
        name="dense_mm4",
        compiler_params=pltpu.CompilerParams(dimension_semantics=("parallel", "parallel"),
                                             vmem_limit_bytes=VMEM_LIMIT_BYTES),
    )(*ys, w)


def _shift_prev(f):
    return jnp.pad(f, ((0, 0), (1, 0), (0, 0)))[:, :-1]


def _shift_next(f):
    return jnp.pad(f, ((0, 0), (0, 1), (0, 0)))[:, 1:]


def _grid_transpose(z, rows, cols):
    b, t, ch = z.shape
    return z.reshape(b, rows, cols, ch).transpose(0, 2, 1, 3).reshape(b, t, ch)


def _rwkv7_group(f_rkv, f_lora, P, s0, rows):
    fA = jnp.concatenate([f_rkv, f_lora], axis=-1)
    if rows is not None:
        fA = _grid_transpose(fA, rows, GRID_W)
    fA = fA + P['mu_prev'] * (_shift_prev(fA) - fA) + P['mu_next'] * (_shift_next(fA) - fA)
    b, t, _ = fA.shape
    r = fA[..., 0:W_A]
    k = fA[..., W_A:2 * W_A]
    v = fA[..., 2 * W_A:3 * W_A]
    o = 3 * W_A
    xw = fA[..., o:o + D_DECAY_LORA]
    xa = fA[..., o + D_DECAY_LORA:o + D_DECAY_LORA + D_AAA_LORA]
    xg = fA[..., o + D_DECAY_LORA + D_AAA_LORA:COLS_A]

    def heads(z):
        return z.reshape(b, t, H_A, HEAD_A)

    hp = lax.Precision.HIGHEST
    g = jnp.dot(jax.nn.sigmoid(xg), P['g_up'], precision=hp)
    kk = heads(k * P['k_k'])
    kk = kk * lax.rsqrt(jnp.sum(kk * kk, -1, keepdims=True) + 1e-12)
    kk = kk.reshape(b, t, W_A)
    wx = jnp.tanh(xw)
    ws, kds, bs, bonus = [], [], [], 0.0
    for d in range(2):
        w = jnp.exp(-DECAY_SCALE * jax.nn.sigmoid(P['w0'][d] + jnp.dot(wx, P['w_up'][d], precision=hp)))
        a = jax.nn.sigmoid(P['a0'][d] + jnp.dot(xa, P['a_up'][d], precision=hp))
        kd = k * (1 + (a - 1) * P['k_a'])
        ws.append(w)
        kds.append(kd)
        bs.append(kk * a)
        bonus = bonus + (jnp.sum(heads(r) * heads(kd) * P['r_k'], -1, keepdims=True) * heads(v)).reshape(b, t, W_A)
    yf, yb, s_fin = _rwkv_scan(r, v, kk, jnp.stack(ws), jnp.stack(kds), jnp.stack(bs), s0)
    y = heads(yf + yb)
    yc = y - jnp.mean(y, -1, keepdims=True)
    y = yc * lax.rsqrt(jnp.mean(yc * yc, -1, keepdims=True) + GN_EPS)
    y = y.reshape(b, t, W_A) * P['rwkv_ln_g'] + P['rwkv_ln_b'] + bonus
    out = y * g
    if rows is not None:
        out = _grid_transpose(out, GRID_W, rows)
    return out, s_fin


def _modulation(cvec, P):
    m = jnp.dot(jax.nn.silu(cvec), P['w_ada'], precision=lax.Precision.HIGHEST) + P['b_ada']
    return m.reshape(cvec.shape[0], N_MOD, D_MODEL)


def _rwkv_path(f, P, s0, rows):
    lo = LORA_COL * W_A
    return _rwkv7_group(f[..., 0:3 * W_A], f[..., lo:lo + (COLS_A - 3 * W_A)], P, s0, rows)


def _pack_w_in(w_in):
    o = 3 * W_A
    parts = [w_in[:, 0:o], w_in[:, COLS_A:], w_in[:, o:COLS_A]]
    w = jnp.concatenate(parts, axis=1)
    return jnp.pad(w, ((0, 0), (0, IN_PAD - IN_COLS))).astype(jnp.bfloat16)


def kernel(x_prompt, x_sample, c, state_rwkv, c_ctx, w_ada, b_ada, w_in, mu_prev, mu_next, rwkv_g_up, rwkv_w0, rwkv_w_up, rwkv_a0, rwkv_a_up, rwkv_k_k, rwkv_k_a, rwkv_r_k, rwkv_ln_g, rwkv_ln_b, pool_w, pool_b, pool_scale, conf_conv_w, conf_conv_b, conf_ln_g, conf_ln_b, gmlp_ln_g, gmlp_ln_b, gmlp_ws, gmlp_bs, w_out, ln1_g, ln1_b, ffn_w_up, ffn_conv_w, ffn_conv_b, ffn_w_down, ln2_g, ln2_b):
    bc, tc, d = x_prompt.shape
    bl, tl, _ = x_sample.shape
    nc = bc * tc
    rows = tl // GRID_W
    lay = TokenLayout(bc, tc, bl, tl)
    s_ctx0 = jnp.zeros((bc, 2, H_A, HEAD_A, HEAD_A), jnp.float32)
    cvec = jnp.concatenate([c_ctx[None, :], c], axis=0)
    mods = [_modulation(cvec, {'w_ada': w_ada[l], 'b_ada': b_ada[l]}) for l in range(DEPTH)]
    x = jnp.concatenate([x_prompt.reshape(nc, d), x_sample.reshape(bl * tl, d)], axis=0)
    h = _modulate(x, jnp.stack([mods[0][:, 1], mods[0][:, 0]], axis=1), lay)
    new_states = []
    for l in range(DEPTH):
        P = {
            'mu_prev': mu_prev[l], 'mu_next': mu_next[l], 'g_up': rwkv_g_up[l],
            'w0': rwkv_w0[l], 'w_up': rwkv_w_up[l], 'a0': rwkv_a0[l], 'a_up': rwkv_a_up[l],
            'k_k': rwkv_k_k[l], 'k_a': rwkv_k_a[l], 'r_k': rwkv_r_k[l],
            'rwkv_ln_g': rwkv_ln_g[l], 'rwkv_ln_b': rwkv_ln_b[l],
        }
        w_in_p = _pack_w_in(w_in[l])
        w_out_b = w_out[l].astype(jnp.bfloat16)
        w_up_b = ffn_w_up[l].astype(jnp.bfloat16)
        w_dn_b = jnp.pad(ffn_w_down[l], ((0, FF_PAD - D_FF), (0, 0))).astype(jnp.bfloat16)

        m = mods[l]

        f = _matmul(h, w_in_p, jnp.float32, 1024, 512, D_MODEL)
        yc, s_fin = _rwkv_path(f[:nc].reshape(bc, tc, IN_PAD), P, s_ctx0, None)
        new_states.append(s_fin)
        yl, _ = _rwkv_path(f[nc:].reshape(bl, tl, IN_PAD), P, state_rwkv[:, l], rows if l % 2 == 1 else None)
        y_a = jnp.concatenate([yc.reshape(nc, W_A), yl.reshape(bl * tl, W_A)], axis=0).astype(jnp.bfloat16)
        y_b = _pool_group(f, pool_w[l], pool_b[l], pool_scale[l], lay)
        y_c = _conformer_group(f, conf_conv_w[l], conf_conv_b[l], conf_ln_g[l], conf_ln_b[l], lay)
        y_d = _gmlp_group(f, gmlp_ln_g[l], gmlp_ln_b[l], gmlp_ws[l], gmlp_bs[l], lay)
        y = _matmul4([y_a, y_b, y_c, y_d], w_out_b, 1024, 1024)
        x, h = _ln_residual(y, x, jnp.stack([m[:, 2], m[:, 4], m[:, 3]], axis=1), ln1_g[l], ln1_b[l], lay, True)

        z = _matmul(h, w_up_b, jnp.bfloat16, 1024, 512, D_MODEL)
        g = _ffn_gate(z, ffn_conv_w[l], ffn_conv_b[l], lay)
        y = _matmul(g, w_dn_b, jnp.float32, 1024, 1024, FF_PAD // 4)
        if l + 1 < DEPTH:
            mn = mods[l + 1]
            x, h = _ln_residual(y, x, jnp.stack([m[:, 5], mn[:, 1], mn[:, 0]], axis=1), ln2_g[l], ln2_b[l], lay, True)
        else:
            (x,) = _ln_residual(y, x, jnp.stack([m[:, 5], m[:, 5], m[:, 5]], axis=1), ln2_g[l], ln2_b[l], lay, False)

    new_state_rwkv = jnp.stack(new_states, axis=1).astype(x_prompt.dtype)
    return (x[:nc].reshape(bc, tc, d), x[nc:].reshape(bl, tl, d), new_state_rwkv)
```

```python
import functools
import math

import numpy as np
import jax
import jax.numpy as jnp
from jax import lax
from jax.experimental import pallas as pl
from jax.experimental.pallas import tpu as pltpu

D_MODEL = 4096
DEPTH = 2
GRID_W = 64
W_A = D_MODEL // 4
W_B = D_MODEL // 4
W_C = D_MODEL // 4
W_D = D_MODEL - W_A - W_B - W_C
HEAD_A = 64
H_A = W_A // HEAD_A
D_DECAY_LORA = max(32, int(round(W_A ** 0.5 * 1.8 / 32)) * 32)
D_AAA_LORA = max(32, int(round(W_A ** 0.5 * 1.8 / 32)) * 32)
D_GATE_LORA = max(32, int(round(W_A ** 0.8 * 0.6 / 32)) * 32)
COLS_A = 3 * W_A + D_DECAY_LORA + D_AAA_LORA + D_GATE_LORA
DECAY_SCALE = math.exp(-0.5)
GN_EPS = 64e-5
POOL_WINDOWS = (2, 4, 8, 16)
POOL_GROUP = W_B // len(POOL_WINDOWS)
CONV_K = 31
CHUNK = 128
H_D = 8
HEAD_D = W_D // H_D
IN_COLS = COLS_A + W_B + 2 * W_C + 2 * W_D
D_FF = ((8 * D_MODEL // 3 + 255) // 256) * 256
FFN_K = 3
ALPHA = (2.0 * DEPTH) ** 0.25
LN_EPS = 1e-5
N_MOD = 6
IN_PAD = ((IN_COLS + 511) // 512) * 512
FF_PAD = ((D_FF + 1023) // 1024) * 1024

LANES = 128
SUBLANES = 8
VMEM_LIMIT_BYTES = 56 * 1024 * 1024

SCAN_TB = 128
SCAN_SUB = 32
SCAN_PARTS = 3
N_KEYVEC = 5
SCAN_TRIP = 4

ROW_TILE = 256
FF_CHUNK = 256
CONV_HALO = 16
POOL_HALO = 8
CONV_ROWS = 32
COL_B, COL_CA, COL_CG, COL_DU, COL_DV, LORA_COL = 3, 4, 5, 6, 7, 8
LORA_W = D_DECAY_LORA + D_AAA_LORA + D_GATE_LORA
LORA_PAD = IN_PAD - LORA_COL * W_A
N_PREP_OUT = 11


def _mm_kernel(x_ref, w_ref, o_ref, acc_ref, *, nk):
    k = pl.program_id(2)
    part = jnp.dot(x_ref[...], w_ref[...], preferred_element_type=jnp.float32)
    if nk == 1:
        o_ref[...] = part.astype(o_ref.dtype)
    else:
        @pl.when(k == 0)
        def _():
            acc_ref[...] = part

        @pl.when(k > 0)
        def _():
            acc_ref[...] += part

        @pl.when(k == nk - 1)
        def _():
            o_ref[...] = acc_ref[...].astype(o_ref.dtype)


def _matmul(x, w, out_dtype, tm, tn, tk):
    m, kdim = x.shape
    _, n = w.shape
    assert m % tm == 0 and n % tn == 0 and kdim % tk == 0
    nk = kdim // tk
    return pl.pallas_call(
        functools.partial(_mm_kernel, nk=nk),
        grid=(m // tm, n // tn, nk),
        in_specs=[
            pl.BlockSpec((tm, tk), lambda i, j, k: (i, k)),
            pl.BlockSpec((tk, tn), lambda i, j, k: (k, j)),
        ],
        out_specs=pl.BlockSpec((tm, tn), lambda i, j, k: (i, j)),
        out_shape=jax.ShapeDtypeStruct((m, n), out_dtype),
        name="dense_mm",
        scratch_shapes=[pltpu.VMEM((tm, tn), jnp.float32)],
        compiler_params=pltpu.CompilerParams(
            dimension_semantics=("parallel", "parallel", "arbitrary"),
            vmem_limit_bytes=VMEM_LIMIT_BYTES,
        ),
    )(x, w)


def _sel_table():
    c = np.arange(2 * LANES)[:, None]
    n = np.arange(2 * LANES)[None, :]
    e = np.zeros((SCAN_SUB // 2, 2 * LANES, 2 * LANES), np.float32)
    for p in range(SCAN_SUB // 2):
        t = 2 * p + n // LANES
        e[p] = (((c % LANES) < SCAN_PARTS * SCAN_SUB) & ((c % SCAN_SUB) == t)
                & ((c // LANES) == ((n % LANES) // HEAD_A)))
    return jnp.asarray(e, jnp.bfloat16)


def _split3_rows(x):
    hi = x.astype(jnp.bfloat16).astype(jnp.float32)
    r1 = x - hi
    mid = r1.astype(jnp.bfloat16).astype(jnp.float32)
    lo = r1 - mid
    return jnp.concatenate([hi, mid, lo, jnp.zeros_like(x)], axis=0)


def _scan_kernel(*refs, nblk):
    key_refs = (refs[0:5], refs[6:11])
    v_refs = (refs[5], refs[11])
    e_ref, s0_ref = refs[12], refs[13]
    y_refs = (refs[14], refs[15])
    sfin_ref, st_ref, lhs_ref = refs[16], refs[17], refs[18]
    tb = pl.program_id(2)

    @pl.when(tb == 0)
    def _():
        st_ref[...] = s0_ref[0, :, 0]

    lane = lax.broadcasted_iota(jnp.int32, (SCAN_SUB, LANES), 1)
    first_head = lane < HEAD_A

    half = N_KEYVEC * HEAD_A
    npairs = SCAN_SUB // 2

    def build_lhs(slot, offs):
        for c in range(2):
            xs = [ref[pl.ds(offs[c], SCAN_SUB), :] for ref in key_refs[c]]
            for pi, (i1, i2) in enumerate(((0, 1), (2, 3), (4, None))):
                x1 = xs[i1]
                xr1 = pltpu.roll(x1, HEAD_A, axis=1)
                if i2 is None:
                    comb_a, comb_b = x1, xr1
                else:
                    x2 = xs[i2]
                    xr2 = pltpu.roll(x2, HEAD_A, axis=1)
                    comb_a = jnp.where(first_head, x1, xr2)
                    comb_b = jnp.where(first_head, xr1, x2)
                mt_a = _split3_rows(comb_a).T.astype(jnp.bfloat16)
                mt_b = _split3_rows(comb_b).T.astype(jnp.bfloat16)
                r0 = (c * N_KEYVEC + 2 * pi) * HEAD_A
                lhs_ref[slot, r0:r0 + HEAD_A, 0:LANES] = mt_a[0:HEAD_A]
                lhs_ref[slot, r0:r0 + HEAD_A, LANES:2 * LANES] = mt_b[0:HEAD_A]
                if i2 is not None:
                    lhs_ref[slot, r0 + HEAD_A:r0 + 2 * HEAD_A, 0:LANES] = mt_a[HEAD_A:2 * HEAD_A]
                    lhs_ref[slot, r0 + HEAD_A:r0 + 2 * HEAD_A, LANES:2 * LANES] = mt_b[HEAD_A:2 * HEAD_A]

    def sub_offsets(q):
        return (pl.multiple_of(q * SCAN_SUB, SCAN_SUB),
                pl.multiple_of(SCAN_TB - (q + 1) * SCAN_SUB, SCAN_SUB))

    def run_sub(slot, offs, hs):
        vs = [v_refs[c][pl.ds(offs[c], SCAN_SUB), :] for c in range(2)]
        ys = ([], [])
        for p in range(npairs):
            tiles = (jnp.dot(lhs_ref[slot, 0:half, :], e_ref[p], preferred_element_type=jnp.float32),
                     jnp.dot(lhs_ref[slot, half:2 * half, :], e_ref[npairs - 1 - p],
                             preferred_element_type=jnp.float32))
            for s in range(2):
                u = 2 * p + s
                for c in range(2):
                    col = s if c == 0 else 1 - s
                    row = u if c == 0 else SCAN_SUB - 1 - u
                    kk_t, w_t, b_t, kd_t, r_t = (
                        tiles[c][k * HEAD_A:(k + 1) * HEAD_A, col * LANES:(col + 1) * LANES]
                        for k in range(N_KEYVEC))
                    h = hs[c]
                    sa = -jnp.sum(h * kk_t, axis=0, keepdims=True)
                    h = h * w_t + sa * b_t + vs[c][row:row + 1, :] * kd_t
                    ys[c].append(jnp.sum(h * r_t, axis=0, keepdims=True))
                    hs[c] = h
        y_refs[0][pl.ds(offs[0], SCAN_SUB), :] = jnp.concatenate(ys[0], axis=0)
        y_refs[1][pl.ds(offs[1], SCAN_SUB), :] = jnp.concatenate(ys[1][::-1], axis=0)
        return hs

    def trip(g, carry):
        hs = list(carry)
        offs = [sub_offsets(g * SCAN_TRIP + k) for k in range(SCAN_TRIP)]
        for k in range(SCAN_TRIP):
            build_lhs(k, offs[k])
        for k in range(SCAN_TRIP):
            hs = run_sub(k, offs[k], hs)
        return tuple(hs)

    hf, hb = lax.fori_loop(0, SCAN_TB // (SCAN_SUB * SCAN_TRIP), trip, (st_ref[0], st_ref[1]))
    st_ref[0] = hf
    st_ref[1] = hb

    @pl.when(tb == nblk - 1)
    def _():
        sfin_ref[0, 0, 0] = hf
        sfin_ref[0, 1, 0] = hb


def _rwkv_scan(ops, s0, row0, bsz, t_len):
    npair = H_A // 2
    nblk = t_len // SCAN_TB
    blk0 = row0 // SCAN_TB
    s0p = s0.reshape(bsz, 2, npair, 2, HEAD_A, HEAD_A).transpose(0, 1, 2, 5, 3, 4)
    s0p = s0p.reshape(bsz, 2, npair, HEAD_A, 2 * HEAD_A)
    ins = [ops['kk'], ops['w0'], ops['b0'], ops['kd0'], ops['r'], ops['v'],
           ops['kk'], ops['w1'], ops['b1'], ops['kd1'], ops['r'], ops['v']]
    blk = (SCAN_TB, LANES)
    fwd_in = pl.BlockSpec(blk, lambda b, p, t: (blk0 + b * nblk + t, p))
    bwd_in = pl.BlockSpec(blk, lambda b, p, t: (blk0 + b * nblk + nblk - 1 - t, p))
    fwd_out = pl.BlockSpec(blk, lambda b, p, t: (b * nblk + t, p))
    bwd_out = pl.BlockSpec(blk, lambda b, p, t: (b * nblk + nblk - 1 - t, p))
    st_spec = pl.BlockSpec((1, 2, 1, HEAD_A, LANES), lambda b, p, t: (b, 0, p, 0, 0))
    e_tab = _sel_table()
    in_specs = ([fwd_in] * 6 + [bwd_in] * 6
                + [pl.BlockSpec(e_tab.shape, lambda b, p, t: (0, 0, 0)), st_spec])
    out_shape = [
        jax.ShapeDtypeStruct((bsz * t_len, W_A), jnp.float32),
        jax.ShapeDtypeStruct((bsz * t_len, W_A), jnp.float32),
        jax.ShapeDtypeStruct((bsz, 2, npair, HEAD_A, LANES), jnp.float32),
    ]
    yf, yb, sfin = pl.pallas_call(
        functools.partial(_scan_kernel, nblk=nblk),
        grid=(bsz, npair, nblk),
        in_specs=in_specs,
        out_specs=[fwd_out, bwd_out, st_spec],
        out_shape=out_shape,
        name="rwkv_scan",
        scratch_shapes=[pltpu.VMEM((2, HEAD_A, LANES), jnp.float32),
                        pltpu.VMEM((SCAN_TRIP, 2 * N_KEYVEC * HEAD_A, 2 * LANES), jnp.bfloat16)],
        compiler_params=pltpu.CompilerParams(
            dimension_semantics=("parallel", "parallel", "arbitrary"),
            vmem_limit_bytes=VMEM_LIMIT_BYTES,
        ),
    )(*ins, e_tab, s0p)
    sfin = sfin.reshape(bsz, 2, npair, HEAD_A, 2, HEAD_A).transpose(0, 1, 2, 4, 5, 3)
    return yf, yb, sfin.reshape(bsz, 2, H_A, HEAD_A, HEAD_A)


class TokenLayout:
    def __init__(self, bc, tc, bl, tl):
        assert tc % ROW_TILE == 0 and tl % ROW_TILE == 0
        self.n_rows = bc * tc + bl * tl
        self.n_tiles = self.n_rows // ROW_TILE
        self.ctx_tiles = bc * tc // ROW_TILE
        self.ctx_seq_tiles = tc // ROW_TILE
        self.lat_seq_tiles = tl // ROW_TILE

    def group(self, i):
        return jnp.where(i < self.ctx_tiles, 0, 1 + (i - self.ctx_tiles) // self.lat_seq_tiles)

    def pos_in_seq(self, i):
        return jnp.where(i < self.ctx_tiles, i % self.ctx_seq_tiles, (i - self.ctx_tiles) % self.lat_seq_tiles)

    def seq_tiles(self, i):
        return jnp.where(i < self.ctx_tiles, self.ctx_seq_tiles, self.lat_seq_tiles)


def _modulate_kernel(x_ref, m_ref, h_ref):
    x = x_ref[...]
    h_ref[...] = (x * (1 + m_ref[0, 0:1, :]) + m_ref[0, 1:2, :]).astype(h_ref.dtype)


def _modulate(x, m2, lay):
    n, d = x.shape
    return pl.pallas_call(
        _modulate_kernel,
        grid=(lay.n_tiles,),
        in_specs=[pl.BlockSpec((ROW_TILE, d), lambda i: (i, 0)),
                  pl.BlockSpec((1, 2, d), lambda i: (lay.group(i), 0, 0))],
        out_specs=pl.BlockSpec((ROW_TILE, d), lambda i: (i, 0)),
        out_shape=jax.ShapeDtypeStruct((n, d), jnp.bfloat16),
        name="modulate",
        compiler_params=pltpu.CompilerParams(dimension_semantics=("parallel",),
                                             vmem_limit_bytes=VMEM_LIMIT_BYTES),
    )(x, m2)


def _ln_residual_kernel(y_ref, x_ref, m_ref, g_ref, b_ref, xo_ref, *h_ref):
    v = ALPHA * x_ref[...] + m_ref[0, 0:1, :] * y_ref[...]
    vc = v - jnp.mean(v, -1, keepdims=True)
    var = jnp.mean(vc * vc, -1, keepdims=True)
    xn = (vc * lax.rsqrt(var + LN_EPS)) * g_ref[...] + b_ref[...]
    xo_ref[...] = xn
    if h_ref:
        h_ref[0][...] = (xn * (1 + m_ref[0, 1:2, :]) + m_ref[0, 2:3, :]).astype(jnp.bfloat16)


def _ln_residual(y, x, m3, g, b, lay, with_h):
    n, d = x.shape
    row = pl.BlockSpec((ROW_TILE, d), lambda i: (i, 0))
    vec = pl.BlockSpec((1, d), lambda i: (0, 0))
    out_shape = [jax.ShapeDtypeStruct((n, d), jnp.float32)]
    out_specs = [row]
    if with_h:
        out_shape.append(jax.ShapeDtypeStruct((n, d), jnp.bfloat16))
        out_specs.append(row)
    return pl.pallas_call(
        _ln_residual_kernel,
        grid=(lay.n_tiles,),
        in_specs=[row, row, pl.BlockSpec((1, 3, d), lambda i: (lay.group(i), 0, 0)), vec, vec],
        out_specs=out_specs,
        out_shape=out_shape,
        name="ln_residual",
        compiler_params=pltpu.CompilerParams(dimension_semantics=("parallel",),
                                             vmem_limit_bytes=VMEM_LIMIT_BYTES),
    )(y, x, m3, g.reshape(1, d), b.reshape(1, d))


def _ffn_gate_kernel(zc_ref, zp_ref, zn_ref, w_ref, b_ref, o_ref, *, lay):
    i = pl.program_id(0)
    pos = lay.pos_in_seq(i)
    has_prev = pos > 0
    has_next = pos < lay.seq_tiles(i) - 1
    row = lax.broadcasted_iota(jnp.int32, (ROW_TILE, FF_CHUNK), 0)
    n_chunks = D_FF // FF_CHUNK

    def conv(col):
        sl = pl.ds(col, FF_CHUNK)
        z = zc_ref[:, sl].astype(jnp.float32)
        zprev_row = jnp.where(has_prev, zp_ref[SUBLANES - 1:SUBLANES, sl].astype(jnp.float32), 0.0)
        znext_row = jnp.where(has_next, zn_ref[0:1, sl].astype(jnp.float32), 0.0)
        zprev = jnp.where(row == 0, zprev_row, pltpu.roll(z, 1, axis=0))
        znext = jnp.where(row == ROW_TILE - 1, znext_row, pltpu.roll(z, ROW_TILE - 1, axis=0))
        return zprev * w_ref[0:1, sl] + b_ref[0:1, sl] + z * w_ref[1:2, sl] + znext * w_ref[2:3, sl]

    def body(j, carry):
        ca = pl.multiple_of(j * FF_CHUNK, FF_CHUNK)
        cb = pl.multiple_of(D_FF + j * FF_CHUNK, FF_CHUNK)
        a = conv(ca)
        bv = conv(cb)
        o_ref[:, pl.ds(ca, FF_CHUNK)] = (a * jax.nn.sigmoid(a) * bv).astype(o_ref.dtype)
        return carry

    lax.fori_loop(0, n_chunks, body, 0)
    o_ref[:, D_FF:FF_PAD] = jnp.zeros((ROW_TILE, FF_PAD - D_FF), o_ref.dtype)


def _ffn_gate(z, conv_w, conv_b, lay):
    n, c2 = z.shape
    rt8 = ROW_TILE // SUBLANES
    last8 = n // SUBLANES - 1
    return pl.pallas_call(
        functools.partial(_ffn_gate_kernel, lay=lay),
        grid=(lay.n_tiles,),
        in_specs=[pl.BlockSpec((ROW_TILE, c2), lambda i: (i, 0)),
                  pl.BlockSpec((SUBLANES, c2), lambda i: (jnp.maximum(i * rt8 - 1, 0), 0)),
                  pl.BlockSpec((SUBLANES, c2), lambda i: (jnp.minimum((i + 1) * rt8, last8), 0)),
                  pl.BlockSpec((3, c2), lambda i: (0, 0)),
                  pl.BlockSpec((1, c2), lambda i: (0, 0))],
        out_specs=pl.BlockSpec((ROW_TILE, FF_PAD), lambda i: (i, 0)),
        out_shape=jax.ShapeDtypeStruct((n, FF_PAD), jnp.bfloat16),
        name="ffn_gate",
        compiler_params=pltpu.CompilerParams(dimension_semantics=("parallel",),
                                             vmem_limit_bytes=VMEM_LIMIT_BYTES),
    )(z, z, z, conv_w, conv_b.reshape(1, c2))


def _cparams():
    return pltpu.CompilerParams(dimension_semantics=("parallel",), vmem_limit_bytes=VMEM_LIMIT_BYTES)


def _halo_specs(n_rows, halo, width, col):
    per_tile = ROW_TILE // halo
    last = n_rows // halo - 1
    prev = pl.BlockSpec((halo, width), lambda i: (jnp.maximum(i * per_tile - 1, 0), col))
    nxt = pl.BlockSpec((halo, width), lambda i: (jnp.minimum((i + 1) * per_tile, last), col))
    return prev, nxt


def _seq_edges(lay):
    i = pl.program_id(0)
    pos = lay.pos_in_seq(i)
    return pos, pos > 0, pos < lay.seq_tiles(i) - 1


def _row_layer_norm(x, g, b):
    xc = x - jnp.mean(x, -1, keepdims=True)
    var = jnp.mean(xc * xc, -1, keepdims=True)
    return (xc * lax.rsqrt(var + LN_EPS)) * g + b


def _conformer_kernel(a_ref, g_ref, ap_ref, gp_ref, an_ref, gn_ref, w_ref, b_ref, lg_ref, lb_ref,
                      o_ref, hext_ref, y_ref, *, lay):
    _, has_prev, has_next = _seq_edges(lay)
    glu = lambda a, g: a * jax.nn.sigmoid(g)
    hext_ref[0:CONV_HALO] = jnp.where(has_prev, glu(ap_ref[...], gp_ref[...]), 0.0)
    hext_ref[CONV_HALO:CONV_HALO + ROW_TILE] = glu(a_ref[...], g_ref[...])
    hext_ref[CONV_HALO + ROW_TILE:] = jnp.where(has_next, glu(an_ref[...], gn_ref[...]), 0.0)
    first = CONV_HALO - CONV_K // 2
    for rc in range(ROW_TILE // CONV_ROWS):
        base = first + rc * CONV_ROWS
        acc = hext_ref[base:base + CONV_ROWS] * w_ref[0:1] + b_ref[...]
        for j in range(1, CONV_K):
            acc = acc + hext_ref[base + j:base + j + CONV_ROWS] * w_ref[j:j + 1]
        y_ref[rc * CONV_ROWS:(rc + 1) * CONV_ROWS] = acc
    y = _row_layer_norm(y_ref[...], lg_ref[...], lb_ref[...])
    o_ref[...] = (y * jax.nn.sigmoid(y)).astype(o_ref.dtype)


def _conformer_group(f, conv_w, conv_b, ln_g, ln_b, lay):
    n = f.shape[0]
    cur = lambda col: pl.BlockSpec((ROW_TILE, W_C), lambda i: (i, col))
    pa, na = _halo_specs(n, CONV_HALO, W_C, COL_CA)
    pg, ng = _halo_specs(n, CONV_HALO, W_C, COL_CG)
    vec = pl.BlockSpec((1, W_C), lambda i: (0, 0))
    return pl.pallas_call(
        functools.partial(_conformer_kernel, lay=lay),
        grid=(lay.n_tiles,),
        in_specs=[cur(COL_CA), cur(COL_CG), pa, pg, na, ng,
                  pl.BlockSpec((CONV_K, W_C), lambda i: (0, 0)), vec, vec, vec],
        out_specs=pl.BlockSpec((ROW_TILE, W_C), lambda i: (i, 0)),
        out_shape=jax.ShapeDtypeStruct((n, W_C), jnp.bfloat16),
        name="conformer_group",
        scratch_shapes=[pltpu.VMEM((ROW_TILE + 2 * CONV_HALO, W_C), jnp.float32),
                        pltpu.VMEM((ROW_TILE, W_C), jnp.float32)],
        compiler_params=_cparams(),
    )(f, f, f, f, f, f, conv_w, conv_b.reshape(1, W_C), ln_g.reshape(1, W_C), ln_b.reshape(1, W_C))


def _gmlp_kernel(u_ref, v_ref, lg_ref, lb_ref, ws_ref, bias_ref, o_ref):
    v = _row_layer_norm(jax.nn.gelu(v_ref[...]), lg_ref[...], lb_ref[...]).astype(jnp.bfloat16)
    for ch in range(ROW_TILE // CHUNK):
        rows = slice(ch * CHUNK, (ch + 1) * CHUNK)
        for h in range(H_D):
            cols = slice(h * HEAD_D, (h + 1) * HEAD_D)
            s = jnp.dot(ws_ref[h], v[rows, cols], preferred_element_type=jnp.float32) + bias_ref[:, cols]
            o_ref[rows, cols] = (jax.nn.gelu(u_ref[rows, cols]) * s).astype(o_ref.dtype)


def _gmlp_group(f, ln_g, ln_b, ws, bs, lay):
    n = f.shape[0]
    vec = pl.BlockSpec((1, W_D), lambda i: (0, 0))
    bias = jnp.repeat(bs.T, HEAD_D, axis=1)
    return pl.pallas_call(
        _gmlp_kernel,
        grid=(lay.n_tiles,),
        in_specs=[pl.BlockSpec((ROW_TILE, W_D), lambda i: (i, COL_DU)),
                  pl.BlockSpec((ROW_TILE, W_D), lambda i: (i, COL_DV)),
                  vec, vec,
                  pl.BlockSpec((H_D, CHUNK, CHUNK), lambda i: (0, 0, 0)),
                  pl.BlockSpec((CHUNK, W_D), lambda i: (0, 0))],
        out_specs=pl.BlockSpec((ROW_TILE, W_D), lambda i: (i, 0)),
        out_shape=jax.ShapeDtypeStruct((n, W_D), jnp.bfloat16),
        name="gmlp_group",
        compiler_params=_cparams(),
    )(f, f, ln_g.reshape(1, W_D), ln_b.reshape(1, W_D), ws.astype(jnp.bfloat16), bias)


def _pool_kernel(x_ref, xp_ref, xn_ref, w_ref, b_ref, sc_ref, o_ref, hext_ref, *, lay):
    pos, has_prev, has_next = _seq_edges(lay)
    hext_ref[0:POOL_HALO] = jnp.where(has_prev, xp_ref[...], 0.0)
    hext_ref[POOL_HALO:POOL_HALO + ROW_TILE] = x_ref[...]
    hext_ref[POOL_HALO + ROW_TILE:] = jnp.where(has_next, xn_ref[...], 0.0)
    t = pos * ROW_TILE + lax.broadcasted_iota(jnp.int32, (ROW_TILE, POOL_GROUP), 0)
    t_len = lay.seq_tiles(pl.program_id(0)) * ROW_TILE
    for gi, win in enumerate(POOL_WINDOWS):
        cols = slice(gi * POOL_GROUP, (gi + 1) * POOL_GROUP)
        half = win // 2
        ssum = hext_ref[POOL_HALO - half:POOL_HALO - half + ROW_TILE, cols]
        for dlt in range(-half + 1, half):
            ssum = ssum + hext_ref[POOL_HALO + dlt:POOL_HALO + dlt + ROW_TILE, cols]
        cnt = (jnp.minimum(t + half, t_len) - jnp.maximum(t - half, 0)).astype(jnp.float32)
        pooled = (ssum / cnt - x_ref[:, cols]).astype(jnp.bfloat16)
        y = jnp.dot(pooled, w_ref[gi], preferred_element_type=jnp.float32) + b_ref[:, cols]
        o_ref[:, cols] = (y * sc_ref[:, cols]).astype(o_ref.dtype)


def _pool_group(f, pool_w, pool_b, pool_scale, lay):
    n = f.shape[0]
    prev, nxt = _halo_specs(n, POOL_HALO, W_B, COL_B)
    vec = pl.BlockSpec((1, W_B), lambda i: (0, 0))
    return pl.pallas_call(
        functools.partial(_pool_kernel, lay=lay),
        grid=(lay.n_tiles,),
        in_specs=[pl.BlockSpec((ROW_TILE, W_B), lambda i: (i, COL_B)), prev, nxt,
                  pl.BlockSpec(pool_w.shape, lambda i: (0, 0, 0)), vec, vec],
        out_specs=pl.BlockSpec((ROW_TILE, W_B), lambda i: (i, 0)),
        out_shape=jax.ShapeDtypeStruct((n, W_B), jnp.bfloat16),
        name="pool_group",
        scratch_shapes=[pltpu.VMEM((ROW_TILE + 2 * POOL_HALO, W_B), jnp.float32)],
        compiler_params=_cparams(),
    )(f, f, f, pool_w.astype(jnp.bfloat16), pool_b.reshape(1, W_B), pool_scale.reshape(1, W_B))


def _mm4_kernel(y0, y1, y2, y3, w_ref, o_ref):
    wk = w_ref.shape[0] // 4
    acc = jnp.dot(y0[...], w_ref[0:wk, :], preferred_element_type=jnp.float32)
    for k, y in enumerate((y1, y2, y3), start=1):
        acc = acc + jnp.dot(y[...], w_ref[k * wk:(k + 1) * wk, :], preferred_element_type=jnp.float32)
    o_ref[...] = acc


def _matmul4(ys, w, tm, tn):
    m = ys[0].shape[0]
    kdim, n = w.shape
    yspec = pl.BlockSpec((tm, kdim // 4), lambda i, j: (i, 0))
    return pl.pallas_call(
        _mm4_kernel,
        grid=(m // tm, n // tn),
        in_specs=[yspec] * 4 + [pl.BlockSpec((kdim, tn), lambda i, j: (0, j))],
        out_specs=pl.BlockSpec((tm, tn), lambda i, j: (i, j)),
        out_shape=jax.ShapeDtypeStruct((m, n), jnp.float32),
        name="dense_mm4",
        compiler_params=pltpu.CompilerParams(dimension_semantics=("parallel", "parallel"),
                                             vmem_limit_bytes=VMEM_LIMIT_BYTES),
    )(*ys, w)


def _head_ones():
    i = np.arange(LANES)
    return jnp.asarray((i[:, None] // HEAD_A) == (i[None, :] // HEAD_A), jnp.bfloat16)


def _head_sum(x, ones):
    hi = x.astype(jnp.bfloat16)
    lo = (x - hi.astype(jnp.float32)).astype(jnp.bfloat16)
    outs = []
    for s in range(x.shape[1] // LANES):
        cols = slice(s * LANES, (s + 1) * LANES)
        outs.append(jnp.dot(hi[:, cols], ones, preferred_element_type=jnp.float32)
                    + jnp.dot(lo[:, cols], ones, preferred_element_type=jnp.float32))
    return jnp.concatenate(outs, axis=1)


def _dot_split(a, w_hi, w_lo):
    a_hi = a.astype(jnp.bfloat16)
    a_lo = (a - a_hi.astype(jnp.float32)).astype(jnp.bfloat16)
    return (jnp.dot(a_hi, w_hi, preferred_element_type=jnp.float32)
            + jnp.dot(a_lo, w_hi, preferred_element_type=jnp.float32)
            + jnp.dot(a_hi, w_lo, preferred_element_type=jnp.float32))


def _token_shift(x, prev_row, next_row, mu_p, mu_n):
    rows = x.shape[0]
    row = lax.broadcasted_iota(jnp.int32, x.shape, 0)
    xp = jnp.where(row == 0, prev_row, pltpu.roll(x, 1, axis=0))
    xn = jnp.where(row == rows - 1, next_row, pltpu.roll(x, rows - 1, axis=0))
    return x + mu_p * (xp - x) + mu_n * (xn - x)


PREP_OUT_NAMES = ('kk', 'w0', 'b0', 'kd0', 'w1', 'b1', 'kd1', 'r', 'v', 'g', 'bonus')


def _rwkv_prep_kernel(x_ref, xp_ref, xn_ref, l_ref, lp_ref, ln_ref, mu_ref, mul_ref, whi_ref, wlo_ref,
                      vec_ref, ones_ref, *out_refs, lay):
    _, has_prev, has_next = _seq_edges(lay)
    edge = lambda ref, r, ok: jnp.where(ok, ref[r:r + 1, :], 0.0)
    x = _token_shift(x_ref[...], edge(xp_ref, SUBLANES - 1, has_prev), edge(xn_ref, 0, has_next),
                     mu_ref[0:1, :], mu_ref[1:2, :])
    lo = _token_shift(l_ref[...], edge(lp_ref, SUBLANES - 1, has_prev), edge(ln_ref, 0, has_next),
                      mul_ref[0:1, :], mul_ref[1:2, :])
    r = x[:, 0:W_A]
    k = x[:, W_A:2 * W_A]
    v = x[:, 2 * W_A:3 * W_A]
    col = lax.broadcasted_iota(jnp.int32, lo.shape, 1)
    act = jnp.where(col < D_DECAY_LORA, jnp.tanh(lo),
                    jnp.where(col < D_DECAY_LORA + D_AAA_LORA, lo, jax.nn.sigmoid(lo)))
    ones = ones_ref[...]
    kk = k * vec_ref[0:1, :]
    kk = kk * lax.rsqrt(_head_sum(kk * kk, ones) + 1e-12)
    g = _dot_split(act, whi_ref[4], wlo_ref[4])
    outs = {'r': r, 'v': v, 'kk': kk, 'g': g}
    bonus = None
    for d in range(2):
        w = jnp.exp(-DECAY_SCALE * jax.nn.sigmoid(vec_ref[3 + d:4 + d, :] + _dot_split(act, whi_ref[d], wlo_ref[d])))
        a = jax.nn.sigmoid(vec_ref[5 + d:6 + d, :] + _dot_split(act, whi_ref[2 + d], wlo_ref[2 + d]))
        kd = k * (1 + (a - 1) * vec_ref[1:2, :])
        term = _head_sum(r * kd * vec_ref[2:3, :], ones) * v
        bonus = term if bonus is None else bonus + term
        outs['w%d' % d] = w
        outs['kd%d' % d] = kd
        outs['b%d' % d] = kk * a
    outs['bonus'] = bonus
    for ref, name in zip(out_refs, PREP_OUT_NAMES):
        ref[...] = outs[name]


def _rwkv_prep(x_rkv, x_lora, lora_col, P, lay):
    n = x_rkv.shape[0]
    o = 3 * W_A
    pr, nr = _halo_specs(n, SUBLANES, o, 0)
    plr, nlr = _halo_specs(n, SUBLANES, LORA_PAD, lora_col)
    full = lambda a: pl.BlockSpec(a.shape, lambda i: (0,) * a.ndim)
    mu = jnp.stack([P['mu_prev'][0:o], P['mu_next'][0:o]])
    pad = LORA_PAD - LORA_W
    mul = jnp.stack([jnp.pad(P['mu_prev'][o:], (0, pad)), jnp.pad(P['mu_next'][o:], (0, pad))])
    z = lambda rows: jnp.zeros((rows, W_A), jnp.float32)
    wcat = jnp.stack([
        jnp.concatenate([P['w_up'][0], z(LORA_PAD - D_DECAY_LORA)]),
        jnp.concatenate([P['w_up'][1], z(LORA_PAD - D_DECAY_LORA)]),
        jnp.concatenate([z(D_DECAY_LORA), P['a_up'][0], z(LORA_PAD - D_DECAY_LORA - D_AAA_LORA)]),
        jnp.concatenate([z(D_DECAY_LORA), P['a_up'][1], z(LORA_PAD - D_DECAY_LORA - D_AAA_LORA)]),
        jnp.concatenate([z(D_DECAY_LORA + D_AAA_LORA), P['g_up'], z(pad)]),
    ])
    w_hi = wcat.astype(jnp.bfloat16)
    w_lo = (wcat - w_hi.astype(jnp.float32)).astype(jnp.bfloat16)
    vec = jnp.stack([P['k_k'], P['k_a'], P['r_k'].reshape(W_A), P['w0'][0], P['w0'][1], P['a0'][0], P['a0'][1],
                     jnp.zeros((W_A,), jnp.float32)])
    ones = _head_ones()
    row_out = pl.BlockSpec((ROW_TILE, W_A), lambda i: (i, 0))
    outs = pl.pallas_call(
        functools.partial(_rwkv_prep_kernel, lay=lay),
        grid=(lay.n_tiles,),
        in_specs=[pl.BlockSpec((ROW_TILE, o), lambda i: (i, 0)), pr, nr,
                  pl.BlockSpec((ROW_TILE, LORA_PAD), lambda i: (i, lora_col)), plr, nlr,
                  full(mu), full(mul), full(w_hi), full(w_lo), full(vec), full(ones)],
        out_specs=[row_out] * N_PREP_OUT,
        out_shape=[jax.ShapeDtypeStruct((n, W_A), jnp.float32)] * N_PREP_OUT,
        name="rwkv_prep",
        compiler_params=_cparams(),
    )(x_rkv, x_rkv, x_rkv, x_lora, x_lora, x_lora, mu, mul, w_hi, w_lo, vec, ones)
    return dict(zip(PREP_OUT_NAMES, outs))


def _rwkv_post_kernel(yf_ref, yb_ref, bonus_ref, g_ref, vec_ref, ones_ref, o_ref):
    ones = ones_ref[...]
    y = yf_ref[...] + yb_ref[...]
    yc = y - _head_sum(y, ones) * (1.0 / HEAD_A)
    var = _head_sum(yc * yc, ones) * (1.0 / HEAD_A)
    y = yc * lax.rsqrt(var + GN_EPS)
    y = y * vec_ref[0:1, :] + vec_ref[1:2, :] + bonus_ref[...]
    o_ref[...] = (y * g_ref[...]).astype(o_ref.dtype)


def _rwkv_post(yf, yb, bonus, g, ln_g, ln_b, lay):
    n = yf.shape[0]
    row = pl.BlockSpec((ROW_TILE, W_A), lambda i: (i, 0))
    vec = jnp.stack([ln_g, ln_b] + [jnp.zeros_like(ln_g)] * 6)
    ones = _head_ones()
    full = lambda a: pl.BlockSpec(a.shape, lambda i: (0,) * a.ndim)
    return pl.pallas_call(
        _rwkv_post_kernel,
        grid=(lay.n_tiles,),
        in_specs=[row, row, row, row, full(vec), full(ones)],
        out_specs=row,
        out_shape=jax.ShapeDtypeStruct((n, W_A), jnp.bfloat16),
        name="rwkv_post",
        compiler_params=_cparams(),
    )(yf, yb, bonus, g, vec, ones)


def _grid_transpose(z, rows, cols):
    b, t, ch = z.shape
    return z.reshape(b, rows, cols, ch).transpose(0, 2, 1, 3).reshape(b, t, ch)


def _rwkv_group(f, P, lay, dims, s_ctx0, s_lat0, transposed):
    bc, tc, bl, tl = dims
    nc = bc * tc
    rows = tl // GRID_W
    if transposed:
        def permuted(cols):
            lat = _grid_transpose(f[nc:, cols].reshape(bl, tl, -1), rows, GRID_W)
            return jnp.concatenate([f[:nc, cols], lat.reshape(bl * tl, -1)], axis=0)
        x_rkv = permuted(slice(0, 3 * W_A))
        x_lora = permuted(slice(LORA_COL * W_A, LORA_COL * W_A + LORA_PAD))
        lora_col = 0
    else:
        x_rkv, x_lora, lora_col = f, f, LORA_COL * W_A // LORA_PAD
    ops = _rwkv_prep(x_rkv, x_lora, lora_col, P, lay)
    yf_c, yb_c, s_fin = _rwkv_scan(ops, s_ctx0, 0, bc, tc)
    yf_l, yb_l, _ = _rwkv_scan(ops, s_lat0, nc, bl, tl)
    out = _rwkv_post(jnp.concatenate([yf_c, yf_l], axis=0), jnp.concatenate([yb_c, yb_l], axis=0),
                     ops['bonus'], ops['g'], P['rwkv_ln_g'], P['rwkv_ln_b'], lay)
    if transposed:
        lat = _grid_transpose(out[nc:].reshape(bl, tl, W_A), GRID_W, rows)
        out = jnp.concatenate([out[:nc], lat.reshape(bl * tl, W_A)], axis=0)
    return out, s_fin


def _modulation(cvec, P):
    m = jnp.dot(jax.nn.silu(cvec), P['w_ada'], precision=lax.Precision.HIGHEST) + P['b_ada']
    return m.reshape(cvec.shape[0], N_MOD, D_MODEL)


def _pack_w_in(w_in):
    o = 3 * W_A
    parts = [w_in[:, 0:o], w_in[:, COLS_A:], w_in[:, o:COLS_A]]
    w = jnp.concatenate(parts, axis=1)
    return jnp.pad(w, ((0, 0), (0, IN_PAD - IN_COLS))).astype(jnp.bfloat16)


def kernel(x_prompt, x_sample, c, state_rwkv, c_ctx, w_ada, b_ada, w_in, mu_prev, mu_next, rwkv_g_up, rwkv_w0, rwkv_w_up, rwkv_a0, rwkv_a_up, rwkv_k_k, rwkv_k_a, rwkv_r_k, rwkv_ln_g, rwkv_ln_b, pool_w, pool_b, pool_scale, conf_conv_w, conf_conv_b, conf_ln_g, conf_ln_b, gmlp_ln_g, gmlp_ln_b, gmlp_ws, gmlp_bs, w_out, ln1_g, ln1_b, ffn_w_up, ffn_conv_w, ffn_conv_b, ffn_w_down, ln2_g, ln2_b):
    bc, tc, d = x_prompt.shape
    bl, tl, _ = x_sample.shape
    nc = bc * tc
    lay = TokenLayout(bc, tc, bl, tl)
    s_ctx0 = jnp.zeros((bc, 2, H_A, HEAD_A, HEAD_A), jnp.float32)
    cvec = jnp.concatenate([c_ctx[None, :], c], axis=0)
    mods = [_modulation(cvec, {'w_ada': w_ada[l], 'b_ada': b_ada[l]}) for l in range(DEPTH)]
    x = jnp.concatenate([x_prompt.reshape(nc, d), x_sample.reshape(bl * tl, d)], axis=0)
    h = _modulate(x, jnp.stack([mods[0][:, 1], mods[0][:, 0]], axis=1), lay)
    new_states = []
    for l in range(DEPTH):
        P = {
            'mu_prev': mu_prev[l], 'mu_next': mu_next[l], 'g_up': rwkv_g_up[l],
            'w0': rwkv_w0[l], 'w_up': rwkv_w_up[l], 'a0': rwkv_a0[l], 'a_up': rwkv_a_up[l],
            'k_k': rwkv_k_k[l], 'k_a': rwkv_k_a[l], 'r_k': rwkv_r_k[l],
            'rwkv_ln_g': rwkv_ln_g[l], 'rwkv_ln_b': rwkv_ln_b[l],
        }
        w_in_p = _pack_w_in(w_in[l])
        w_out_b = w_out[l].astype(jnp.bfloat16)
        w_up_b = ffn_w_up[l].astype(jnp.bfloat16)
        w_dn_b = jnp.pad(ffn_w_down[l], ((0, FF_PAD - D_FF), (0, 0))).astype(jnp.bfloat16)

        m = mods[l]

        f = _matmul(h, w_in_p, jnp.float32, 1024, 512, D_MODEL)
        y_a, s_fin = _rwkv_group(f, P, lay, (bc, tc, bl, tl), s_ctx0, state_rwkv[:, l], l % 2 == 1)
        new_states.append(s_fin)
        y_b = _pool_group(f, pool_w[l], pool_b[l], pool_scale[l], lay)
        y_c = _conformer_group(f, conf_conv_w[l], conf_conv_b[l], conf_ln_g[l], conf_ln_b[l], lay)
        y_d = _gmlp_group(f, gmlp_ln_g[l], gmlp_ln_b[l], gmlp_ws[l], gmlp_bs[l], lay)
        y = _matmul4([y_a, y_b, y_c, y_d], w_out_b, 1024, 1024)
        x, h = _ln_residual(y, x, jnp.stack([m[:, 2], m[:, 4], m[:, 3]], axis=1), ln1_g[l], ln1_b[l], lay, True)

        z = _matmul(h, w_up_b, jnp.bfloat16, 1024, 512, D_MODEL)
        g = _ffn_gate(z, ffn_conv_w[l], ffn_conv_b[l], lay)
        y = _matmul(g, w_dn_b, jnp.float32, 1024, 1024, FF_PAD // 4)
        if l + 1 < DEPTH:
            mn = mods[l + 1]
            x, h = _ln_residual(y, x, jnp.stack([m[:, 5], mn[:, 1], mn[:, 0]], axis=1), ln2_g[l], ln2_b[l], lay, True)
        else:
            (x,) = _ln_residual(y, x, jnp.stack([m[:, 5], m[:, 5], m[:, 5]], axis=1), ln2_g[l], ln2_b[l], lay, False)

    new_state_rwkv = jnp.stack(new_states, axis=1).astype(x_prompt.dtype)
    return (x[:nc].reshape(bc, tc, d), x[nc:].reshape(bl, tl, d), new_state_rwkv)
```

```python
import functools
import math

import numpy as np
import jax
import jax.numpy as jnp
from jax import lax
from jax.experimental import pallas as pl
from jax.experimental.pallas import tpu as pltpu

D_MODEL = 4096
DEPTH = 2
GRID_W = 64
W_A = D_MODEL // 4
W_B = D_MODEL // 4
W_C = D_MODEL // 4
W_D = D_MODEL - W_A - W_B - W_C
HEAD_A = 64
H_A = W_A // HEAD_A
D_DECAY_LORA = max(32, int(round(W_A ** 0.5 * 1.8 / 32)) * 32)
D_AAA_LORA = max(32, int(round(W_A ** 0.5 * 1.8 / 32)) * 32)
D_GATE_LORA = max(32, int(round(W_A ** 0.8 * 0.6 / 32)) * 32)
COLS_A = 3 * W_A + D_DECAY_LORA + D_AAA_LORA + D_GATE_LORA
DECAY_SCALE = math.exp(-0.5)
GN_EPS = 64e-5
POOL_WINDOWS = (2, 4, 8, 16)
POOL_GROUP = W_B // len(POOL_WINDOWS)
CONV_K = 31
CHUNK = 128
H_D = 8
HEAD_D = W_D // H_D
IN_COLS = COLS_A + W_B + 2 * W_C + 2 * W_D
D_FF = ((8 * D_MODEL // 3 + 255) // 256) * 256
FFN_K = 3
ALPHA = (2.0 * DEPTH) ** 0.25
LN_EPS = 1e-5
N_MOD = 6
IN_PAD = ((IN_COLS + 511) // 512) * 512
FF_PAD = ((D_FF + 1023) // 1024) * 1024

LANES = 128
SUBLANES = 8
VMEM_LIMIT_BYTES = 56 * 1024 * 1024

SCAN_TB = 128
SCAN_SUB = 32
SCAN_PARTS = 3
N_KEYVEC = 5
N_STEPVEC = 4
SCAN_TRIP = 4

ROW_TILE = 256
FF_CHUNK = 256
CONV_HALO = 16
POOL_HALO = 8
CONV_ROWS = 32
COL_B, COL_CA, COL_CG, COL_DU, COL_DV, LORA_COL = 3, 4, 5, 6, 7, 8
LORA_W = D_DECAY_LORA + D_AAA_LORA + D_GATE_LORA
LORA_PAD = IN_PAD - LORA_COL * W_A
N_PREP_OUT = 13


def _mm_kernel(x_ref, w_ref, o_ref, acc_ref, *, nk):
    k = pl.program_id(2)
    part = jnp.dot(x_ref[...], w_ref[...], preferred_element_type=jnp.float32)
    if nk == 1:
        o_ref[...] = part.astype(o_ref.dtype)
    else:
        @pl.when(k == 0)
        def _():
            acc_ref[...] = part

        @pl.when(k > 0)
        def _():
            acc_ref[...] += part

        @pl.when(k == nk - 1)
        def _():
            o_ref[...] = acc_ref[...].astype(o_ref.dtype)


def _matmul(x, w, out_dtype, tm, tn, tk):
    m, kdim = x.shape
    _, n = w.shape
    assert m % tm == 0 and n % tn == 0 and kdim % tk == 0
    nk = kdim // tk
    return pl.pallas_call(
        functools.partial(_mm_kernel, nk=nk),
        grid=(m // tm, n // tn, nk),
        in_specs=[
            pl.BlockSpec((tm, tk), lambda i, j, k: (i, k)),
            pl.BlockSpec((tk, tn), lambda i, j, k: (k, j)),
        ],
        out_specs=pl.BlockSpec((tm, tn), lambda i, j, k: (i, j)),
        out_shape=jax.ShapeDtypeStruct((m, n), out_dtype),
        name="dense_mm",
        scratch_shapes=[pltpu.VMEM((tm, tn), jnp.float32)],
        compiler_params=pltpu.CompilerParams(
            dimension_semantics=("parallel", "parallel", "arbitrary"),
            vmem_limit_bytes=VMEM_LIMIT_BYTES,
        ),
    )(x, w)


def _sel_table():
    c = np.arange(2 * LANES)[:, None]
    n = np.arange(2 * LANES)[None, :]
    e = np.zeros((SCAN_SUB // 2, 2 * LANES, 2 * LANES), np.float32)
    for p in range(SCAN_SUB // 2):
        t = 2 * p + n // LANES
        e[p] = (((c % LANES) < SCAN_PARTS * SCAN_SUB) & ((c % SCAN_SUB) == t)
                & ((c // LANES) == ((n % LANES) // HEAD_A)))
    return jnp.asarray(e, jnp.bfloat16)


def _split3_rows(x):
    hi = x.astype(jnp.bfloat16).astype(jnp.float32)
    r1 = x - hi
    mid = r1.astype(jnp.bfloat16).astype(jnp.float32)
    lo = r1 - mid
    return jnp.concatenate([hi, mid, lo, jnp.zeros_like(x)], axis=0)


def _scan_kernel(*refs, nblk):
    key_refs = (refs[0:5], refs[6:11])
    v_refs = (refs[5], refs[11])
    e_ref, s0_ref = refs[12], refs[13]
    y_refs = (refs[14], refs[15])
    sfin_ref, st_ref, lhs_ref = refs[16], refs[17], refs[18]
    tb = pl.program_id(2)

    @pl.when(tb == 0)
    def _():
        st_ref[...] = s0_ref[0, :, 0]

    lane = lax.broadcasted_iota(jnp.int32, (SCAN_SUB, LANES), 1)
    first_head = lane < HEAD_A

    half = N_KEYVEC * HEAD_A
    npairs = SCAN_SUB // 2

    def build_lhs(slot, offs):
        for c in range(2):
            xs = [ref[pl.ds(offs[c], SCAN_SUB), :] for ref in key_refs[c]]
            for pi, (i1, i2) in enumerate(((0, 1), (2, 3), (4, None))):
                x1 = xs[i1]
                xr1 = pltpu.roll(x1, HEAD_A, axis=1)
                if i2 is None:
                    comb_a, comb_b = x1, xr1
                else:
                    x2 = xs[i2]
                    xr2 = pltpu.roll(x2, HEAD_A, axis=1)
                    comb_a = jnp.where(first_head, x1, xr2)
                    comb_b = jnp.where(first_head, xr1, x2)
                mt_a = _split3_rows(comb_a).T.astype(jnp.bfloat16)
                mt_b = _split3_rows(comb_b).T.astype(jnp.bfloat16)
                r0 = (c * N_KEYVEC + 2 * pi) * HEAD_A
                lhs_ref[slot, r0:r0 + HEAD_A, 0:LANES] = mt_a[0:HEAD_A]
                lhs_ref[slot, r0:r0 + HEAD_A, LANES:2 * LANES] = mt_b[0:HEAD_A]
                if i2 is not None:
                    lhs_ref[slot, r0 + HEAD_A:r0 + 2 * HEAD_A, 0:LANES] = mt_a[HEAD_A:2 * HEAD_A]
                    lhs_ref[slot, r0 + HEAD_A:r0 + 2 * HEAD_A, LANES:2 * LANES] = mt_b[HEAD_A:2 * HEAD_A]

    def sub_offsets(q):
        return (pl.multiple_of(q * SCAN_SUB, SCAN_SUB),
                pl.multiple_of(SCAN_TB - (q + 1) * SCAN_SUB, SCAN_SUB))

    def run_sub(slot, offs, hs):
        vs = [v_refs[c][pl.ds(offs[c], SCAN_SUB), :] for c in range(2)]
        ys = ([], [])
        gs = list(hs)
        step_rows = N_STEPVEC * HEAD_A
        for p in range(npairs):
            tiles = (jnp.dot(lhs_ref[slot, 0:step_rows, :], e_ref[p], preferred_element_type=jnp.float32),
                     jnp.dot(lhs_ref[slot, half:half + step_rows, :], e_ref[npairs - 1 - p],
                             preferred_element_type=jnp.float32))
            for s in range(2):
                u = 2 * p + s
                for c in range(2):
                    col = s if c == 0 else 1 - s
                    row = u if c == 0 else SCAN_SUB - 1 - u
                    a_t, b_t, k_t, r_t = (
                        tiles[c][k * HEAD_A:(k + 1) * HEAD_A, col * LANES:(col + 1) * LANES]
                        for k in range(N_STEPVEC))
                    g = gs[c]
                    sa = jnp.sum(g * a_t, axis=0, keepdims=True)
                    g = g + sa * b_t + vs[c][row:row + 1, :] * k_t
                    ys[c].append(jnp.sum(g * r_t, axis=0, keepdims=True))
                    gs[c] = g
        y_refs[0][pl.ds(offs[0], SCAN_SUB), :] = jnp.concatenate(ys[0], axis=0)
        y_refs[1][pl.ds(offs[1], SCAN_SUB), :] = jnp.concatenate(ys[1][::-1], axis=0)
        gam_f = jnp.dot(lhs_ref[slot, step_rows:half, :], e_ref[npairs - 1],
                        preferred_element_type=jnp.float32)[:, LANES:2 * LANES]
        gam_b = jnp.dot(lhs_ref[slot, half + step_rows:2 * half, :], e_ref[0],
                        preferred_element_type=jnp.float32)[:, 0:LANES]
        return [gs[0] * gam_f, gs[1] * gam_b]

    def trip(g, carry):
        hs = list(carry)
        offs = [sub_offsets(g * SCAN_TRIP + k) for k in range(SCAN_TRIP)]
        for k in range(SCAN_TRIP):
            build_lhs(k, offs[k])
        for k in range(SCAN_TRIP):
            hs = run_sub(k, offs[k], hs)
        return tuple(hs)

    hf, hb = lax.fori_loop(0, SCAN_TB // (SCAN_SUB * SCAN_TRIP), trip, (st_ref[0], st_ref[1]))
    st_ref[0] = hf
    st_ref[1] = hb

    @pl.when(tb == nblk - 1)
    def _():
        sfin_ref[0, 0, 0] = hf
        sfin_ref[0, 1, 0] = hb


def _rwkv_scan(ops, s0, row0, bsz, t_len):
    npair = H_A // 2
    nblk = t_len // SCAN_TB
    blk0 = row0 // SCAN_TB
    s0p = s0.reshape(bsz, 2, npair, 2, HEAD_A, HEAD_A).transpose(0, 1, 2, 5, 3, 4)
    s0p = s0p.reshape(bsz, 2, npair, HEAD_A, 2 * HEAD_A)
    ins = []
    for d in range(2):
        ins += [ops[n + str(d)] for n in ('at', 'bt', 'kt', 'rt', 'gam')] + [ops['v']]
    blk = (SCAN_TB, LANES)
    fwd_in = pl.BlockSpec(blk, lambda b, p, t: (blk0 + b * nblk + t, p))
    bwd_in = pl.BlockSpec(blk, lambda b, p, t: (blk0 + b * nblk + nblk - 1 - t, p))
    fwd_out = pl.BlockSpec(blk, lambda b, p, t: (b * nblk + t, p))
    bwd_out = pl.BlockSpec(blk, lambda b, p, t: (b * nblk + nblk - 1 - t, p))
    st_spec = pl.BlockSpec((1, 2, 1, HEAD_A, LANES), lambda b, p, t: (b, 0, p, 0, 0))
    e_tab = _sel_table()
    in_specs = ([fwd_in] * 6 + [bwd_in] * 6
                + [pl.BlockSpec(e_tab.shape, lambda b, p, t: (0, 0, 0)), st_spec])
    out_shape = [
        jax.ShapeDtypeStruct((bsz * t_len, W_A), jnp.float32),
        jax.ShapeDtypeStruct((bsz * t_len, W_A), jnp.float32),
        jax.ShapeDtypeStruct((bsz, 2, npair, HEAD_A, LANES), jnp.float32),
    ]
    yf, yb, sfin = pl.pallas_call(
        functools.partial(_scan_kernel, nblk=nblk),
        grid=(bsz, npair, nblk),
        in_specs=in_specs,
        out_specs=[fwd_out, bwd_out, st_spec],
        out_shape=out_shape,
        name="rwkv_scan",
        scratch_shapes=[pltpu.VMEM((2, HEAD_A, LANES), jnp.float32),
                        pltpu.VMEM((SCAN_TRIP, 2 * N_KEYVEC * HEAD_A, 2 * LANES), jnp.bfloat16)],
        compiler_params=pltpu.CompilerParams(
            dimension_semantics=("parallel", "parallel", "arbitrary"),
            vmem_limit_bytes=VMEM_LIMIT_BYTES,
        ),
    )(*ins, e_tab, s0p)
    sfin = sfin.reshape(bsz, 2, npair, HEAD_A, 2, HEAD_A).transpose(0, 1, 2, 4, 5, 3)
    return yf, yb, sfin.reshape(bsz, 2, H_A, HEAD_A, HEAD_A)


class TokenLayout:
    def __init__(self, bc, tc, bl, tl):
        assert tc % ROW_TILE == 0 and tl % ROW_TILE == 0
        self.n_rows = bc * tc + bl * tl
        self.n_tiles = self.n_rows // ROW_TILE
        self.ctx_tiles = bc * tc // ROW_TILE
        self.ctx_seq_tiles = tc // ROW_TILE
        self.lat_seq_tiles = tl // ROW_TILE

    def group(self, i):
        return jnp.where(i < self.ctx_tiles, 0, 1 + (i - self.ctx_tiles) // self.lat_seq_tiles)

    def pos_in_seq(self, i):
        return jnp.where(i < self.ctx_tiles, i % self.ctx_seq_tiles, (i - self.ctx_tiles) % self.lat_seq_tiles)

    def seq_tiles(self, i):
        return jnp.where(i < self.ctx_tiles, self.ctx_seq_tiles, self.lat_seq_tiles)


def _modulate_kernel(x_ref, m_ref, h_ref):
    x = x_ref[...]
    h_ref[...] = (x * (1 + m_ref[0, 0:1, :]) + m_ref[0, 1:2, :]).astype(h_ref.dtype)


def _modulate(x, m2, lay):
    n, d = x.shape
    return pl.pallas_call(
        _modulate_kernel,
        grid=(lay.n_tiles,),
        in_specs=[pl.BlockSpec((ROW_TILE, d), lambda i: (i, 0)),
                  pl.BlockSpec((1, 2, d), lambda i: (lay.group(i), 0, 0))],
        out_specs=pl.BlockSpec((ROW_TILE, d), lambda i: (i, 0)),
        out_shape=jax.ShapeDtypeStruct((n, d), jnp.bfloat16),
        name="modulate",
        compiler_params=pltpu.CompilerParams(dimension_semantics=("parallel",),
                                             vmem_limit_bytes=VMEM_LIMIT_BYTES),
    )(x, m2)


def _ln_residual_kernel(y_ref, x_ref, m_ref, g_ref, b_ref, xo_ref, *h_ref):
    v = ALPHA * x_ref[...] + m_ref[0, 0:1, :] * y_ref[...]
    vc = v - jnp.mean(v, -1, keepdims=True)
    var = jnp.mean(vc * vc, -1, keepdims=True)
    xn = (vc * lax.rsqrt(var + LN_EPS)) * g_ref[...] + b_ref[...]
    xo_ref[...] = xn
    if h_ref:
        h_ref[0][...] = (xn * (1 + m_ref[0, 1:2, :]) + m_ref[0, 2:3, :]).astype(jnp.bfloat16)


def _ln_residual(y, x, m3, g, b, lay, with_h):
    n, d = x.shape
    row = pl.BlockSpec((ROW_TILE, d), lambda i: (i, 0))
    vec = pl.BlockSpec((1, d), lambda i: (0, 0))
    out_shape = [jax.ShapeDtypeStruct((n, d), jnp.float32)]
    out_specs = [row]
    if with_h:
        out_shape.append(jax.ShapeDtypeStruct((n, d), jnp.bfloat16))
        out_specs.append(row)
    return pl.pallas_call(
        _ln_residual_kernel,
        grid=(lay.n_tiles,),
        in_specs=[row, row, pl.BlockSpec((1, 3, d), lambda i: (lay.group(i), 0, 0)), vec, vec],
        out_specs=out_specs,
        out_shape=out_shape,
        name="ln_residual",
        compiler_params=pltpu.CompilerParams(dimension_semantics=("parallel",),
                                             vmem_limit_bytes=VMEM_LIMIT_BYTES),
    )(y, x, m3, g.reshape(1, d), b.reshape(1, d))


def _ffn_gate_kernel(zc_ref, zp_ref, zn_ref, w_ref, b_ref, o_ref, *, lay):
    i = pl.program_id(0)
    pos = lay.pos_in_seq(i)
    has_prev = pos > 0
    has_next = pos < lay.seq_tiles(i) - 1
    row = lax.broadcasted_iota(jnp.int32, (ROW_TILE, FF_CHUNK), 0)
    n_chunks = D_FF // FF_CHUNK

    def conv(col):
        sl = pl.ds(col, FF_CHUNK)
        z = zc_ref[:, sl].astype(jnp.float32)
        zprev_row = jnp.where(has_prev, zp_ref[SUBLANES - 1:SUBLANES, sl].astype(jnp.float32), 0.0)
        znext_row = jnp.where(has_next, zn_ref[0:1, sl].astype(jnp.float32), 0.0)
        zprev = jnp.where(row == 0, zprev_row, pltpu.roll(z, 1, axis=0))
        znext = jnp.where(row == ROW_TILE - 1, znext_row, pltpu.roll(z, ROW_TILE - 1, axis=0))
        return zprev * w_ref[0:1, sl] + b_ref[0:1, sl] + z * w_ref[1:2, sl] + znext * w_ref[2:3, sl]

    def body(j, carry):
        ca = pl.multiple_of(j * FF_CHUNK, FF_CHUNK)
        cb = pl.multiple_of(D_FF + j * FF_CHUNK, FF_CHUNK)
        a = conv(ca)
        bv = conv(cb)
        o_ref[:, pl.ds(ca, FF_CHUNK)] = (a * jax.nn.sigmoid(a) * bv).astype(o_ref.dtype)
        return carry

    lax.fori_loop(0, n_chunks, body, 0)
    o_ref[:, D_FF:FF_PAD] = jnp.zeros((ROW_TILE, FF_PAD - D_FF), o_ref.dtype)


def _ffn_gate(z, conv_w, conv_b, lay):
    n, c2 = z.shape
    rt8 = ROW_TILE // SUBLANES
    last8 = n // SUBLANES - 1
    return pl.pallas_call(
        functools.partial(_ffn_gate_kernel, lay=lay),
        grid=(lay.n_tiles,),
        in_specs=[pl.BlockSpec((ROW_TILE, c2), lambda i: (i, 0)),
                  pl.BlockSpec((SUBLANES, c2), lambda i: (jnp.maximum(i * rt8 - 1, 0), 0)),
                  pl.BlockSpec((SUBLANES, c2), lambda i: (jnp.minimum((i + 1) * rt8, last8), 0)),
                  pl.BlockSpec((3, c2), lambda i: (0, 0)),
                  pl.BlockSpec((1, c2), lambda i: (0, 0))],
        out_specs=pl.BlockSpec((ROW_TILE, FF_PAD), lambda i: (i, 0)),
        out_shape=jax.ShapeDtypeStruct((n, FF_PAD), jnp.bfloat16),
        name="ffn_gate",
        compiler_params=pltpu.CompilerParams(dimension_semantics=("parallel",),
                                             vmem_limit_bytes=VMEM_LIMIT_BYTES),
    )(z, z, z, conv_w, conv_b.reshape(1, c2))


def _cparams():
    return pltpu.CompilerParams(dimension_semantics=("parallel",), vmem_limit_bytes=VMEM_LIMIT_BYTES)


def _halo_specs(n_rows, halo, width, col):
    per_tile = ROW_TILE // halo
    last = n_rows // halo - 1
    prev = pl.BlockSpec((halo, width), lambda i: (jnp.maximum(i * per_tile - 1, 0), col))
    nxt = pl.BlockSpec((halo, width), lambda i: (jnp.minimum((i + 1) * per_tile, last), col))
    return prev, nxt


def _seq_edges(lay):
    i = pl.program_id(0)
    pos = lay.pos_in_seq(i)
    return pos, pos > 0, pos < lay.seq_tiles(i) - 1


def _row_layer_norm(x, g, b):
    xc = x - jnp.mean(x, -1, keepdims=True)
    var = jnp.mean(xc * xc, -1, keepdims=True)
    return (xc * lax.rsqrt(var + LN_EPS)) * g + b


def _conformer_kernel(a_ref, g_ref, ap_ref, gp_ref, an_ref, gn_ref, w_ref, b_ref, lg_ref, lb_ref,
                      o_ref, hext_ref, y_ref, *, lay):
    _, has_prev, has_next = _seq_edges(lay)
    glu = lambda a, g: a * jax.nn.sigmoid(g)
    hext_ref[0:CONV_HALO] = jnp.where(has_prev, glu(ap_ref[...], gp_ref[...]), 0.0)
    hext_ref[CONV_HALO:CONV_HALO + ROW_TILE] = glu(a_ref[...], g_ref[...])
    hext_ref[CONV_HALO + ROW_TILE:] = jnp.where(has_next, glu(an_ref[...], gn_ref[...]), 0.0)
    first = CONV_HALO - CONV_K // 2
    for rc in range(ROW_TILE // CONV_ROWS):
        base = first + rc * CONV_ROWS
        acc = hext_ref[base:base + CONV_ROWS] * w_ref[0:1] + b_ref[...]
        for j in range(1, CONV_K):
            acc = acc + hext_ref[base + j:base + j + CONV_ROWS] * w_ref[j:j + 1]
        y_ref[rc * CONV_ROWS:(rc + 1) * CONV_ROWS] = acc
    y = _row_layer_norm(y_ref[...], lg_ref[...], lb_ref[...])
    o_ref[...] = (y * jax.nn.sigmoid(y)).astype(o_ref.dtype)


def _conformer_group(f, conv_w, conv_b, ln_g, ln_b, lay):
    n = f.shape[0]
    cur = lambda col: pl.BlockSpec((ROW_TILE, W_C), lambda i: (i, col))
    pa, na = _halo_specs(n, CONV_HALO, W_C, COL_CA)
    pg, ng = _halo_specs(n, CONV_HALO, W_C, COL_CG)
    vec = pl.BlockSpec((1, W_C), lambda i: (0, 0))
    return pl.pallas_call(
        functools.partial(_conformer_kernel, lay=lay),
        grid=(lay.n_tiles,),
        in_specs=[cur(COL_CA), cur(COL_CG), pa, pg, na, ng,
                  pl.BlockSpec((CONV_K, W_C), lambda i: (0, 0)), vec, vec, vec],
        out_specs=pl.BlockSpec((ROW_TILE, W_C), lambda i: (i, 0)),
        out_shape=jax.ShapeDtypeStruct((n, W_C), jnp.bfloat16),
        name="conformer_group",
        scratch_shapes=[pltpu.VMEM((ROW_TILE + 2 * CONV_HALO, W_C), jnp.float32),
                        pltpu.VMEM((ROW_TILE, W_C), jnp.float32)],
        compiler_params=_cparams(),
    )(f, f, f, f, f, f, conv_w, conv_b.reshape(1, W_C), ln_g.reshape(1, W_C), ln_b.reshape(1, W_C))


def _gmlp_kernel(u_ref, v_ref, lg_ref, lb_ref, ws_ref, bias_ref, o_ref):
    v = _row_layer_norm(jax.nn.gelu(v_ref[...]), lg_ref[...], lb_ref[...]).astype(jnp.bfloat16)
    for ch in range(ROW_TILE // CHUNK):
        rows = slice(ch * CHUNK, (ch + 1) * CHUNK)
        for h in range(H_D):
            cols = slice(h * HEAD_D, (h + 1) * HEAD_D)
            s = jnp.dot(ws_ref[h], v[rows, cols], preferred_element_type=jnp.float32) + bias_ref[:, cols]
            o_ref[rows, cols] = (jax.nn.gelu(u_ref[rows, cols]) * s).astype(o_ref.dtype)


def _gmlp_group(f, ln_g, ln_b, ws, bs, lay):
    n = f.shape[0]
    vec = pl.BlockSpec((1, W_D), lambda i: (0, 0))
    bias = jnp.repeat(bs.T, HEAD_D, axis=1)
    return pl.pallas_call(
        _gmlp_kernel,
        grid=(lay.n_tiles,),
        in_specs=[pl.BlockSpec((ROW_TILE, W_D), lambda i: (i, COL_DU)),
                  pl.BlockSpec((ROW_TILE, W_D), lambda i: (i, COL_DV)),
                  vec, vec,
                  pl.BlockSpec((H_D, CHUNK, CHUNK), lambda i: (0, 0, 0)),
                  pl.BlockSpec((CHUNK, W_D), lambda i: (0, 0))],
        out_specs=pl.BlockSpec((ROW_TILE, W_D), lambda i: (i, 0)),
        out_shape=jax.ShapeDtypeStruct((n, W_D), jnp.bfloat16),
        name="gmlp_group",
        compiler_params=_cparams(),
    )(f, f, ln_g.reshape(1, W_D), ln_b.reshape(1, W_D), ws.astype(jnp.bfloat16), bias)


def _pool_kernel(x_ref, xp_ref, xn_ref, w_ref, b_ref, sc_ref, o_ref, hext_ref, *, lay):
    pos, has_prev, has_next = _seq_edges(lay)
    hext_ref[0:POOL_HALO] = jnp.where(has_prev, xp_ref[...], 0.0)
    hext_ref[POOL_HALO:POOL_HALO + ROW_TILE] = x_ref[...]
    hext_ref[POOL_HALO + ROW_TILE:] = jnp.where(has_next, xn_ref[...], 0.0)
    t = pos * ROW_TILE + lax.broadcasted_iota(jnp.int32, (ROW_TILE, POOL_GROUP), 0)
    t_len = lay.seq_tiles(pl.program_id(0)) * ROW_TILE
    for gi, win in enumerate(POOL_WINDOWS):
        cols = slice(gi * POOL_GROUP, (gi + 1) * POOL_GROUP)
        half = win // 2
        ssum = hext_ref[POOL_HALO - half:POOL_HALO - half + ROW_TILE, cols]
        for dlt in range(-half + 1, half):
            ssum = ssum + hext_ref[POOL_HALO + dlt:POOL_HALO + dlt + ROW_TILE, cols]
        cnt = (jnp.minimum(t + half, t_len) - jnp.maximum(t - half, 0)).astype(jnp.float32)
        pooled = (ssum / cnt - x_ref[:, cols]).astype(jnp.bfloat16)
        y = jnp.dot(pooled, w_ref[gi], preferred_element_type=jnp.float32) + b_ref[:, cols]
        o_ref[:, cols] = (y * sc_ref[:, cols]).astype(o_ref.dtype)


def _pool_group(f, pool_w, pool_b, pool_scale, lay):
    n = f.shape[0]
    prev, nxt = _halo_specs(n, POOL_HALO, W_B, COL_B)
    vec = pl.BlockSpec((1, W_B), lambda i: (0, 0))
    return pl.pallas_call(
        functools.partial(_pool_kernel, lay=lay),
        grid=(lay.n_tiles,),
        in_specs=[pl.BlockSpec((ROW_TILE, W_B), lambda i: (i, COL_B)), prev, nxt,
                  pl.BlockSpec(pool_w.shape, lambda i: (0, 0, 0)), vec, vec],
        out_specs=pl.BlockSpec((ROW_TILE, W_B), lambda i: (i, 0)),
        out_shape=jax.ShapeDtypeStruct((n, W_B), jnp.bfloat16),
        name="pool_group",
        scratch_shapes=[pltpu.VMEM((ROW_TILE + 2 * POOL_HALO, W_B), jnp.float32)],
        compiler_params=_cparams(),
    )(f, f, f, pool_w.astype(jnp.bfloat16), pool_b.reshape(1, W_B), pool_scale.reshape(1, W_B))


def _mm4_kernel(y0, y1, y2, y3, w_ref, o_ref):
    wk = w_ref.shape[0] // 4
    acc = jnp.dot(y0[...], w_ref[0:wk, :], preferred_element_type=jnp.float32)
    for k, y in enumerate((y1, y2, y3), start=1):
        acc = acc + jnp.dot(y[...], w_ref[k * wk:(k + 1) * wk, :], preferred_element_type=jnp.float32)
    o_ref[...] = acc


def _matmul4(ys, w, tm, tn):
    m = ys[0].shape[0]
    kdim, n = w.shape
    yspec = pl.BlockSpec((tm, kdim // 4), lambda i, j: (i, 0))
    return pl.pallas_call(
        _mm4_kernel,
        grid=(m // tm, n // tn),
        in_specs=[yspec] * 4 + [pl.BlockSpec((kdim, tn), lambda i, j: (0, j))],
        out_specs=pl.BlockSpec((tm, tn), lambda i, j: (i, j)),
        out_shape=jax.ShapeDtypeStruct((m, n), jnp.float32),
        name="dense_mm4",
        compiler_params=pltpu.CompilerParams(dimension_semantics=("parallel", "parallel"),
                                             vmem_limit_bytes=VMEM_LIMIT_BYTES),
    )(*ys, w)


def _head_ones():
    i = np.arange(LANES)
    return jnp.asarray((i[:, None] // HEAD_A) == (i[None, :] // HEAD_A), jnp.bfloat16)


def _head_sum(x, ones):
    hi = x.astype(jnp.bfloat16)
    lo = (x - hi.astype(jnp.float32)).astype(jnp.bfloat16)
    outs = []
    for s in range(x.shape[1] // LANES):
        cols = slice(s * LANES, (s + 1) * LANES)
        outs.append(jnp.dot(hi[:, cols], ones, preferred_element_type=jnp.float32)
                    + jnp.dot(lo[:, cols], ones, preferred_element_type=jnp.float32))
    return jnp.concatenate(outs, axis=1)


def _dot_split(a, w_hi, w_lo):
    a_hi = a.astype(jnp.bfloat16)
    a_lo = (a - a_hi.astype(jnp.float32)).astype(jnp.bfloat16)
    return (jnp.dot(a_hi, w_hi, preferred_element_type=jnp.float32)
            + jnp.dot(a_lo, w_hi, preferred_element_type=jnp.float32)
            + jnp.dot(a_hi, w_lo, preferred_element_type=jnp.float32))


def _token_shift(x, prev_row, next_row, mu_p, mu_n):
    rows = x.shape[0]
    row = lax.broadcasted_iota(jnp.int32, x.shape, 0)
    xp = jnp.where(row == 0, prev_row, pltpu.roll(x, 1, axis=0))
    xn = jnp.where(row == rows - 1, next_row, pltpu.roll(x, rows - 1, axis=0))
    return x + mu_p * (xp - x) + mu_n * (xn - x)


PREP_OUT_NAMES = ('at0', 'bt0', 'kt0', 'rt0', 'gam0', 'at1', 'bt1', 'kt1', 'rt1', 'gam1', 'v', 'g', 'bonus')


def _block_cumprod(w, pos, reverse):
    rows = w.shape[0]
    x = w
    s = 1
    while s < SCAN_SUB:
        if reverse:
            x = x * jnp.where(pos < SCAN_SUB - s, pltpu.roll(x, rows - s, axis=0), 1.0)
        else:
            x = x * jnp.where(pos >= s, pltpu.roll(x, s, axis=0), 1.0)
        s *= 2
    return x


def _block_prev(x, pos, reverse):
    rows = x.shape[0]
    if reverse:
        return jnp.where(pos == SCAN_SUB - 1, 1.0, pltpu.roll(x, rows - 1, axis=0))
    return jnp.where(pos == 0, 1.0, pltpu.roll(x, 1, axis=0))


def _rwkv_prep_kernel(x_ref, xp_ref, xn_ref, l_ref, lp_ref, ln_ref, mu_ref, mul_ref, whi_ref, wlo_ref,
                      vec_ref, ones_ref, *out_refs, lay):
    _, has_prev, has_next = _seq_edges(lay)
    edge = lambda ref, r, ok: jnp.where(ok, ref[r:r + 1, :], 0.0)
    x = _token_shift(x_ref[...], edge(xp_ref, SUBLANES - 1, has_prev), edge(xn_ref, 0, has_next),
                     mu_ref[0:1, :], mu_ref[1:2, :])
    lo = _token_shift(l_ref[...], edge(lp_ref, SUBLANES - 1, has_prev), edge(ln_ref, 0, has_next),
                      mul_ref[0:1, :], mul_ref[1:2, :])
    r = x[:, 0:W_A]
    k = x[:, W_A:2 * W_A]
    v = x[:, 2 * W_A:3 * W_A]
    col = lax.broadcasted_iota(jnp.int32, lo.shape, 1)
    act = jnp.where(col < D_DECAY_LORA, jnp.tanh(lo),
                    jnp.where(col < D_DECAY_LORA + D_AAA_LORA, lo, jax.nn.sigmoid(lo)))
    ones = ones_ref[...]
    pos = lax.broadcasted_iota(jnp.int32, r.shape, 0) % SCAN_SUB
    kk = k * vec_ref[0:1, :]
    kk = kk * lax.rsqrt(_head_sum(kk * kk, ones) + 1e-12)
    outs = {'v': v, 'g': _dot_split(act, whi_ref[4], wlo_ref[4])}
    bonus = None
    for d in range(2):
        w = jnp.exp(-DECAY_SCALE * jax.nn.sigmoid(vec_ref[3 + d:4 + d, :] + _dot_split(act, whi_ref[d], wlo_ref[d])))
        a = jax.nn.sigmoid(vec_ref[5 + d:6 + d, :] + _dot_split(act, whi_ref[2 + d], wlo_ref[2 + d]))
        kd = k * (1 + (a - 1) * vec_ref[1:2, :])
        term = _head_sum(r * kd * vec_ref[2:3, :], ones) * v
        bonus = term if bonus is None else bonus + term
        gam = _block_cumprod(w, pos, reverse=(d == 1))
        inv = 1.0 / gam
        outs['at%d' % d] = -kk * _block_prev(gam, pos, reverse=(d == 1))
        outs['bt%d' % d] = kk * a * inv
        outs['kt%d' % d] = kd * inv
        outs['rt%d' % d] = r * gam
        outs['gam%d' % d] = gam
    outs['bonus'] = bonus
    for ref, name in zip(out_refs, PREP_OUT_NAMES):
        ref[...] = outs[name]


def _rwkv_prep(x_rkv, x_lora, lora_col, P, lay):
    n = x_rkv.shape[0]
    o = 3 * W_A
    pr, nr = _halo_specs(n, SUBLANES, o, 0)
    plr, nlr = _halo_specs(n, SUBLANES, LORA_PAD, lora_col)
    full = lambda a: pl.BlockSpec(a.shape, lambda i: (0,) * a.ndim)
    mu = jnp.stack([P['mu_prev'][0:o], P['mu_next'][0:o]])
    pad = LORA_PAD - LORA_W
    mul = jnp.stack([jnp.pad(P['mu_prev'][o:], (0, pad)), jnp.pad(P['mu_next'][o:], (0, pad))])
    z = lambda rows: jnp.zeros((rows, W_A), jnp.float32)
    wcat = jnp.stack([
        jnp.concatenate([P['w_up'][0], z(LORA_PAD - D_DECAY_LORA)]),
        jnp.concatenate([P['w_up'][1], z(LORA_PAD - D_DECAY_LORA)]),
        jnp.concatenate([z(D_DECAY_LORA), P['a_up'][0], z(LORA_PAD - D_DECAY_LORA - D_AAA_LORA)]),
        jnp.concatenate([z(D_DECAY_LORA), P['a_up'][1], z(LORA_PAD - D_DECAY_LORA - D_AAA_LORA)]),
        jnp.concatenate([z(D_DECAY_LORA + D_AAA_LORA), P['g_up'], z(pad)]),
    ])
    w_hi = wcat.astype(jnp.bfloat16)
    w_lo = (wcat - w_hi.astype(jnp.float32)).astype(jnp.bfloat16)
    vec = jnp.stack([P['k_k'], P['k_a'], P['r_k'].reshape(W_A), P['w0'][0], P['w0'][1], P['a0'][0], P['a0'][1],
                     jnp.zeros((W_A,), jnp.float32)])
    ones = _head_ones()
    row_out = pl.BlockSpec((ROW_TILE, W_A), lambda i: (i, 0))
    outs = pl.pallas_call(
        functools.partial(_rwkv_prep_kernel, lay=lay),
        grid=(lay.n_tiles,),
        in_specs=[pl.BlockSpec((ROW_TILE, o), lambda i: (i, 0)), pr, nr,
                  pl.BlockSpec((ROW_TILE, LORA_PAD), lambda i: (i, lora_col)), plr, nlr,
                  full(mu), full(mul), full(w_hi), full(w_lo), full(vec), full(ones)],
        out_specs=[row_out] * N_PREP_OUT,
        out_shape=[jax.ShapeDtypeStruct((n, W_A), jnp.float32)] * N_PREP_OUT,
        name="rwkv_prep",
        compiler_params=_cparams(),
    )(x_rkv, x_rkv, x_rkv, x_lora, x_lora, x_lora, mu, mul, w_hi, w_lo, vec, ones)
    return dict(zip(PREP_OUT_NAMES, outs))


def _rwkv_post_kernel(yf_ref, yb_ref, bonus_ref, g_ref, vec_ref, ones_ref, o_ref):
    ones = ones_ref[...]
    y = yf_ref[...] + yb_ref[...]
    yc = y - _head_sum(y, ones) * (1.0 / HEAD_A)
    var = _head_sum(yc * yc, ones) * (1.0 / HEAD_A)
    y = yc * lax.rsqrt(var + GN_EPS)
    y = y * vec_ref[0:1, :] + vec_ref[1:2, :] + bonus_ref[...]
    o_ref[...] = (y * g_ref[...]).astype(o_ref.dtype)


def _rwkv_post(yf, yb, bonus, g, ln_g, ln_b, lay):
    n = yf.shape[0]
    row = pl.BlockSpec((ROW_TILE, W_A), lambda i: (i, 0))
    vec = jnp.stack([ln_g, ln_b] + [jnp.zeros_like(ln_g)] * 6)
    ones = _head_ones()
    full = lambda a: pl.BlockSpec(a.shape, lambda i: (0,) * a.ndim)
    return pl.pallas_call(
        _rwkv_post_kernel,
        grid=(lay.n_tiles,),
        in_specs=[row, row, row, row, full(vec), full(ones)],
        out_specs=row,
        out_shape=jax.ShapeDtypeStruct((n, W_A), jnp.bfloat16),
        name="rwkv_post",
        compiler_params=_cparams(),
    )(yf, yb, bonus, g, vec, ones)


def _grid_transpose(z, rows, cols):
    b, t, ch = z.shape
    return z.reshape(b, rows, cols, ch).transpose(0, 2, 1, 3).reshape(b, t, ch)


def _rwkv_group(f, P, lay, dims, s_ctx0, s_lat0, transposed):
    bc, tc, bl, tl = dims
    nc = bc * tc
    rows = tl // GRID_W
    if transposed:
        def permuted(cols):
            lat = _grid_transpose(f[nc:, cols].reshape(bl, tl, -1), rows, GRID_W)
            return jnp.concatenate([f[:nc, cols], lat.reshape(bl * tl, -1)], axis=0)
        x_rkv = permuted(slice(0, 3 * W_A))
        x_lora = permuted(slice(LORA_COL * W_A, LORA_COL * W_A + LORA_PAD))
        lora_col = 0
    else:
        x_rkv, x_lora, lora_col = f, f, LORA_COL * W_A // LORA_PAD
    ops = _rwkv_prep(x_rkv, x_lora, lora_col, P, lay)
    yf_c, yb_c, s_fin = _rwkv_scan(ops, s_ctx0, 0, bc, tc)
    yf_l, yb_l, _ = _rwkv_scan(ops, s_lat0, nc, bl, tl)
    out = _rwkv_post(jnp.concatenate([yf_c, yf_l], axis=0), jnp.concatenate([yb_c, yb_l], axis=0),
                     ops['bonus'], ops['g'], P['rwkv_ln_g'], P['rwkv_ln_b'], lay)
    if transposed:
        lat = _grid_transpose(out[nc:].reshape(bl, tl, W_A), GRID_W, rows)
        out = jnp.concatenate([out[:nc], lat.reshape(bl * tl, W_A)], axis=0)
    return out, s_fin


def _modulation(cvec, P):
    m = jnp.dot(jax.nn.silu(cvec), P['w_ada'], precision=lax.Precision.HIGHEST) + P['b_ada']
    return m.reshape(cvec.shape[0], N_MOD, D_MODEL)


def _pack_w_in(w_in):
    o = 3 * W_A
    parts = [w_in[:, 0:o], w_in[:, COLS_A:], w_in[:, o:COLS_A]]
    parts = [p.astype(jnp.bfloat16) for p in parts]
    parts.append(jnp.zeros((w_in.shape[0], IN_PAD - IN_COLS), jnp.bfloat16))
    return jnp.concatenate(parts, axis=1)


def kernel(x_prompt, x_sample, c, state_rwkv, c_ctx, w_ada, b_ada, w_in, mu_prev, mu_next, rwkv_g_up, rwkv_w0, rwkv_w_up, rwkv_a0, rwkv_a_up, rwkv_k_k, rwkv_k_a, rwkv_r_k, rwkv_ln_g, rwkv_ln_b, pool_w, pool_b, pool_scale, conf_conv_w, conf_conv_b, conf_ln_g, conf_ln_b, gmlp_ln_g, gmlp_ln_b, gmlp_ws, gmlp_bs, w_out, ln1_g, ln1_b, ffn_w_up, ffn_conv_w, ffn_conv_b, ffn_w_down, ln2_g, ln2_b):
    bc, tc, d = x_prompt.shape
    bl, tl, _ = x_sample.shape
    nc = bc * tc
    lay = TokenLayout(bc, tc, bl, tl)
    s_ctx0 = jnp.zeros((bc, 2, H_A, HEAD_A, HEAD_A), jnp.float32)
    cvec = jnp.concatenate([c_ctx[None, :], c], axis=0)
    mods = [_modulation(cvec, {'w_ada': w_ada[l], 'b_ada': b_ada[l]}) for l in range(DEPTH)]
    x = jnp.concatenate([x_prompt.reshape(nc, d), x_sample.reshape(bl * tl, d)], axis=0)
    h = _modulate(x, jnp.stack([mods[0][:, 1], mods[0][:, 0]], axis=1), lay)
    new_states = []
    for l in range(DEPTH):
        P = {
            'mu_prev': mu_prev[l], 'mu_next': mu_next[l], 'g_up': rwkv_g_up[l],
            'w0': rwkv_w0[l], 'w_up': rwkv_w_up[l], 'a0': rwkv_a0[l], 'a_up': rwkv_a_up[l],
            'k_k': rwkv_k_k[l], 'k_a': rwkv_k_a[l], 'r_k': rwkv_r_k[l],
            'rwkv_ln_g': rwkv_ln_g[l], 'rwkv_ln_b': rwkv_ln_b[l],
        }
        w_in_p = _pack_w_in(w_in[l])
        w_out_b = w_out[l].astype(jnp.bfloat16)
        w_up_b = ffn_w_up[l].astype(jnp.bfloat16)
        w_dn_b = jnp.concatenate([ffn_w_down[l].astype(jnp.bfloat16),
                                  jnp.zeros((FF_PAD - D_FF, d), jnp.bfloat16)], axis=0)

        m = mods[l]

        f = _matmul(h, w_in_p, jnp.float32, 1024, 512, D_MODEL)
        y_a, s_fin = _rwkv_group(f, P, lay, (bc, tc, bl, tl), s_ctx0, state_rwkv[:, l], l % 2 == 1)
        new_states.append(s_fin)
        y_b = _pool_group(f, pool_w[l], pool_b[l], pool_scale[l], lay)
        y_c = _conformer_group(f, conf_conv_w[l], conf_conv_b[l], conf_ln_g[l], conf_ln_b[l], lay)
        y_d = _gmlp_group(f, gmlp_ln_g[l], gmlp_ln_b[l], gmlp_ws[l], gmlp_bs[l], lay)
        y = _matmul4([y_a, y_b, y_c, y_d], w_out_b, 1024, 1024)
        x, h = _ln_residual(y, x, jnp.stack([m[:, 2], m[:, 4], m[:, 3]], axis=1), ln1_g[l], ln1_b[l], lay, True)

        z = _matmul(h, w_up_b, jnp.bfloat16, 1024, 512, D_MODEL)
        g = _ffn_gate(z, ffn_conv_w[l], ffn_conv_b[l], lay)
        y = _matmul(g, w_dn_b, jnp.float32, 1024, 1024, FF_PAD // 4)
        if l + 1 < DEPTH:
            mn = mods[l + 1]
            x, h = _ln_residual(y, x, jnp.stack([m[:, 5], mn[:, 1], mn[:, 0]], axis=1), ln2_g[l], ln2_b[l], lay, True)
        else:
            (x,) = _ln_residual(y, x, jnp.stack([m[:, 5], m[:, 5], m[:, 5]], axis=1), ln2_g[l], ln2_b[l], lay, False)

    new_state_rwkv = jnp.stack(new_states, axis=1).astype(x_prompt.dtype)
    return (x[:nc].reshape(bc, tc, d), x[nc:].reshape(bl, tl, d), new_state_rwkv)
```

```python
import functools
import math

import numpy as np
import jax
import jax.numpy as jnp
from jax import lax
from jax.experimental import pallas as pl
from jax.experimental.pallas import tpu as pltpu

D_MODEL = 4096
DEPTH = 2
GRID_W = 64
W_A = D_MODEL // 4
W_B = D_MODEL // 4
W_C = D_MODEL // 4
W_D = D_MODEL - W_A - W_B - W_C
HEAD_A = 64
H_A = W_A // HEAD_A
D_DECAY_LORA = max(32, int(round(W_A ** 0.5 * 1.8 / 32)) * 32)
D_AAA_LORA = max(32, int(round(W_A ** 0.5 * 1.8 / 32)) * 32)
D_GATE_LORA = max(32, int(round(W_A ** 0.8 * 0.6 / 32)) * 32)
COLS_A = 3 * W_A + D_DECAY_LORA + D_AAA_LORA + D_GATE_LORA
DECAY_SCALE = math.exp(-0.5)
GN_EPS = 64e-5
POOL_WINDOWS = (2, 4, 8, 16)
POOL_GROUP = W_B // len(POOL_WINDOWS)
CONV_K = 31
CHUNK = 128
H_D = 8
HEAD_D = W_D // H_D
IN_COLS = COLS_A + W_B + 2 * W_C + 2 * W_D
D_FF = ((8 * D_MODEL // 3 + 255) // 256) * 256
FFN_K = 3
ALPHA = (2.0 * DEPTH) ** 0.25
LN_EPS = 1e-5
N_MOD = 6
IN_PAD = ((IN_COLS + 511) // 512) * 512
FF_PAD = ((D_FF + 1023) // 1024) * 1024

LANES = 128
SUBLANES = 8
VMEM_LIMIT_BYTES = 56 * 1024 * 1024

SCAN_TB = 128
SCAN_SUB = 32
SCAN_PARTS = 3
N_KEYVEC = 5
N_STEPVEC = 4
SCAN_TRIP = 4

ROW_TILE = 256
FF_CHUNK = 256
ADA_TN = 1024
CONV_HALO = 16
POOL_HALO = 8
CONV_ROWS = 32
COL_B, COL_CA, COL_CG, COL_DU, COL_DV, LORA_COL = 3, 4, 5, 6, 7, 8
LORA_W = D_DECAY_LORA + D_AAA_LORA + D_GATE_LORA
LORA_PAD = IN_PAD - LORA_COL * W_A
N_PREP_OUT = 13


def _mm_kernel(x_ref, w_ref, o_ref, acc_ref, *, nk):
    k = pl.program_id(2)
    part = jnp.dot(x_ref[...], w_ref[...], preferred_element_type=jnp.float32)
    if nk == 1:
        o_ref[...] = part.astype(o_ref.dtype)
    else:
        @pl.when(k == 0)
        def _():
            acc_ref[...] = part

        @pl.when(k > 0)
        def _():
            acc_ref[...] += part

        @pl.when(k == nk - 1)
        def _():
            o_ref[...] = acc_ref[...].astype(o_ref.dtype)


def _matmul(x, w, out_dtype, tm, tn, tk):
    m, kdim = x.shape
    _, n = w.shape
    assert m % tm == 0 and n % tn == 0 and kdim % tk == 0
    nk = kdim // tk
    return pl.pallas_call(
        functools.partial(_mm_kernel, nk=nk),
        grid=(m // tm, n // tn, nk),
        in_specs=[
            pl.BlockSpec((tm, tk), lambda i, j, k: (i, k)),
            pl.BlockSpec((tk, tn), lambda i, j, k: (k, j)),
        ],
        out_specs=pl.BlockSpec((tm, tn), lambda i, j, k: (i, j)),
        out_shape=jax.ShapeDtypeStruct((m, n), out_dtype),
        name="dense_mm",
        scratch_shapes=[pltpu.VMEM((tm, tn), jnp.float32)],
        compiler_params=pltpu.CompilerParams(
            dimension_semantics=("parallel", "parallel", "arbitrary"),
            vmem_limit_bytes=VMEM_LIMIT_BYTES,
        ),
    )(x, w)


def _sel_table():
    c = np.arange(2 * LANES)[:, None]
    n = np.arange(2 * LANES)[None, :]
    e = np.zeros((SCAN_SUB // 2, 2 * LANES, 2 * LANES), np.float32)
    for p in range(SCAN_SUB // 2):
        t = 2 * p + n // LANES
        e[p] = (((c % LANES) < SCAN_PARTS * SCAN_SUB) & ((c % SCAN_SUB) == t)
                & ((c // LANES) == ((n % LANES) // HEAD_A)))
    return jnp.asarray(e, jnp.bfloat16)


def _split3_rows(x):
    hi = x.astype(jnp.bfloat16).astype(jnp.float32)
    r1 = x - hi
    mid = r1.astype(jnp.bfloat16).astype(jnp.float32)
    lo = r1 - mid
    return jnp.concatenate([hi, mid, lo, jnp.zeros_like(x)], axis=0)


def _scan_kernel(*refs, nblk):
    key_refs = (refs[0:5], refs[6:11])
    v_refs = (refs[5], refs[11])
    e_ref, s0_ref = refs[12], refs[13]
    y_refs = (refs[14], refs[15])
    sfin_ref, st_ref, lhs_ref = refs[16], refs[17], refs[18]
    tb = pl.program_id(2)

    @pl.when(tb == 0)
    def _():
        st_ref[...] = s0_ref[0, :, 0]

    lane = lax.broadcasted_iota(jnp.int32, (SCAN_SUB, LANES), 1)
    first_head = lane < HEAD_A

    half = N_KEYVEC * HEAD_A
    npairs = SCAN_SUB // 2

    def build_lhs(slot, offs):
        for c in range(2):
            xs = [ref[pl.ds(offs[c], SCAN_SUB), :] for ref in key_refs[c]]
            for pi, (i1, i2) in enumerate(((0, 1), (2, 3), (4, None))):
                x1 = xs[i1]
                xr1 = pltpu.roll(x1, HEAD_A, axis=1)
                if i2 is None:
                    comb_a, comb_b = x1, xr1
                else:
                    x2 = xs[i2]
                    xr2 = pltpu.roll(x2, HEAD_A, axis=1)
                    comb_a = jnp.where(first_head, x1, xr2)
                    comb_b = jnp.where(first_head, xr1, x2)
                mt_a = _split3_rows(comb_a).T.astype(jnp.bfloat16)
                mt_b = _split3_rows(comb_b).T.astype(jnp.bfloat16)
                r0 = (c * N_KEYVEC + 2 * pi) * HEAD_A
                lhs_ref[slot, r0:r0 + HEAD_A, 0:LANES] = mt_a[0:HEAD_A]
                lhs_ref[slot, r0:r0 + HEAD_A, LANES:2 * LANES] = mt_b[0:HEAD_A]
                if i2 is not None:
                    lhs_ref[slot, r0 + HEAD_A:r0 + 2 * HEAD_A, 0:LANES] = mt_a[HEAD_A:2 * HEAD_A]
                    lhs_ref[slot, r0 + HEAD_A:r0 + 2 * HEAD_A, LANES:2 * LANES] = mt_b[HEAD_A:2 * HEAD_A]

    def sub_offsets(q):
        return (pl.multiple_of(q * SCAN_SUB, SCAN_SUB),
                pl.multiple_of(SCAN_TB - (q + 1) * SCAN_SUB, SCAN_SUB))

    def run_sub(slot, offs, hs):
        vs = [v_refs[c][pl.ds(offs[c], SCAN_SUB), :] for c in range(2)]
        ys = ([], [])
        gs = list(hs)
        step_rows = N_STEPVEC * HEAD_A
        for p in range(npairs):
            tiles = (jnp.dot(lhs_ref[slot, 0:step_rows, :], e_ref[p], preferred_element_type=jnp.float32),
                     jnp.dot(lhs_ref[slot, half:half + step_rows, :], e_ref[npairs - 1 - p],
                             preferred_element_type=jnp.float32))
            for s in range(2):
                u = 2 * p + s
                for c in range(2):
                    col = s if c == 0 else 1 - s
                    row = u if c == 0 else SCAN_SUB - 1 - u
                    a_t, b_t, k_t, r_t = (
                        tiles[c][k * HEAD_A:(k + 1) * HEAD_A, col * LANES:(col + 1) * LANES]
                        for k in range(N_STEPVEC))
                    g = gs[c]
                    sa = jnp.sum(g * a_t, axis=0, keepdims=True)
                    g = g + sa * b_t + vs[c][row:row + 1, :] * k_t
                    ys[c].append(jnp.sum(g * r_t, axis=0, keepdims=True))
                    gs[c] = g
        y_refs[0][pl.ds(offs[0], SCAN_SUB), :] = jnp.concatenate(ys[0], axis=0)
        y_refs[1][pl.ds(offs[1], SCAN_SUB), :] = jnp.concatenate(ys[1][::-1], axis=0)
        gam_f = jnp.dot(lhs_ref[slot, step_rows:half, :], e_ref[npairs - 1],
                        preferred_element_type=jnp.float32)[:, LANES:2 * LANES]
        gam_b = jnp.dot(lhs_ref[slot, half + step_rows:2 * half, :], e_ref[0],
                        preferred_element_type=jnp.float32)[:, 0:LANES]
        return [gs[0] * gam_f, gs[1] * gam_b]

    def trip(g, carry):
        hs = list(carry)
        offs = [sub_offsets(g * SCAN_TRIP + k) for k in range(SCAN_TRIP)]
        for k in range(SCAN_TRIP):
            build_lhs(k, offs[k])
        for k in range(SCAN_TRIP):
            hs = run_sub(k, offs[k], hs)
        return tuple(hs)

    hf, hb = lax.fori_loop(0, SCAN_TB // (SCAN_SUB * SCAN_TRIP), trip, (st_ref[0], st_ref[1]))
    st_ref[0] = hf
    st_ref[1] = hb

    @pl.when(tb == nblk - 1)
    def _():
        sfin_ref[0, 0, 0] = hf
        sfin_ref[0, 1, 0] = hb


def _rwkv_scan(ops, s0, row0, bsz, t_len):
    npair = H_A // 2
    nblk = t_len // SCAN_TB
    blk0 = row0 // SCAN_TB
    s0p = s0.reshape(bsz, 2, npair, 2, HEAD_A, HEAD_A).transpose(0, 1, 2, 5, 3, 4)
    s0p = s0p.reshape(bsz, 2, npair, HEAD_A, 2 * HEAD_A)
    ins = []
    for d in range(2):
        ins += [ops[n + str(d)] for n in ('at', 'bt', 'kt', 'rt', 'gam')] + [ops['v']]
    blk = (SCAN_TB, LANES)
    fwd_in = pl.BlockSpec(blk, lambda b, p, t: (blk0 + b * nblk + t, p))
    bwd_in = pl.BlockSpec(blk, lambda b, p, t: (blk0 + b * nblk + nblk - 1 - t, p))
    fwd_out = pl.BlockSpec(blk, lambda b, p, t: (b * nblk + t, p))
    bwd_out = pl.BlockSpec(blk, lambda b, p, t: (b * nblk + nblk - 1 - t, p))
    st_spec = pl.BlockSpec((1, 2, 1, HEAD_A, LANES), lambda b, p, t: (b, 0, p, 0, 0))
    e_tab = _sel_table()
    in_specs = ([fwd_in] * 6 + [bwd_in] * 6
                + [pl.BlockSpec(e_tab.shape, lambda b, p, t: (0, 0, 0)), st_spec])
    out_shape = [
        jax.ShapeDtypeStruct((bsz * t_len, W_A), jnp.float32),
        jax.ShapeDtypeStruct((bsz * t_len, W_A), jnp.float32),
        jax.ShapeDtypeStruct((bsz, 2, npair, HEAD_A, LANES), jnp.float32),
    ]
    yf, yb, sfin = pl.pallas_call(
        functools.partial(_scan_kernel, nblk=nblk),
        grid=(bsz, npair, nblk),
        in_specs=in_specs,
        out_specs=[fwd_out, bwd_out, st_spec],
        out_shape=out_shape,
        name="rwkv_scan",
        scratch_shapes=[pltpu.VMEM((2, HEAD_A, LANES), jnp.float32),
                        pltpu.VMEM((SCAN_TRIP, 2 * N_KEYVEC * HEAD_A, 2 * LANES), jnp.bfloat16)],
        compiler_params=pltpu.CompilerParams(
            dimension_semantics=("parallel", "parallel", "arbitrary"),
            vmem_limit_bytes=VMEM_LIMIT_BYTES,
        ),
    )(*ins, e_tab, s0p)
    sfin = sfin.reshape(bsz, 2, npair, HEAD_A, 2, HEAD_A).transpose(0, 1, 2, 4, 5, 3)
    return yf, yb, sfin.reshape(bsz, 2, H_A, HEAD_A, HEAD_A)


class TokenLayout:
    def __init__(self, bc, tc, bl, tl):
        assert tc % ROW_TILE == 0 and tl % ROW_TILE == 0
        self.n_rows = bc * tc + bl * tl
        self.n_tiles = self.n_rows // ROW_TILE
        self.ctx_tiles = bc * tc // ROW_TILE
        self.ctx_seq_tiles = tc // ROW_TILE
        self.lat_seq_tiles = tl // ROW_TILE

    def group(self, i):
        return jnp.where(i < self.ctx_tiles, 0, 1 + (i - self.ctx_tiles) // self.lat_seq_tiles)

    def pos_in_seq(self, i):
        return jnp.where(i < self.ctx_tiles, i % self.ctx_seq_tiles, (i - self.ctx_tiles) % self.lat_seq_tiles)

    def seq_tiles(self, i):
        return jnp.where(i < self.ctx_tiles, self.ctx_seq_tiles, self.lat_seq_tiles)


def _modulate_kernel(x_ref, m_ref, h_ref):
    x = x_ref[...]
    h_ref[...] = (x * (1 + m_ref[0, 0:1, :]) + m_ref[0, 1:2, :]).astype(h_ref.dtype)


def _modulate(x, m2, lay):
    n, d = x.shape
    return pl.pallas_call(
        _modulate_kernel,
        grid=(lay.n_tiles,),
        in_specs=[pl.BlockSpec((ROW_TILE, d), lambda i: (i, 0)),
                  pl.BlockSpec((1, 2, d), lambda i: (lay.group(i), 0, 0))],
        out_specs=pl.BlockSpec((ROW_TILE, d), lambda i: (i, 0)),
        out_shape=jax.ShapeDtypeStruct((n, d), jnp.bfloat16),
        name="modulate",
        compiler_params=pltpu.CompilerParams(dimension_semantics=("parallel",),
                                             vmem_limit_bytes=VMEM_LIMIT_BYTES),
    )(x, m2)


def _ln_residual_kernel(y_ref, x_ref, m_ref, g_ref, b_ref, *o_refs, lay, split):
    v = ALPHA * x_ref[...] + m_ref[0, 0:1, :] * y_ref[...]
    vc = v - jnp.mean(v, -1, keepdims=True)
    var = jnp.mean(vc * vc, -1, keepdims=True)
    xn = (vc * lax.rsqrt(var + LN_EPS)) * g_ref[...] + b_ref[...]
    if split:
        is_ctx = pl.program_id(0) < lay.ctx_tiles

        @pl.when(is_ctx)
        def _():
            o_refs[0][...] = xn

        @pl.when(jnp.logical_not(is_ctx))
        def _():
            o_refs[1][...] = xn
    else:
        o_refs[0][...] = xn
        o_refs[1][...] = (xn * (1 + m_ref[0, 1:2, :]) + m_ref[0, 2:3, :]).astype(jnp.bfloat16)


def _ln_residual(y, x, m3, g, b, lay, split):
    n, d = x.shape
    row = pl.BlockSpec((ROW_TILE, d), lambda i: (i, 0))
    vec = pl.BlockSpec((1, d), lambda i: (0, 0))
    if split:
        n_ctx = lay.ctx_tiles * ROW_TILE
        out_shape = [jax.ShapeDtypeStruct((n_ctx, d), jnp.float32),
                     jax.ShapeDtypeStruct((n - n_ctx, d), jnp.float32)]
        out_specs = [pl.BlockSpec((ROW_TILE, d), lambda i: (jnp.minimum(i, lay.ctx_tiles - 1), 0)),
                     pl.BlockSpec((ROW_TILE, d), lambda i: (jnp.maximum(i - lay.ctx_tiles, 0), 0))]
    else:
        out_shape = [jax.ShapeDtypeStruct((n, d), jnp.float32), jax.ShapeDtypeStruct((n, d), jnp.bfloat16)]
        out_specs = [row, row]
    return pl.pallas_call(
        functools.partial(_ln_residual_kernel, lay=lay, split=split),
        grid=(lay.n_tiles,),
        in_specs=[row, row, pl.BlockSpec((1, 3, d), lambda i: (lay.group(i), 0, 0)), vec, vec],
        out_specs=out_specs,
        out_shape=out_shape,
        name="ln_residual",
        compiler_params=pltpu.CompilerParams(dimension_semantics=("arbitrary",),
                                             vmem_limit_bytes=VMEM_LIMIT_BYTES),
    )(y, x, m3, g.reshape(1, d), b.reshape(1, d))


def _ffn_gate_kernel(zc_ref, zp_ref, zn_ref, w_ref, b_ref, o_ref, *, lay):
    i = pl.program_id(0)
    pos = lay.pos_in_seq(i)
    has_prev = pos > 0
    has_next = pos < lay.seq_tiles(i) - 1
    row = lax.broadcasted_iota(jnp.int32, (ROW_TILE, FF_CHUNK), 0)
    n_chunks = D_FF // FF_CHUNK

    def conv(col):
        sl = pl.ds(col, FF_CHUNK)
        z = zc_ref[:, sl].astype(jnp.float32)
        zprev_row = jnp.where(has_prev, zp_ref[SUBLANES - 1:SUBLANES, sl].astype(jnp.float32), 0.0)
        znext_row = jnp.where(has_next, zn_ref[0:1, sl].astype(jnp.float32), 0.0)
        zprev = jnp.where(row == 0, zprev_row, pltpu.roll(z, 1, axis=0))
        znext = jnp.where(row == ROW_TILE - 1, znext_row, pltpu.roll(z, ROW_TILE - 1, axis=0))
        return zprev * w_ref[0:1, sl] + b_ref[0:1, sl] + z * w_ref[1:2, sl] + znext * w_ref[2:3, sl]

    def body(j, carry):
        ca = pl.multiple_of(j * FF_CHUNK, FF_CHUNK)
        cb = pl.multiple_of(D_FF + j * FF_CHUNK, FF_CHUNK)
        a = conv(ca)
        bv = conv(cb)
        o_ref[:, pl.ds(ca, FF_CHUNK)] = (a * jax.nn.sigmoid(a) * bv).astype(o_ref.dtype)
        return carry

    lax.fori_loop(0, n_chunks, body, 0)
    o_ref[:, D_FF:FF_PAD] = jnp.zeros((ROW_TILE, FF_PAD - D_FF), o_ref.dtype)


def _ffn_gate(z, conv_w, conv_b, lay):
    n, c2 = z.shape
    rt8 = ROW_TILE // SUBLANES
    last8 = n // SUBLANES - 1
    return pl.pallas_call(
        functools.partial(_ffn_gate_kernel, lay=lay),
        grid=(lay.n_tiles,),
        in_specs=[pl.BlockSpec((ROW_TILE, c2), lambda i: (i, 0)),
                  pl.BlockSpec((SUBLANES, c2), lambda i: (jnp.maximum(i * rt8 - 1, 0), 0)),
                  pl.BlockSpec((SUBLANES, c2), lambda i: (jnp.minimum((i + 1) * rt8, last8), 0)),
                  pl.BlockSpec((3, c2), lambda i: (0, 0)),
                  pl.BlockSpec((1, c2), lambda i: (0, 0))],
        out_specs=pl.BlockSpec((ROW_TILE, FF_PAD), lambda i: (i, 0)),
        out_shape=jax.ShapeDtypeStruct((n, FF_PAD), jnp.bfloat16),
        name="ffn_gate",
        compiler_params=pltpu.CompilerParams(dimension_semantics=("parallel",),
                                             vmem_limit_bytes=VMEM_LIMIT_BYTES),
    )(z, z, z, conv_w, conv_b.reshape(1, c2))


def _cparams():
    return pltpu.CompilerParams(dimension_semantics=("parallel",), vmem_limit_bytes=VMEM_LIMIT_BYTES)


def _halo_specs(n_rows, halo, width, col):
    per_tile = ROW_TILE // halo
    last = n_rows // halo - 1
    prev = pl.BlockSpec((halo, width), lambda i: (jnp.maximum(i * per_tile - 1, 0), col))
    nxt = pl.BlockSpec((halo, width), lambda i: (jnp.minimum((i + 1) * per_tile, last), col))
    return prev, nxt


def _seq_edges(lay):
    i = pl.program_id(0)
    pos = lay.pos_in_seq(i)
    return pos, pos > 0, pos < lay.seq_tiles(i) - 1


def _row_layer_norm(x, g, b):
    xc = x - jnp.mean(x, -1, keepdims=True)
    var = jnp.mean(xc * xc, -1, keepdims=True)
    return (xc * lax.rsqrt(var + LN_EPS)) * g + b


def _conformer_kernel(a_ref, g_ref, ap_ref, gp_ref, an_ref, gn_ref, w_ref, b_ref, lg_ref, lb_ref,
                      o_ref, hext_ref, y_ref, *, lay):
    _, has_prev, has_next = _seq_edges(lay)
    glu = lambda a, g: a * jax.nn.sigmoid(g)
    hext_ref[0:CONV_HALO] = jnp.where(has_prev, glu(ap_ref[...], gp_ref[...]), 0.0)
    hext_ref[CONV_HALO:CONV_HALO + ROW_TILE] = glu(a_ref[...], g_ref[...])
    hext_ref[CONV_HALO + ROW_TILE:] = jnp.where(has_next, glu(an_ref[...], gn_ref[...]), 0.0)
    first = CONV_HALO - CONV_K // 2
    for rc in range(ROW_TILE // CONV_ROWS):
        base = first + rc * CONV_ROWS
        acc = hext_ref[base:base + CONV_ROWS] * w_ref[0:1] + b_ref[...]
        for j in range(1, CONV_K):
            acc = acc + hext_ref[base + j:base + j + CONV_ROWS] * w_ref[j:j + 1]
        y_ref[rc * CONV_ROWS:(rc + 1) * CONV_ROWS] = acc
    y = _row_layer_norm(y_ref[...], lg_ref[...], lb_ref[...])
    o_ref[...] = (y * jax.nn.sigmoid(y)).astype(o_ref.dtype)


def _conformer_group(f, conv_w, conv_b, ln_g, ln_b, lay):
    n = f.shape[0]
    cur = lambda col: pl.BlockSpec((ROW_TILE, W_C), lambda i: (i, col))
    pa, na = _halo_specs(n, CONV_HALO, W_C, COL_CA)
    pg, ng = _halo_specs(n, CONV_HALO, W_C, COL_CG)
    vec = pl.BlockSpec((1, W_C), lambda i: (0, 0))
    return pl.pallas_call(
        functools.partial(_conformer_kernel, lay=lay),
        grid=(lay.n_tiles,),
        in_specs=[cur(COL_CA), cur(COL_CG), pa, pg, na, ng,
                  pl.BlockSpec((CONV_K, W_C), lambda i: (0, 0)), vec, vec, vec],
        out_specs=pl.BlockSpec((ROW_TILE, W_C), lambda i: (i, 0)),
        out_shape=jax.ShapeDtypeStruct((n, W_C), jnp.bfloat16),
        name="conformer_group",
        scratch_shapes=[pltpu.VMEM((ROW_TILE + 2 * CONV_HALO, W_C), jnp.float32),
                        pltpu.VMEM((ROW_TILE, W_C), jnp.float32)],
        compiler_params=_cparams(),
    )(f, f, f, f, f, f, conv_w, conv_b.reshape(1, W_C), ln_g.reshape(1, W_C), ln_b.reshape(1, W_C))


def _gmlp_kernel(u_ref, v_ref, lg_ref, lb_ref, ws_ref, bias_ref, o_ref):
    v = _row_layer_norm(jax.nn.gelu(v_ref[...]), lg_ref[...], lb_ref[...]).astype(jnp.bfloat16)
    for ch in range(ROW_TILE // CHUNK):
        rows = slice(ch * CHUNK, (ch + 1) * CHUNK)
        for h in range(H_D):
            cols = slice(h * HEAD_D, (h + 1) * HEAD_D)
            s = jnp.dot(ws_ref[h], v[rows, cols], preferred_element_type=jnp.float32) + bias_ref[:, cols]
            o_ref[rows, cols] = (jax.nn.gelu(u_ref[rows, cols]) * s).astype(o_ref.dtype)


def _gmlp_group(f, ln_g, ln_b, ws, bs, lay):
    n = f.shape[0]
    vec = pl.BlockSpec((1, W_D), lambda i: (0, 0))
    bias = jnp.repeat(bs.T, HEAD_D, axis=1)
    return pl.pallas_call(
        _gmlp_kernel,
        grid=(lay.n_tiles,),
        in_specs=[pl.BlockSpec((ROW_TILE, W_D), lambda i: (i, COL_DU)),
                  pl.BlockSpec((ROW_TILE, W_D), lambda i: (i, COL_DV)),
                  vec, vec,
                  pl.BlockSpec((H_D, CHUNK, CHUNK), lambda i: (0, 0, 0)),
                  pl.BlockSpec((CHUNK, W_D), lambda i: (0, 0))],
        out_specs=pl.BlockSpec((ROW_TILE, W_D), lambda i: (i, 0)),
        out_shape=jax.ShapeDtypeStruct((n, W_D), jnp.bfloat16),
        name="gmlp_group",
        compiler_params=_cparams(),
    )(f, f, ln_g.reshape(1, W_D), ln_b.reshape(1, W_D), ws.astype(jnp.bfloat16), bias)


def _pool_kernel(x_ref, xp_ref, xn_ref, w_ref, b_ref, sc_ref, o_ref, hext_ref, *, lay):
    pos, has_prev, has_next = _seq_edges(lay)
    hext_ref[0:POOL_HALO] = jnp.where(has_prev, xp_ref[...], 0.0)
    hext_ref[POOL_HALO:POOL_HALO + ROW_TILE] = x_ref[...]
    hext_ref[POOL_HALO + ROW_TILE:] = jnp.where(has_next, xn_ref[...], 0.0)
    t = pos * ROW_TILE + lax.broadcasted_iota(jnp.int32, (ROW_TILE, POOL_GROUP), 0)
    t_len = lay.seq_tiles(pl.program_id(0)) * ROW_TILE
    for gi, win in enumerate(POOL_WINDOWS):
        cols = slice(gi * POOL_GROUP, (gi + 1) * POOL_GROUP)
        half = win // 2
        ssum = hext_ref[POOL_HALO - half:POOL_HALO - half + ROW_TILE, cols]
        for dlt in range(-half + 1, half):
            ssum = ssum + hext_ref[POOL_HALO + dlt:POOL_HALO + dlt + ROW_TILE, cols]
        cnt = (jnp.minimum(t + half, t_len) - jnp.maximum(t - half, 0)).astype(jnp.float32)
        pooled = (ssum / cnt - x_ref[:, cols]).astype(jnp.bfloat16)
        y = jnp.dot(pooled, w_ref[gi], preferred_element_type=jnp.float32) + b_ref[:, cols]
        o_ref[:, cols] = (y * sc_ref[:, cols]).astype(o_ref.dtype)


def _pool_group(f, pool_w, pool_b, pool_scale, lay):
    n = f.shape[0]
    prev, nxt = _halo_specs(n, POOL_HALO, W_B, COL_B)
    vec = pl.BlockSpec((1, W_B), lambda i: (0, 0))
    return pl.pallas_call(
        functools.partial(_pool_kernel, lay=lay),
        grid=(lay.n_tiles,),
        in_specs=[pl.BlockSpec((ROW_TILE, W_B), lambda i: (i, COL_B)), prev, nxt,
                  pl.BlockSpec(pool_w.shape, lambda i: (0, 0, 0)), vec, vec],
        out_specs=pl.BlockSpec((ROW_TILE, W_B), lambda i: (i, 0)),
        out_shape=jax.ShapeDtypeStruct((n, W_B), jnp.bfloat16),
        name="pool_group",
        scratch_shapes=[pltpu.VMEM((ROW_TILE + 2 * POOL_HALO, W_B), jnp.float32)],
        compiler_params=_cparams(),
    )(f, f, f, pool_w.astype(jnp.bfloat16), pool_b.reshape(1, W_B), pool_scale.reshape(1, W_B))


def _mm4_kernel(y0, y1, y2, y3, w_ref, o_ref):
    wk = w_ref.shape[0] // 4
    acc = jnp.dot(y0[...], w_ref[0:wk, :], preferred_element_type=jnp.float32)
    for k, y in enumerate((y1, y2, y3), start=1):
        acc = acc + jnp.dot(y[...], w_ref[k * wk:(k + 1) * wk, :], preferred_element_type=jnp.float32)
    o_ref[...] = acc


def _matmul4(ys, w, tm, tn):
    m = ys[0].shape[0]
    kdim, n = w.shape
    yspec = pl.BlockSpec((tm, kdim // 4), lambda i, j: (i, 0))
    return pl.pallas_call(
        _mm4_kernel,
        grid=(m // tm, n // tn),
        in_specs=[yspec] * 4 + [pl.BlockSpec((kdim, tn), lambda i, j: (0, j))],
        out_specs=pl.BlockSpec((tm, tn), lambda i, j: (i, j)),
        out_shape=jax.ShapeDtypeStruct((m, n), jnp.float32),
        name="dense_mm4",
        compiler_params=pltpu.CompilerParams(dimension_semantics=("parallel", "parallel"),
                                             vmem_limit_bytes=VMEM_LIMIT_BYTES),
    )(*ys, w)


def _head_ones():
    i = np.arange(LANES)
    return jnp.asarray((i[:, None] // HEAD_A) == (i[None, :] // HEAD_A), jnp.bfloat16)


def _head_sum(x, ones):
    hi = x.astype(jnp.bfloat16)
    lo = (x - hi.astype(jnp.float32)).astype(jnp.bfloat16)
    outs = []
    for s in range(x.shape[1] // LANES):
        cols = slice(s * LANES, (s + 1) * LANES)
        outs.append(jnp.dot(hi[:, cols], ones, preferred_element_type=jnp.float32)
                    + jnp.dot(lo[:, cols], ones, preferred_element_type=jnp.float32))
    return jnp.concatenate(outs, axis=1)


def _dot_split(a, w_hi, w_lo):
    a_hi = a.astype(jnp.bfloat16)
    a_lo = (a - a_hi.astype(jnp.float32)).astype(jnp.bfloat16)
    return (jnp.dot(a_hi, w_hi, preferred_element_type=jnp.float32)
            + jnp.dot(a_lo, w_hi, preferred_element_type=jnp.float32)
            + jnp.dot(a_hi, w_lo, preferred_element_type=jnp.float32))


def _token_shift(x, prev_row, next_row, mu_p, mu_n):
    rows = x.shape[0]
    row = lax.broadcasted_iota(jnp.int32, x.shape, 0)
    xp = jnp.where(row == 0, prev_row, pltpu.roll(x, 1, axis=0))
    xn = jnp.where(row == rows - 1, next_row, pltpu.roll(x, rows - 1, axis=0))
    return x + mu_p * (xp - x) + mu_n * (xn - x)


PREP_OUT_NAMES = ('at0', 'bt0', 'kt0', 'rt0', 'gam0', 'at1', 'bt1', 'kt1', 'rt1', 'gam1', 'v', 'g', 'bonus')


def _block_cumprod(w, pos, reverse):
    rows = w.shape[0]
    x = w
    s = 1
    while s < SCAN_SUB:
        if reverse:
            x = x * jnp.where(pos < SCAN_SUB - s, pltpu.roll(x, rows - s, axis=0), 1.0)
        else:
            x = x * jnp.where(pos >= s, pltpu.roll(x, s, axis=0), 1.0)
        s *= 2
    return x


def _block_prev(x, pos, reverse):
    rows = x.shape[0]
    if reverse:
        return jnp.where(pos == SCAN_SUB - 1, 1.0, pltpu.roll(x, rows - 1, axis=0))
    return jnp.where(pos == 0, 1.0, pltpu.roll(x, 1, axis=0))


def _rwkv_prep_kernel(x_ref, xp_ref, xn_ref, l_ref, lp_ref, ln_ref, mu_ref, mul_ref, whi_ref, wlo_ref,
                      vec_ref, ones_ref, *out_refs, lay):
    _, has_prev, has_next = _seq_edges(lay)
    edge = lambda ref, r, ok: jnp.where(ok, ref[r:r + 1, :], 0.0)
    x = _token_shift(x_ref[...], edge(xp_ref, SUBLANES - 1, has_prev), edge(xn_ref, 0, has_next),
                     mu_ref[0:1, :], mu_ref[1:2, :])
    lo = _token_shift(l_ref[...], edge(lp_ref, SUBLANES - 1, has_prev), edge(ln_ref, 0, has_next),
                      mul_ref[0:1, :], mul_ref[1:2, :])
    r = x[:, 0:W_A]
    k = x[:, W_A:2 * W_A]
    v = x[:, 2 * W_A:3 * W_A]
    col = lax.broadcasted_iota(jnp.int32, lo.shape, 1)
    act = jnp.where(col < D_DECAY_LORA, jnp.tanh(lo),
                    jnp.where(col < D_DECAY_LORA + D_AAA_LORA, lo, jax.nn.sigmoid(lo)))
    ones = ones_ref[...]
    pos = lax.broadcasted_iota(jnp.int32, r.shape, 0) % SCAN_SUB
    kk = k * vec_ref[0:1, :]
    kk = kk * lax.rsqrt(_head_sum(kk * kk, ones) + 1e-12)
    outs = {'v': v, 'g': _dot_split(act, whi_ref[4], wlo_ref[4])}
    bonus = None
    for d in range(2):
        w = jnp.exp(-DECAY_SCALE * jax.nn.sigmoid(vec_ref[3 + d:4 + d, :] + _dot_split(act, whi_ref[d], wlo_ref[d])))
        a = jax.nn.sigmoid(vec_ref[5 + d:6 + d, :] + _dot_split(act, whi_ref[2 + d], wlo_ref[2 + d]))
        kd = k * (1 + (a - 1) * vec_ref[1:2, :])
        term = _head_sum(r * kd * vec_ref[2:3, :], ones) * v
        bonus = term if bonus is None else bonus + term
        gam = _block_cumprod(w, pos, reverse=(d == 1))
        inv = 1.0 / gam
        outs['at%d' % d] = -kk * _block_prev(gam, pos, reverse=(d == 1))
        outs['bt%d' % d] = kk * a * inv
        outs['kt%d' % d] = kd * inv
        outs['rt%d' % d] = r * gam
        outs['gam%d' % d] = gam
    outs['bonus'] = bonus
    for ref, name in zip(out_refs, PREP_OUT_NAMES):
        ref[...] = outs[name]


def _rwkv_prep(x_rkv, x_lora, lora_col, P, lay):
    n = x_rkv.shape[0]
    o = 3 * W_A
    pr, nr = _halo_specs(n, SUBLANES, o, 0)
    plr, nlr = _halo_specs(n, SUBLANES, LORA_PAD, lora_col)
    full = lambda a: pl.BlockSpec(a.shape, lambda i: (0,) * a.ndim)
    mu = jnp.stack([P['mu_prev'][0:o], P['mu_next'][0:o]])
    pad = LORA_PAD - LORA_W
    mul = jnp.stack([jnp.pad(P['mu_prev'][o:], (0, pad)), jnp.pad(P['mu_next'][o:], (0, pad))])
    z = lambda rows: jnp.zeros((rows, W_A), jnp.float32)
    wcat = jnp.stack([
        jnp.concatenate([P['w_up'][0], z(LORA_PAD - D_DECAY_LORA)]),
        jnp.concatenate([P['w_up'][1], z(LORA_PAD - D_DECAY_LORA)]),
        jnp.concatenate([z(D_DECAY_LORA), P['a_up'][0], z(LORA_PAD - D_DECAY_LORA - D_AAA_LORA)]),
        jnp.concatenate([z(D_DECAY_LORA), P['a_up'][1], z(LORA_PAD - D_DECAY_LORA - D_AAA_LORA)]),
        jnp.concatenate([z(D_DECAY_LORA + D_AAA_LORA), P['g_up'], z(pad)]),
    ])
    w_hi = wcat.astype(jnp.bfloat16)
    w_lo = (wcat - w_hi.astype(jnp.float32)).astype(jnp.bfloat16)
    vec = jnp.stack([P['k_k'], P['k_a'], P['r_k'].reshape(W_A), P['w0'][0], P['w0'][1], P['a0'][0], P['a0'][1],
                     jnp.zeros((W_A,), jnp.float32)])
    ones = _head_ones()
    row_out = pl.BlockSpec((ROW_TILE, W_A), lambda i: (i, 0))
    outs = pl.pallas_call(
        functools.partial(_rwkv_prep_kernel, lay=lay),
        grid=(lay.n_tiles,),
        in_specs=[pl.BlockSpec((ROW_TILE, o), lambda i: (i, 0)), pr, nr,
                  pl.BlockSpec((ROW_TILE, LORA_PAD), lambda i: (i, lora_col)), plr, nlr,
                  full(mu), full(mul), full(w_hi), full(w_lo), full(vec), full(ones)],
        out_specs=[row_out] * N_PREP_OUT,
        out_shape=[jax.ShapeDtypeStruct((n, W_A), jnp.float32)] * N_PREP_OUT,
        name="rwkv_prep",
        compiler_params=_cparams(),
    )(x_rkv, x_rkv, x_rkv, x_lora, x_lora, x_lora, mu, mul, w_hi, w_lo, vec, ones)
    return dict(zip(PREP_OUT_NAMES, outs))


def _rwkv_post_kernel(yfc_ref, ybc_ref, yfl_ref, ybl_ref, bonus_ref, g_ref, vec_ref, ones_ref, o_ref, *, lay):
    ones = ones_ref[...]
    is_ctx = pl.program_id(0) < lay.ctx_tiles
    y = jnp.where(is_ctx, yfc_ref[...] + ybc_ref[...], yfl_ref[...] + ybl_ref[...])
    yc = y - _head_sum(y, ones) * (1.0 / HEAD_A)
    var = _head_sum(yc * yc, ones) * (1.0 / HEAD_A)
    y = yc * lax.rsqrt(var + GN_EPS)
    y = y * vec_ref[0:1, :] + vec_ref[1:2, :] + bonus_ref[...]
    o_ref[...] = (y * g_ref[...]).astype(o_ref.dtype)


def _rwkv_post(y_ctx, y_lat, bonus, g, ln_g, ln_b, lay):
    n = bonus.shape[0]
    row = pl.BlockSpec((ROW_TILE, W_A), lambda i: (i, 0))
    ctx_row = pl.BlockSpec((ROW_TILE, W_A), lambda i: (jnp.minimum(i, lay.ctx_tiles - 1), 0))
    lat_row = pl.BlockSpec((ROW_TILE, W_A), lambda i: (jnp.maximum(i - lay.ctx_tiles, 0), 0))
    vec = jnp.stack([ln_g, ln_b] + [jnp.zeros_like(ln_g)] * 6)
    ones = _head_ones()
    full = lambda a: pl.BlockSpec(a.shape, lambda i: (0,) * a.ndim)
    return pl.pallas_call(
        functools.partial(_rwkv_post_kernel, lay=lay),
        grid=(lay.n_tiles,),
        in_specs=[ctx_row, ctx_row, lat_row, lat_row, row, row, full(vec), full(ones)],
        out_specs=row,
        out_shape=jax.ShapeDtypeStruct((n, W_A), jnp.bfloat16),
        name="rwkv_post",
        compiler_params=_cparams(),
    )(*y_ctx, *y_lat, bonus, g, vec, ones)


def _grid_transpose(z, rows, cols):
    b, t, ch = z.shape
    return z.reshape(b, rows, cols, ch).transpose(0, 2, 1, 3).reshape(b, t, ch)


def _rwkv_group(f, P, lay, dims, s_ctx0, s_lat0, transposed):
    bc, tc, bl, tl = dims
    nc = bc * tc
    rows = tl // GRID_W
    if transposed:
        def permuted(cols):
            lat = _grid_transpose(f[nc:, cols].reshape(bl, tl, -1), rows, GRID_W)
            return jnp.concatenate([f[:nc, cols], lat.reshape(bl * tl, -1)], axis=0)
        x_rkv = permuted(slice(0, 3 * W_A))
        x_lora = permuted(slice(LORA_COL * W_A, LORA_COL * W_A + LORA_PAD))
        lora_col = 0
    else:
        x_rkv, x_lora, lora_col = f, f, LORA_COL * W_A // LORA_PAD
    ops = _rwkv_prep(x_rkv, x_lora, lora_col, P, lay)
    yf_c, yb_c, s_fin = _rwkv_scan(ops, s_ctx0, 0, bc, tc)
    yf_l, yb_l, _ = _rwkv_scan(ops, s_lat0, nc, bl, tl)
    out = _rwkv_post((yf_c, yb_c), (yf_l, yb_l), ops['bonus'], ops['g'], P['rwkv_ln_g'], P['rwkv_ln_b'], lay)
    if transposed:
        lat = _grid_transpose(out[nc:].reshape(bl, tl, W_A), GRID_W, rows)
        out = jnp.concatenate([out[:nc], lat.reshape(bl * tl, W_A)], axis=0)
    return out, s_fin


def _modulation_kernel(c_ref, w_ref, b_ref, o_ref):
    c = c_ref[...]
    w = w_ref[...]
    w_hi = w.astype(jnp.bfloat16)
    w_lo = (w - w_hi.astype(jnp.float32)).astype(jnp.bfloat16)
    o_ref[...] = _dot_split(c * jax.nn.sigmoid(c), w_hi, w_lo) + b_ref[...]


def _modulation(cvec, w_ada, b_ada):
    n_c, d = cvec.shape
    n_out = w_ada.shape[1]
    c_pad = jnp.pad(cvec, ((0, SUBLANES - n_c), (0, 0)))
    m = pl.pallas_call(
        _modulation_kernel,
        grid=(n_out // ADA_TN,),
        in_specs=[pl.BlockSpec((SUBLANES, d), lambda j: (0, 0)),
                  pl.BlockSpec((d, ADA_TN), lambda j: (0, j)),
                  pl.BlockSpec((1, ADA_TN), lambda j: (0, j))],
        out_specs=pl.BlockSpec((SUBLANES, ADA_TN), lambda j: (0, j)),
        out_shape=jax.ShapeDtypeStruct((SUBLANES, n_out), jnp.float32),
        name="modulation",
        compiler_params=_cparams(),
    )(c_pad, w_ada, b_ada.reshape(1, n_out))
    return m[:n_c].reshape(n_c, N_MOD, D_MODEL)


def _pack_w_in(w_in):
    o = 3 * W_A
    parts = [w_in[:, 0:o], w_in[:, COLS_A:], w_in[:, o:COLS_A]]
    parts = [p.astype(jnp.bfloat16) for p in parts]
    parts.append(jnp.zeros((w_in.shape[0], IN_PAD - IN_COLS), jnp.bfloat16))
    return jnp.concatenate(parts, axis=1)


def kernel(x_prompt, x_sample, c, state_rwkv, c_ctx, w_ada, b_ada, w_in, mu_prev, mu_next, rwkv_g_up, rwkv_w0, rwkv_w_up, rwkv_a0, rwkv_a_up, rwkv_k_k, rwkv_k_a, rwkv_r_k, rwkv_ln_g, rwkv_ln_b, pool_w, pool_b, pool_scale, conf_conv_w, conf_conv_b, conf_ln_g, conf_ln_b, gmlp_ln_g, gmlp_ln_b, gmlp_ws, gmlp_bs, w_out, ln1_g, ln1_b, ffn_w_up, ffn_conv_w, ffn_conv_b, ffn_w_down, ln2_g, ln2_b):
    bc, tc, d = x_prompt.shape
    bl, tl, _ = x_sample.shape
    nc = bc * tc
    lay = TokenLayout(bc, tc, bl, tl)
    s_ctx0 = jnp.zeros((bc, 2, H_A, HEAD_A, HEAD_A), jnp.float32)
    cvec = jnp.concatenate([c_ctx[None, :], c], axis=0)
    mods = [_modulation(cvec, w_ada[l], b_ada[l]) for l in range(DEPTH)]
    x = jnp.concatenate([x_prompt.reshape(nc, d), x_sample.reshape(bl * tl, d)], axis=0)
    h = _modulate(x, jnp.stack([mods[0][:, 1], mods[0][:, 0]], axis=1), lay)
    new_states = []
    for l in range(DEPTH):
        P = {
            'mu_prev': mu_prev[l], 'mu_next': mu_next[l], 'g_up': rwkv_g_up[l],
            'w0': rwkv_w0[l], 'w_up': rwkv_w_up[l], 'a0': rwkv_a0[l], 'a_up': rwkv_a_up[l],
            'k_k': rwkv_k_k[l], 'k_a': rwkv_k_a[l], 'r_k': rwkv_r_k[l],
            'rwkv_ln_g': rwkv_ln_g[l], 'rwkv_ln_b': rwkv_ln_b[l],
        }
        w_in_p = _pack_w_in(w_in[l])
        w_out_b = w_out[l].astype(jnp.bfloat16)
        w_up_b = ffn_w_up[l].astype(jnp.bfloat16)
        w_dn_b = jnp.concatenate([ffn_w_down[l].astype(jnp.bfloat16),
                                  jnp.zeros((FF_PAD - D_FF, d), jnp.bfloat16)], axis=0)

        m = mods[l]

        f = _matmul(h, w_in_p, jnp.float32, 1024, 512, D_MODEL)
        y_a, s_fin = _rwkv_group(f, P, lay, (bc, tc, bl, tl), s_ctx0, state_rwkv[:, l], l % 2 == 1)
        new_states.append(s_fin)
        y_b = _pool_group(f, pool_w[l], pool_b[l], pool_scale[l], lay)
        y_c = _conformer_group(f, conf_conv_w[l], conf_conv_b[l], conf_ln_g[l], conf_ln_b[l], lay)
        y_d = _gmlp_group(f, gmlp_ln_g[l], gmlp_ln_b[l], gmlp_ws[l], gmlp_bs[l], lay)
        y = _matmul4([y_a, y_b, y_c, y_d], w_out_b, 1024, 1024)
        x, h = _ln_residual(y, x, jnp.stack([m[:, 2], m[:, 4], m[:, 3]], axis=1), ln1_g[l], ln1_b[l], lay, False)

        z = _matmul(h, w_up_b, jnp.bfloat16, 1024, 512, D_MODEL)
        g = _ffn_gate(z, ffn_conv_w[l], ffn_conv_b[l], lay)
        y = _matmul(g, w_dn_b, jnp.float32, 1024, 1024, FF_PAD // 4)
        if l + 1 < DEPTH:
            mn = mods[l + 1]
            x, h = _ln_residual(y, x, jnp.stack([m[:, 5], mn[:, 1], mn[:, 0]], axis=1), ln2_g[l], ln2_b[l], lay, False)
        else:
            x_ctx, x_lat = _ln_residual(y, x, jnp.stack([m[:, 5], m[:, 5], m[:, 5]], axis=1), ln2_g[l], ln2_b[l],
                                        lay, True)

    new_state_rwkv = jnp.stack(new_states, axis=1).astype(x_prompt.dtype)
    return (x_ctx.reshape(bc, tc, d), x_lat.reshape(bl, tl, d), new_state_rwkv)
```

```python
import functools
import math

import numpy as np
import jax
import jax.numpy as jnp
from jax import lax
from jax.experimental import pallas as pl
from jax.experimental.pallas import tpu as pltpu

D_MODEL = 4096
DEPTH = 2
GRID_W = 64
W_A = D_MODEL // 4
W_B = D_MODEL // 4
W_C = D_MODEL // 4
W_D = D_MODEL - W_A - W_B - W_C
HEAD_A = 64
H_A = W_A // HEAD_A
D_DECAY_LORA = max(32, int(round(W_A ** 0.5 * 1.8 / 32)) * 32)
D_AAA_LORA = max(32, int(round(W_A ** 0.5 * 1.8 / 32)) * 32)
D_GATE_LORA = max(32, int(round(W_A ** 0.8 * 0.6 / 32)) * 32)
COLS_A = 3 * W_A + D_DECAY_LORA + D_AAA_LORA + D_GATE_LORA
DECAY_SCALE = math.exp(-0.5)
GN_EPS = 64e-5
POOL_WINDOWS = (2, 4, 8, 16)
POOL_GROUP = W_B // len(POOL_WINDOWS)
CONV_K = 31
CHUNK = 128
H_D = 8
HEAD_D = W_D // H_D
IN_COLS = COLS_A + W_B + 2 * W_C + 2 * W_D
D_FF = ((8 * D_MODEL // 3 + 255) // 256) * 256
FFN_K = 3
ALPHA = (2.0 * DEPTH) ** 0.25
LN_EPS = 1e-5
N_MOD = 6
IN_PAD = ((IN_COLS + 511) // 512) * 512
FF_PAD = ((D_FF + 1023) // 1024) * 1024

LANES = 128
SUBLANES = 8
VMEM_LIMIT_BYTES = 56 * 1024 * 1024

SCAN_TB = 128
SCAN_SUB = 32
SCAN_PARTS = 3
N_KEYVEC = 5
N_STEPVEC = 4
SCAN_TRIP = 4

ROW_TILE = 256
FF_CHUNK = 256
ADA_TN = 1024
CONV_HALO = 16
POOL_HALO = 8
CONV_ROWS = 32
COL_B, COL_CA, COL_CG, COL_DU, COL_DV, LORA_COL = 3, 4, 5, 6, 7, 8
LORA_W = D_DECAY_LORA + D_AAA_LORA + D_GATE_LORA
LORA_PAD = IN_PAD - LORA_COL * W_A
N_PREP_OUT = 13


def _mm_kernel(x_ref, w_ref, o_ref, acc_ref, *, nk):
    k = pl.program_id(2)
    part = jnp.dot(x_ref[...], w_ref[...], preferred_element_type=jnp.float32)
    if nk == 1:
        o_ref[...] = part.astype(o_ref.dtype)
    else:
        @pl.when(k == 0)
        def _():
            acc_ref[...] = part

        @pl.when(k > 0)
        def _():
            acc_ref[...] += part

        @pl.when(k == nk - 1)
        def _():
            o_ref[...] = acc_ref[...].astype(o_ref.dtype)


def _matmul(x, w_all, layer, out_dtype, tm, tn, tk):
    m, kdim = x.shape
    _, _, n = w_all.shape
    assert m % tm == 0 and n % tn == 0 and kdim % tk == 0
    nk = kdim // tk
    return pl.pallas_call(
        functools.partial(_mm_kernel, nk=nk),
        grid=(m // tm, n // tn, nk),
        in_specs=[
            pl.BlockSpec((tm, tk), lambda i, j, k: (i, k)),
            pl.BlockSpec((None, tk, tn), lambda i, j, k: (layer, k, j)),
        ],
        out_specs=pl.BlockSpec((tm, tn), lambda i, j, k: (i, j)),
        out_shape=jax.ShapeDtypeStruct((m, n), out_dtype),
        name="dense_mm",
        scratch_shapes=[pltpu.VMEM((tm, tn), jnp.float32)],
        compiler_params=pltpu.CompilerParams(
            dimension_semantics=("parallel", "parallel", "arbitrary"),
            vmem_limit_bytes=VMEM_LIMIT_BYTES,
        ),
    )(x, w_all)


def _sel_table():
    c = np.arange(2 * LANES)[:, None]
    n = np.arange(2 * LANES)[None, :]
    e = np.zeros((SCAN_SUB // 2, 2 * LANES, 2 * LANES), np.float32)
    for p in range(SCAN_SUB // 2):
        t = 2 * p + n // LANES
        e[p] = (((c % LANES) < SCAN_PARTS * SCAN_SUB) & ((c % SCAN_SUB) == t)
                & ((c // LANES) == ((n % LANES) // HEAD_A)))
    return jnp.asarray(e, jnp.bfloat16)


def _split3_rows(x):
    hi = x.astype(jnp.bfloat16).astype(jnp.float32)
    r1 = x - hi
    mid = r1.astype(jnp.bfloat16).astype(jnp.float32)
    lo = r1 - mid
    return jnp.concatenate([hi, mid, lo, jnp.zeros_like(x)], axis=0)


def _scan_kernel(*refs, nblk):
    key_refs = (refs[0:5], refs[6:11])
    v_refs = (refs[5], refs[11])
    e_ref, s0_ref = refs[12], refs[13]
    y_refs = (refs[14], refs[15])
    sfin_ref, st_ref, lhs_ref = refs[16], refs[17], refs[18]
    tb = pl.program_id(2)

    @pl.when(tb == 0)
    def _():
        st_ref[...] = s0_ref[0, :, 0]

    lane = lax.broadcasted_iota(jnp.int32, (SCAN_SUB, LANES), 1)
    first_head = lane < HEAD_A

    half = N_KEYVEC * HEAD_A
    npairs = SCAN_SUB // 2

    def build_lhs(slot, offs):
        for c in range(2):
            xs = [ref[pl.ds(offs[c], SCAN_SUB), :] for ref in key_refs[c]]
            for pi, (i1, i2) in enumerate(((0, 1), (2, 3), (4, None))):
                x1 = xs[i1]
                xr1 = pltpu.roll(x1, HEAD_A, axis=1)
                if i2 is None:
                    comb_a, comb_b = x1, xr1
                else:
                    x2 = xs[i2]
                    xr2 = pltpu.roll(x2, HEAD_A, axis=1)
                    comb_a = jnp.where(first_head, x1, xr2)
                    comb_b = jnp.where(first_head, xr1, x2)
                mt_a = _split3_rows(comb_a).T.astype(jnp.bfloat16)
                mt_b = _split3_rows(comb_b).T.astype(jnp.bfloat16)
                r0 = (c * N_KEYVEC + 2 * pi) * HEAD_A
                lhs_ref[slot, r0:r0 + HEAD_A, 0:LANES] = mt_a[0:HEAD_A]
                lhs_ref[slot, r0:r0 + HEAD_A, LANES:2 * LANES] = mt_b[0:HEAD_A]
                if i2 is not None:
                    lhs_ref[slot, r0 + HEAD_A:r0 + 2 * HEAD_A, 0:LANES] = mt_a[HEAD_A:2 * HEAD_A]
                    lhs_ref[slot, r0 + HEAD_A:r0 + 2 * HEAD_A, LANES:2 * LANES] = mt_b[HEAD_A:2 * HEAD_A]

    def sub_offsets(q):
        return (pl.multiple_of(q * SCAN_SUB, SCAN_SUB),
                pl.multiple_of(SCAN_TB - (q + 1) * SCAN_SUB, SCAN_SUB))

    def run_sub(slot, offs, hs):
        vs = [v_refs[c][pl.ds(offs[c], SCAN_SUB), :] for c in range(2)]
        ys = ([], [])
        gs = list(hs)
        step_rows = N_STEPVEC * HEAD_A
        for p in range(npairs):
            tiles = (jnp.dot(lhs_ref[slot, 0:step_rows, :], e_ref[p], preferred_element_type=jnp.float32),
                     jnp.dot(lhs_ref[slot, half:half + step_rows, :], e_ref[npairs - 1 - p],
                             preferred_element_type=jnp.float32))
            for s in range(2):
                u = 2 * p + s
                for c in range(2):
                    col = s if c == 0 else 1 - s
                    row = u if c == 0 else SCAN_SUB - 1 - u
                    a_t, b_t, k_t, r_t = (
                        tiles[c][k * HEAD_A:(k + 1) * HEAD_A, col * LANES:(col + 1) * LANES]
                        for k in range(N_STEPVEC))
                    g = gs[c]
                    sa = jnp.sum(g * a_t, axis=0, keepdims=True)
                    g = g + sa * b_t + vs[c][row:row + 1, :] * k_t
                    ys[c].append(jnp.sum(g * r_t, axis=0, keepdims=True))
                    gs[c] = g
        y_refs[0][pl.ds(offs[0], SCAN_SUB), :] = jnp.concatenate(ys[0], axis=0)
        y_refs[1][pl.ds(offs[1], SCAN_SUB), :] = jnp.concatenate(ys[1][::-1], axis=0)
        gam_f = jnp.dot(lhs_ref[slot, step_rows:half, :], e_ref[npairs - 1],
                        preferred_element_type=jnp.float32)[:, LANES:2 * LANES]
        gam_b = jnp.dot(lhs_ref[slot, half + step_rows:2 * half, :], e_ref[0],
                        preferred_element_type=jnp.float32)[:, 0:LANES]
        return [gs[0] * gam_f, gs[1] * gam_b]

    def trip(g, carry):
        hs = list(carry)
        offs = [sub_offsets(g * SCAN_TRIP + k) for k in range(SCAN_TRIP)]
        for k in range(SCAN_TRIP):
            build_lhs(k, offs[k])
        for k in range(SCAN_TRIP):
            hs = run_sub(k, offs[k], hs)
        return tuple(hs)

    hf, hb = lax.fori_loop(0, SCAN_TB // (SCAN_SUB * SCAN_TRIP), trip, (st_ref[0], st_ref[1]))
    st_ref[0] = hf
    st_ref[1] = hb

    @pl.when(tb == nblk - 1)
    def _():
        sfin_ref[0, 0, 0] = hf
        sfin_ref[0, 1, 0] = hb


def _rwkv_scan(ops, s0, row0, bsz, t_len):
    npair = H_A // 2
    nblk = t_len // SCAN_TB
    blk0 = row0 // SCAN_TB
    s0p = s0.reshape(bsz, 2, npair, 2, HEAD_A, HEAD_A).transpose(0, 1, 2, 5, 3, 4)
    s0p = s0p.reshape(bsz, 2, npair, HEAD_A, 2 * HEAD_A)
    ins = []
    for d in range(2):
        ins += [ops[n + str(d)] for n in ('at', 'bt', 'kt', 'rt', 'gam')] + [ops['v']]
    blk = (SCAN_TB, LANES)
    fwd_in = pl.BlockSpec(blk, lambda b, p, t: (blk0 + b * nblk + t, p))
    bwd_in = pl.BlockSpec(blk, lambda b, p, t: (blk0 + b * nblk + nblk - 1 - t, p))
    fwd_out = pl.BlockSpec(blk, lambda b, p, t: (b * nblk + t, p))
    bwd_out = pl.BlockSpec(blk, lambda b, p, t: (b * nblk + nblk - 1 - t, p))
    st_spec = pl.BlockSpec((1, 2, 1, HEAD_A, LANES), lambda b, p, t: (b, 0, p, 0, 0))
    e_tab = _sel_table()
    in_specs = ([fwd_in] * 6 + [bwd_in] * 6
                + [pl.BlockSpec(e_tab.shape, lambda b, p, t: (0, 0, 0)), st_spec])
    out_shape = [
        jax.ShapeDtypeStruct((bsz * t_len, W_A), jnp.float32),
        jax.ShapeDtypeStruct((bsz * t_len, W_A), jnp.float32),
        jax.ShapeDtypeStruct((bsz, 2, npair, HEAD_A, LANES), jnp.float32),
    ]
    yf, yb, sfin = pl.pallas_call(
        functools.partial(_scan_kernel, nblk=nblk),
        grid=(bsz, npair, nblk),
        in_specs=in_specs,
        out_specs=[fwd_out, bwd_out, st_spec],
        out_shape=out_shape,
        name="rwkv_scan",
        scratch_shapes=[pltpu.VMEM((2, HEAD_A, LANES), jnp.float32),
                        pltpu.VMEM((SCAN_TRIP, 2 * N_KEYVEC * HEAD_A, 2 * LANES), jnp.bfloat16)],
        compiler_params=pltpu.CompilerParams(
            dimension_semantics=("parallel", "parallel", "arbitrary"),
            vmem_limit_bytes=VMEM_LIMIT_BYTES,
        ),
    )(*ins, e_tab, s0p)
    sfin = sfin.reshape(bsz, 2, npair, HEAD_A, 2, HEAD_A).transpose(0, 1, 2, 4, 5, 3)
    return yf, yb, sfin.reshape(bsz, 2, H_A, HEAD_A, HEAD_A)


class TokenLayout:
    def __init__(self, bc, tc, bl, tl):
        assert tc % ROW_TILE == 0 and tl % ROW_TILE == 0
        self.n_rows = bc * tc + bl * tl
        self.n_tiles = self.n_rows // ROW_TILE
        self.ctx_tiles = bc * tc // ROW_TILE
        self.ctx_seq_tiles = tc // ROW_TILE
        self.lat_seq_tiles = tl // ROW_TILE

    def group(self, i):
        return jnp.where(i < self.ctx_tiles, 0, 1 + (i - self.ctx_tiles) // self.lat_seq_tiles)

    def pos_in_seq(self, i):
        return jnp.where(i < self.ctx_tiles, i % self.ctx_seq_tiles, (i - self.ctx_tiles) % self.lat_seq_tiles)

    def seq_tiles(self, i):
        return jnp.where(i < self.ctx_tiles, self.ctx_seq_tiles, self.lat_seq_tiles)


def _modulate_kernel(x_ref, m_ref, h_ref):
    x = x_ref[...]
    h_ref[...] = (x * (1 + m_ref[0, 0:1, :]) + m_ref[0, 1:2, :]).astype(h_ref.dtype)


def _modulate(x, m2, lay):
    n, d = x.shape
    return pl.pallas_call(
        _modulate_kernel,
        grid=(lay.n_tiles,),
        in_specs=[pl.BlockSpec((ROW_TILE, d), lambda i: (i, 0)),
                  pl.BlockSpec((1, 2, d), lambda i: (lay.group(i), 0, 0))],
        out_specs=pl.BlockSpec((ROW_TILE, d), lambda i: (i, 0)),
        out_shape=jax.ShapeDtypeStruct((n, d), jnp.bfloat16),
        name="modulate",
        compiler_params=pltpu.CompilerParams(dimension_semantics=("parallel",),
                                             vmem_limit_bytes=VMEM_LIMIT_BYTES),
    )(x, m2)


def _ln_residual_kernel(y_ref, x_ref, m_ref, g_ref, b_ref, *o_refs, lay, split):
    v = ALPHA * x_ref[...] + m_ref[0, 0:1, :] * y_ref[...]
    vc = v - jnp.mean(v, -1, keepdims=True)
    var = jnp.mean(vc * vc, -1, keepdims=True)
    xn = (vc * lax.rsqrt(var + LN_EPS)) * g_ref[...] + b_ref[...]
    if split:
        is_ctx = pl.program_id(0) < lay.ctx_tiles

        @pl.when(is_ctx)
        def _():
            o_refs[0][...] = xn

        @pl.when(jnp.logical_not(is_ctx))
        def _():
            o_refs[1][...] = xn
    else:
        o_refs[0][...] = xn
        o_refs[1][...] = (xn * (1 + m_ref[0, 1:2, :]) + m_ref[0, 2:3, :]).astype(jnp.bfloat16)


def _ln_residual(y, x, m3, g, b, lay, split):
    n, d = x.shape
    row = pl.BlockSpec((ROW_TILE, d), lambda i: (i, 0))
    vec = pl.BlockSpec((1, d), lambda i: (0, 0))
    if split:
        n_ctx = lay.ctx_tiles * ROW_TILE
        out_shape = [jax.ShapeDtypeStruct((n_ctx, d), jnp.float32),
                     jax.ShapeDtypeStruct((n - n_ctx, d), jnp.float32)]
        out_specs = [pl.BlockSpec((ROW_TILE, d), lambda i: (jnp.minimum(i, lay.ctx_tiles - 1), 0)),
                     pl.BlockSpec((ROW_TILE, d), lambda i: (jnp.maximum(i - lay.ctx_tiles, 0), 0))]
    else:
        out_shape = [jax.ShapeDtypeStruct((n, d), jnp.float32), jax.ShapeDtypeStruct((n, d), jnp.bfloat16)]
        out_specs = [row, row]
    return pl.pallas_call(
        functools.partial(_ln_residual_kernel, lay=lay, split=split),
        grid=(lay.n_tiles,),
        in_specs=[row, row, pl.BlockSpec((1, 3, d), lambda i: (lay.group(i), 0, 0)), vec, vec],
        out_specs=out_specs,
        out_shape=out_shape,
        name="ln_residual",
        compiler_params=pltpu.CompilerParams(dimension_semantics=("arbitrary",),
                                             vmem_limit_bytes=VMEM_LIMIT_BYTES),
    )(y, x, m3, g.reshape(1, d), b.reshape(1, d))


def _ffn_gate_kernel(zc_ref, zp_ref, zn_ref, w_ref, b_ref, o_ref, *, lay):
    i = pl.program_id(0)
    pos = lay.pos_in_seq(i)
    has_prev = pos > 0
    has_next = pos < lay.seq_tiles(i) - 1
    row = lax.broadcasted_iota(jnp.int32, (ROW_TILE, FF_CHUNK), 0)
    n_chunks = D_FF // FF_CHUNK

    def conv(col):
        sl = pl.ds(col, FF_CHUNK)
        z = zc_ref[:, sl].astype(jnp.float32)
        zprev_row = jnp.where(has_prev, zp_ref[SUBLANES - 1:SUBLANES, sl].astype(jnp.float32), 0.0)
        znext_row = jnp.where(has_next, zn_ref[0:1, sl].astype(jnp.float32), 0.0)
        zprev = jnp.where(row == 0, zprev_row, pltpu.roll(z, 1, axis=0))
        znext = jnp.where(row == ROW_TILE - 1, znext_row, pltpu.roll(z, ROW_TILE - 1, axis=0))
        return zprev * w_ref[0:1, sl] + b_ref[0:1, sl] + z * w_ref[1:2, sl] + znext * w_ref[2:3, sl]

    def body(j, carry):
        ca = pl.multiple_of(j * FF_CHUNK, FF_CHUNK)
        cb = pl.multiple_of(D_FF + j * FF_CHUNK, FF_CHUNK)
        a = conv(ca)
        bv = conv(cb)
        o_ref[:, pl.ds(ca, FF_CHUNK)] = (a * jax.nn.sigmoid(a) * bv).astype(o_ref.dtype)
        return carry

    lax.fori_loop(0, n_chunks, body, 0)
    o_ref[:, D_FF:FF_PAD] = jnp.zeros((ROW_TILE, FF_PAD - D_FF), o_ref.dtype)


def _ffn_gate(z, conv_w, conv_b, lay):
    n, c2 = z.shape
    rt8 = ROW_TILE // SUBLANES
    last8 = n // SUBLANES - 1
    return pl.pallas_call(
        functools.partial(_ffn_gate_kernel, lay=lay),
        grid=(lay.n_tiles,),
        in_specs=[pl.BlockSpec((ROW_TILE, c2), lambda i: (i, 0)),
                  pl.BlockSpec((SUBLANES, c2), lambda i: (jnp.maximum(i * rt8 - 1, 0), 0)),
                  pl.BlockSpec((SUBLANES, c2), lambda i: (jnp.minimum((i + 1) * rt8, last8), 0)),
                  pl.BlockSpec((3, c2), lambda i: (0, 0)),
                  pl.BlockSpec((1, c2), lambda i: (0, 0))],
        out_specs=pl.BlockSpec((ROW_TILE, FF_PAD), lambda i: (i, 0)),
        out_shape=jax.ShapeDtypeStruct((n, FF_PAD), jnp.bfloat16),
        name="ffn_gate",
        compiler_params=pltpu.CompilerParams(dimension_semantics=("parallel",),
                                             vmem_limit_bytes=VMEM_LIMIT_BYTES),
    )(z, z, z, conv_w, conv_b.reshape(1, c2))


def _cparams():
    return pltpu.CompilerParams(dimension_semantics=("parallel",), vmem_limit_bytes=VMEM_LIMIT_BYTES)


def _halo_specs(n_rows, halo, width, col):
    per_tile = ROW_TILE // halo
    last = n_rows // halo - 1
    prev = pl.BlockSpec((halo, width), lambda i: (jnp.maximum(i * per_tile - 1, 0), col))
    nxt = pl.BlockSpec((halo, width), lambda i: (jnp.minimum((i + 1) * per_tile, last), col))
    return prev, nxt


def _seq_edges(lay):
    i = pl.program_id(0)
    pos = lay.pos_in_seq(i)
    return pos, pos > 0, pos < lay.seq_tiles(i) - 1


def _row_layer_norm(x, g, b):
    xc = x - jnp.mean(x, -1, keepdims=True)
    var = jnp.mean(xc * xc, -1, keepdims=True)
    return (xc * lax.rsqrt(var + LN_EPS)) * g + b


def _conformer_kernel(a_ref, g_ref, ap_ref, gp_ref, an_ref, gn_ref, w_ref, b_ref, lg_ref, lb_ref,
                      o_ref, hext_ref, y_ref, *, lay):
    _, has_prev, has_next = _seq_edges(lay)
    glu = lambda a, g: a * jax.nn.sigmoid(g)
    hext_ref[0:CONV_HALO] = jnp.where(has_prev, glu(ap_ref[...], gp_ref[...]), 0.0)
    hext_ref[CONV_HALO:CONV_HALO + ROW_TILE] = glu(a_ref[...], g_ref[...])
    hext_ref[CONV_HALO + ROW_TILE:] = jnp.where(has_next, glu(an_ref[...], gn_ref[...]), 0.0)
    first = CONV_HALO - CONV_K // 2
    for rc in range(ROW_TILE // CONV_ROWS):
        base = first + rc * CONV_ROWS
        acc = hext_ref[base:base + CONV_ROWS] * w_ref[0:1] + b_ref[...]
        for j in range(1, CONV_K):
            acc = acc + hext_ref[base + j:base + j + CONV_ROWS] * w_ref[j:j + 1]
        y_ref[rc * CONV_ROWS:(rc + 1) * CONV_ROWS] = acc
    y = _row_layer_norm(y_ref[...], lg_ref[...], lb_ref[...])
    o_ref[...] = (y * jax.nn.sigmoid(y)).astype(o_ref.dtype)


def _conformer_group(f, conv_w, conv_b, ln_g, ln_b, lay):
    n = f.shape[0]
    cur = lambda col: pl.BlockSpec((ROW_TILE, W_C), lambda i: (i, col))
    pa, na = _halo_specs(n, CONV_HALO, W_C, COL_CA)
    pg, ng = _halo_specs(n, CONV_HALO, W_C, COL_CG)
    vec = pl.BlockSpec((1, W_C), lambda i: (0, 0))
    return pl.pallas_call(
        functools.partial(_conformer_kernel, lay=lay),
        grid=(lay.n_tiles,),
        in_specs=[cur(COL_CA), cur(COL_CG), pa, pg, na, ng,
                  pl.BlockSpec((CONV_K, W_C), lambda i: (0, 0)), vec, vec, vec],
        out_specs=pl.BlockSpec((ROW_TILE, W_C), lambda i: (i, 0)),
        out_shape=jax.ShapeDtypeStruct((n, W_C), jnp.bfloat16),
        name="conformer_group",
        scratch_shapes=[pltpu.VMEM((ROW_TILE + 2 * CONV_HALO, W_C), jnp.float32),
                        pltpu.VMEM((ROW_TILE, W_C), jnp.float32)],
        compiler_params=_cparams(),
    )(f, f, f, f, f, f, conv_w, conv_b.reshape(1, W_C), ln_g.reshape(1, W_C), ln_b.reshape(1, W_C))


def _gmlp_kernel(u_ref, v_ref, lg_ref, lb_ref, ws_ref, bias_ref, o_ref):
    v = _row_layer_norm(jax.nn.gelu(v_ref[...]), lg_ref[...], lb_ref[...]).astype(jnp.bfloat16)
    for ch in range(ROW_TILE // CHUNK):
        rows = slice(ch * CHUNK, (ch + 1) * CHUNK)
        for h in range(H_D):
            cols = slice(h * HEAD_D, (h + 1) * HEAD_D)
            s = jnp.dot(ws_ref[h], v[rows, cols], preferred_element_type=jnp.float32) + bias_ref[:, cols]
            o_ref[rows, cols] = (jax.nn.gelu(u_ref[rows, cols]) * s).astype(o_ref.dtype)


def _gmlp_group(f, ln_g, ln_b, ws, bs, lay):
    n = f.shape[0]
    vec = pl.BlockSpec((1, W_D), lambda i: (0, 0))
    bias = jnp.repeat(bs.T, HEAD_D, axis=1)
    return pl.pallas_call(
        _gmlp_kernel,
        grid=(lay.n_tiles,),
        in_specs=[pl.BlockSpec((ROW_TILE, W_D), lambda i: (i, COL_DU)),
                  pl.BlockSpec((ROW_TILE, W_D), lambda i: (i, COL_DV)),
                  vec, vec,
                  pl.BlockSpec((H_D, CHUNK, CHUNK), lambda i: (0, 0, 0)),
                  pl.BlockSpec((CHUNK, W_D), lambda i: (0, 0))],
        out_specs=pl.BlockSpec((ROW_TILE, W_D), lambda i: (i, 0)),
        out_shape=jax.ShapeDtypeStruct((n, W_D), jnp.bfloat16),
        name="gmlp_group",
        compiler_params=_cparams(),
    )(f, f, ln_g.reshape(1, W_D), ln_b.reshape(1, W_D), ws.astype(jnp.bfloat16), bias)


def _pool_kernel(x_ref, xp_ref, xn_ref, w_ref, b_ref, sc_ref, o_ref, hext_ref, *, lay):
    pos, has_prev, has_next = _seq_edges(lay)
    hext_ref[0:POOL_HALO] = jnp.where(has_prev, xp_ref[...], 0.0)
    hext_ref[POOL_HALO:POOL_HALO + ROW_TILE] = x_ref[...]
    hext_ref[POOL_HALO + ROW_TILE:] = jnp.where(has_next, xn_ref[...], 0.0)
    t = pos * ROW_TILE + lax.broadcasted_iota(jnp.int32, (ROW_TILE, POOL_GROUP), 0)
    t_len = lay.seq_tiles(pl.program_id(0)) * ROW_TILE
    for gi, win in enumerate(POOL_WINDOWS):
        cols = slice(gi * POOL_GROUP, (gi + 1) * POOL_GROUP)
        half = win // 2
        ssum = hext_ref[POOL_HALO - half:POOL_HALO - half + ROW_TILE, cols]
        for dlt in range(-half + 1, half):
            ssum = ssum + hext_ref[POOL_HALO + dlt:POOL_HALO + dlt + ROW_TILE, cols]
        cnt = (jnp.minimum(t + half, t_len) - jnp.maximum(t - half, 0)).astype(jnp.float32)
        pooled = (ssum / cnt - x_ref[:, cols]).astype(jnp.bfloat16)
        y = jnp.dot(pooled, w_ref[gi], preferred_element_type=jnp.float32) + b_ref[:, cols]
        o_ref[:, cols] = (y * sc_ref[:, cols]).astype(o_ref.dtype)


def _pool_group(f, pool_w, pool_b, pool_scale, lay):
    n = f.shape[0]
    prev, nxt = _halo_specs(n, POOL_HALO, W_B, COL_B)
    vec = pl.BlockSpec((1, W_B), lambda i: (0, 0))
    return pl.pallas_call(
        functools.partial(_pool_kernel, lay=lay),
        grid=(lay.n_tiles,),
        in_specs=[pl.BlockSpec((ROW_TILE, W_B), lambda i: (i, COL_B)), prev, nxt,
                  pl.BlockSpec(pool_w.shape, lambda i: (0, 0, 0)), vec, vec],
        out_specs=pl.BlockSpec((ROW_TILE, W_B), lambda i: (i, 0)),
        out_shape=jax.ShapeDtypeStruct((n, W_B), jnp.bfloat16),
        name="pool_group",
        scratch_shapes=[pltpu.VMEM((ROW_TILE + 2 * POOL_HALO, W_B), jnp.float32)],
        compiler_params=_cparams(),
    )(f, f, f, pool_w.astype(jnp.bfloat16), pool_b.reshape(1, W_B), pool_scale.reshape(1, W_B))


def _mm4_kernel(y0, y1, y2, y3, w_ref, o_ref):
    wk = w_ref.shape[0] // 4
    acc = jnp.dot(y0[...], w_ref[0:wk, :], preferred_element_type=jnp.float32)
    for k, y in enumerate((y1, y2, y3), start=1):
        acc = acc + jnp.dot(y[...], w_ref[k * wk:(k + 1) * wk, :], preferred_element_type=jnp.float32)
    o_ref[...] = acc


def _matmul4(ys, w_all, layer, tm, tn):
    m = ys[0].shape[0]
    _, kdim, n = w_all.shape
    yspec = pl.BlockSpec((tm, kdim // 4), lambda i, j: (i, 0))
    return pl.pallas_call(
        _mm4_kernel,
        grid=(m // tm, n // tn),
        in_specs=[yspec] * 4 + [pl.BlockSpec((None, kdim, tn), lambda i, j: (layer, 0, j))],
        out_specs=pl.BlockSpec((tm, tn), lambda i, j: (i, j)),
        out_shape=jax.ShapeDtypeStruct((m, n), jnp.float32),
        name="dense_mm4",
        compiler_params=pltpu.CompilerParams(dimension_semantics=("parallel", "parallel"),
                                             vmem_limit_bytes=VMEM_LIMIT_BYTES),
    )(*ys, w_all)


def _head_ones():
    i = np.arange(LANES)
    return jnp.asarray((i[:, None] // HEAD_A) == (i[None, :] // HEAD_A), jnp.bfloat16)


def _head_sum(x, ones):
    hi = x.astype(jnp.bfloat16)
    lo = (x - hi.astype(jnp.float32)).astype(jnp.bfloat16)
    outs = []
    for s in range(x.shape[1] // LANES):
        cols = slice(s * LANES, (s + 1) * LANES)
        outs.append(jnp.dot(hi[:, cols], ones, preferred_element_type=jnp.float32)
                    + jnp.dot(lo[:, cols], ones, preferred_element_type=jnp.float32))
    return jnp.concatenate(outs, axis=1)


def _dot_split(a, w_hi, w_lo):
    a_hi = a.astype(jnp.bfloat16)
    a_lo = (a - a_hi.astype(jnp.float32)).astype(jnp.bfloat16)
    return (jnp.dot(a_hi, w_hi, preferred_element_type=jnp.float32)
            + jnp.dot(a_lo, w_hi, preferred_element_type=jnp.float32)
            + jnp.dot(a_hi, w_lo, preferred_element_type=jnp.float32))


def _token_shift(x, prev_row, next_row, mu_p, mu_n):
    rows = x.shape[0]
    row = lax.broadcasted_iota(jnp.int32, x.shape, 0)
    xp = jnp.where(row == 0, prev_row, pltpu.roll(x, 1, axis=0))
    xn = jnp.where(row == rows - 1, next_row, pltpu.roll(x, rows - 1, axis=0))
    return x + mu_p * (xp - x) + mu_n * (xn - x)


PREP_OUT_NAMES = ('at0', 'bt0', 'kt0', 'rt0', 'gam0', 'at1', 'bt1', 'kt1', 'rt1', 'gam1', 'v', 'g', 'bonus')


def _block_cumprod(w, pos, reverse):
    rows = w.shape[0]
    x = w
    s = 1
    while s < SCAN_SUB:
        if reverse:
            x = x * jnp.where(pos < SCAN_SUB - s, pltpu.roll(x, rows - s, axis=0), 1.0)
        else:
            x = x * jnp.where(pos >= s, pltpu.roll(x, s, axis=0), 1.0)
        s *= 2
    return x


def _block_prev(x, pos, reverse):
    rows = x.shape[0]
    if reverse:
        return jnp.where(pos == SCAN_SUB - 1, 1.0, pltpu.roll(x, rows - 1, axis=0))
    return jnp.where(pos == 0, 1.0, pltpu.roll(x, 1, axis=0))


def _rwkv_prep_kernel(x_ref, xp_ref, xn_ref, l_ref, lp_ref, ln_ref, mu_ref, mul_ref, whi_ref, wlo_ref,
                      vec_ref, ones_ref, *out_refs, lay):
    _, has_prev, has_next = _seq_edges(lay)
    edge = lambda ref, r, ok: jnp.where(ok, ref[r:r + 1, :], 0.0)
    x = _token_shift(x_ref[...], edge(xp_ref, SUBLANES - 1, has_prev), edge(xn_ref, 0, has_next),
                     mu_ref[0:1, :], mu_ref[1:2, :])
    lo = _token_shift(l_ref[...], edge(lp_ref, SUBLANES - 1, has_prev), edge(ln_ref, 0, has_next),
                      mul_ref[0:1, :], mul_ref[1:2, :])
    r = x[:, 0:W_A]
    k = x[:, W_A:2 * W_A]
    v = x[:, 2 * W_A:3 * W_A]
    col = lax.broadcasted_iota(jnp.int32, lo.shape, 1)
    act = jnp.where(col < D_DECAY_LORA, jnp.tanh(lo),
                    jnp.where(col < D_DECAY_LORA + D_AAA_LORA, lo, jax.nn.sigmoid(lo)))
    ones = ones_ref[...]
    pos = lax.broadcasted_iota(jnp.int32, r.shape, 0) % SCAN_SUB
    kk = k * vec_ref[0:1, :]
    kk = kk * lax.rsqrt(_head_sum(kk * kk, ones) + 1e-12)
    outs = {'v': v, 'g': _dot_split(act, whi_ref[4], wlo_ref[4])}
    bonus = None
    for d in range(2):
        w = jnp.exp(-DECAY_SCALE * jax.nn.sigmoid(vec_ref[3 + d:4 + d, :] + _dot_split(act, whi_ref[d], wlo_ref[d])))
        a = jax.nn.sigmoid(vec_ref[5 + d:6 + d, :] + _dot_split(act, whi_ref[2 + d], wlo_ref[2 + d]))
        kd = k * (1 + (a - 1) * vec_ref[1:2, :])
        term = _head_sum(r * kd * vec_ref[2:3, :], ones) * v
        bonus = term if bonus is None else bonus + term
        gam = _block_cumprod(w, pos, reverse=(d == 1))
        inv = 1.0 / gam
        outs['at%d' % d] = -kk * _block_prev(gam, pos, reverse=(d == 1))
        outs['bt%d' % d] = kk * a * inv
        outs['kt%d' % d] = kd * inv
        outs['rt%d' % d] = r * gam
        outs['gam%d' % d] = gam
    outs['bonus'] = bonus
    for ref, name in zip(out_refs, PREP_OUT_NAMES):
        ref[...] = outs[name]


def _rwkv_prep(x_rkv, x_lora, lora_col, P, lay):
    n = x_rkv.shape[0]
    o = 3 * W_A
    pr, nr = _halo_specs(n, SUBLANES, o, 0)
    plr, nlr = _halo_specs(n, SUBLANES, LORA_PAD, lora_col)
    full = lambda a: pl.BlockSpec(a.shape, lambda i: (0,) * a.ndim)
    mu = jnp.stack([P['mu_prev'][0:o], P['mu_next'][0:o]])
    pad = LORA_PAD - LORA_W
    mul = jnp.stack([jnp.pad(P['mu_prev'][o:], (0, pad)), jnp.pad(P['mu_next'][o:], (0, pad))])
    z = lambda rows: jnp.zeros((rows, W_A), jnp.float32)
    wcat = jnp.stack([
        jnp.concatenate([P['w_up'][0], z(LORA_PAD - D_DECAY_LORA)]),
        jnp.concatenate([P['w_up'][1], z(LORA_PAD - D_DECAY_LORA)]),
        jnp.concatenate([z(D_DECAY_LORA), P['a_up'][0], z(LORA_PAD - D_DECAY_LORA - D_AAA_LORA)]),
        jnp.concatenate([z(D_DECAY_LORA), P['a_up'][1], z(LORA_PAD - D_DECAY_LORA - D_AAA_LORA)]),
        jnp.concatenate([z(D_DECAY_LORA + D_AAA_LORA), P['g_up'], z(pad)]),
    ])
    w_hi = wcat.astype(jnp.bfloat16)
    w_lo = (wcat - w_hi.astype(jnp.float32)).astype(jnp.bfloat16)
    vec = jnp.stack([P['k_k'], P['k_a'], P['r_k'].reshape(W_A), P['w0'][0], P['w0'][1], P['a0'][0], P['a0'][1],
                     jnp.zeros((W_A,), jnp.float32)])
    ones = _head_ones()
    row_out = pl.BlockSpec((ROW_TILE, W_A), lambda i: (i, 0))
    outs = pl.pallas_call(
        functools.partial(_rwkv_prep_kernel, lay=lay),
        grid=(lay.n_tiles,),
        in_specs=[pl.BlockSpec((ROW_TILE, o), lambda i: (i, 0)), pr, nr,
                  pl.BlockSpec((ROW_TILE, LORA_PAD), lambda i: (i, lora_col)), plr, nlr,
                  full(mu), full(mul), full(w_hi), full(w_lo), full(vec), full(ones)],
        out_specs=[row_out] * N_PREP_OUT,
        out_shape=[jax.ShapeDtypeStruct((n, W_A), jnp.float32)] * N_PREP_OUT,
        name="rwkv_prep",
        compiler_params=_cparams(),
    )(x_rkv, x_rkv, x_rkv, x_lora, x_lora, x_lora, mu, mul, w_hi, w_lo, vec, ones)
    return dict(zip(PREP_OUT_NAMES, outs))


def _rwkv_post_kernel(yfc_ref, ybc_ref, yfl_ref, ybl_ref, bonus_ref, g_ref, vec_ref, ones_ref, o_ref, *, lay):
    ones = ones_ref[...]
    is_ctx = pl.program_id(0) < lay.ctx_tiles
    y = jnp.where(is_ctx, yfc_ref[...] + ybc_ref[...], yfl_ref[...] + ybl_ref[...])
    yc = y - _head_sum(y, ones) * (1.0 / HEAD_A)
    var = _head_sum(yc * yc, ones) * (1.0 / HEAD_A)
    y = yc * lax.rsqrt(var + GN_EPS)
    y = y * vec_ref[0:1, :] + vec_ref[1:2, :] + bonus_ref[...]
    o_ref[...] = (y * g_ref[...]).astype(o_ref.dtype)


def _rwkv_post(y_ctx, y_lat, bonus, g, ln_g, ln_b, lay):
    n = bonus.shape[0]
    row = pl.BlockSpec((ROW_TILE, W_A), lambda i: (i, 0))
    ctx_row = pl.BlockSpec((ROW_TILE, W_A), lambda i: (jnp.minimum(i, lay.ctx_tiles - 1), 0))
    lat_row = pl.BlockSpec((ROW_TILE, W_A), lambda i: (jnp.maximum(i - lay.ctx_tiles, 0), 0))
    vec = jnp.stack([ln_g, ln_b] + [jnp.zeros_like(ln_g)] * 6)
    ones = _head_ones()
    full = lambda a: pl.BlockSpec(a.shape, lambda i: (0,) * a.ndim)
    return pl.pallas_call(
        functools.partial(_rwkv_post_kernel, lay=lay),
        grid=(lay.n_tiles,),
        in_specs=[ctx_row, ctx_row, lat_row, lat_row, row, row, full(vec), full(ones)],
        out_specs=row,
        out_shape=jax.ShapeDtypeStruct((n, W_A), jnp.bfloat16),
        name="rwkv_post",
        compiler_params=_cparams(),
    )(*y_ctx, *y_lat, bonus, g, vec, ones)


def _grid_transpose(z, rows, cols):
    b, t, ch = z.shape
    return z.reshape(b, rows, cols, ch).transpose(0, 2, 1, 3).reshape(b, t, ch)


def _rwkv_group(f, P, lay, dims, s_ctx0, s_lat0, transposed):
    bc, tc, bl, tl = dims
    nc = bc * tc
    rows = tl // GRID_W
    if transposed:
        def permuted(cols):
            lat = _grid_transpose(f[nc:, cols].reshape(bl, tl, -1), rows, GRID_W)
            return jnp.concatenate([f[:nc, cols], lat.reshape(bl * tl, -1)], axis=0)
        x_rkv = permuted(slice(0, 3 * W_A))
        x_lora = permuted(slice(LORA_COL * W_A, LORA_COL * W_A + LORA_PAD))
        lora_col = 0
    else:
        x_rkv, x_lora, lora_col = f, f, LORA_COL * W_A // LORA_PAD
    ops = _rwkv_prep(x_rkv, x_lora, lora_col, P, lay)
    yf_c, yb_c, s_fin = _rwkv_scan(ops, s_ctx0, 0, bc, tc)
    yf_l, yb_l, _ = _rwkv_scan(ops, s_lat0, nc, bl, tl)
    out = _rwkv_post((yf_c, yb_c), (yf_l, yb_l), ops['bonus'], ops['g'], P['rwkv_ln_g'], P['rwkv_ln_b'], lay)
    if transposed:
        lat = _grid_transpose(out[nc:].reshape(bl, tl, W_A), GRID_W, rows)
        out = jnp.concatenate([out[:nc], lat.reshape(bl * tl, W_A)], axis=0)
    return out, s_fin


def _modulation_kernel(c_ref, w_ref, b_ref, o_ref):
    c = c_ref[...]
    w = w_ref[...]
    w_hi = w.astype(jnp.bfloat16)
    w_lo = (w - w_hi.astype(jnp.float32)).astype(jnp.bfloat16)
    o_ref[...] = _dot_split(c * jax.nn.sigmoid(c), w_hi, w_lo) + b_ref[...]


def _modulation(cvec, w_ada, b_ada):
    n_c, d = cvec.shape
    n_layers, _, n_out = w_ada.shape
    c_pad = jnp.pad(cvec, ((0, SUBLANES - n_c), (0, 0)))
    m = pl.pallas_call(
        _modulation_kernel,
        grid=(n_layers, n_out // ADA_TN),
        in_specs=[pl.BlockSpec((SUBLANES, d), lambda l, j: (0, 0)),
                  pl.BlockSpec((None, d, ADA_TN), lambda l, j: (l, 0, j)),
                  pl.BlockSpec((None, 1, ADA_TN), lambda l, j: (l, 0, j))],
        out_specs=pl.BlockSpec((None, SUBLANES, ADA_TN), lambda l, j: (l, 0, j)),
        out_shape=jax.ShapeDtypeStruct((n_layers, SUBLANES, n_out), jnp.float32),
        name="modulation",
        compiler_params=pltpu.CompilerParams(dimension_semantics=("parallel", "parallel"),
                                             vmem_limit_bytes=VMEM_LIMIT_BYTES),
    )(c_pad, w_ada, b_ada.reshape(n_layers, 1, n_out))
    return [m[l, :n_c].reshape(n_c, N_MOD, D_MODEL) for l in range(n_layers)]


def _pack_w_in(w_in):
    o = 3 * W_A
    parts = [w_in[..., 0:o], w_in[..., COLS_A:], w_in[..., o:COLS_A]]
    parts = [p.astype(jnp.bfloat16) for p in parts]
    parts.append(jnp.zeros(w_in.shape[:-1] + (IN_PAD - IN_COLS,), jnp.bfloat16))
    return jnp.concatenate(parts, axis=-1)


def kernel(x_prompt, x_sample, c, state_rwkv, c_ctx, w_ada, b_ada, w_in, mu_prev, mu_next, rwkv_g_up, rwkv_w0, rwkv_w_up, rwkv_a0, rwkv_a_up, rwkv_k_k, rwkv_k_a, rwkv_r_k, rwkv_ln_g, rwkv_ln_b, pool_w, pool_b, pool_scale, conf_conv_w, conf_conv_b, conf_ln_g, conf_ln_b, gmlp_ln_g, gmlp_ln_b, gmlp_ws, gmlp_bs, w_out, ln1_g, ln1_b, ffn_w_up, ffn_conv_w, ffn_conv_b, ffn_w_down, ln2_g, ln2_b):
    bc, tc, d = x_prompt.shape
    bl, tl, _ = x_sample.shape
    nc = bc * tc
    lay = TokenLayout(bc, tc, bl, tl)
    s_ctx0 = jnp.zeros((bc, 2, H_A, HEAD_A, HEAD_A), jnp.float32)
    cvec = jnp.concatenate([c_ctx[None, :], c], axis=0)
    mods = _modulation(cvec, w_ada, b_ada)
    w_in_p = _pack_w_in(w_in)
    w_out_b = w_out.astype(jnp.bfloat16)
    w_up_b = ffn_w_up.astype(jnp.bfloat16)
    w_dn_b = jnp.concatenate([ffn_w_down.astype(jnp.bfloat16),
                              jnp.zeros((DEPTH, FF_PAD - D_FF, d), jnp.bfloat16)], axis=1)
    x = jnp.concatenate([x_prompt.reshape(nc, d), x_sample.reshape(bl * tl, d)], axis=0)
    h = _modulate(x, jnp.stack([mods[0][:, 1], mods[0][:, 0]], axis=1), lay)
    new_states = []
    for l in range(DEPTH):
        P = {
            'mu_prev': mu_prev[l], 'mu_next': mu_next[l], 'g_up': rwkv_g_up[l],
            'w0': rwkv_w0[l], 'w_up': rwkv_w_up[l], 'a0': rwkv_a0[l], 'a_up': rwkv_a_up[l],
            'k_k': rwkv_k_k[l], 'k_a': rwkv_k_a[l], 'r_k': rwkv_r_k[l],
            'rwkv_ln_g': rwkv_ln_g[l], 'rwkv_ln_b': rwkv_ln_b[l],
        }

        m = mods[l]

        f = _matmul(h, w_in_p, l, jnp.float32, 1024, 512, D_MODEL)
        y_a, s_fin = _rwkv_group(f, P, lay, (bc, tc, bl, tl), s_ctx0, state_rwkv[:, l], l % 2 == 1)
        new_states.append(s_fin)
        y_b = _pool_group(f, pool_w[l], pool_b[l], pool_scale[l], lay)
        y_c = _conformer_group(f, conf_conv_w[l], conf_conv_b[l], conf_ln_g[l], conf_ln_b[l], lay)
        y_d = _gmlp_group(f, gmlp_ln_g[l], gmlp_ln_b[l], gmlp_ws[l], gmlp_bs[l], lay)
        y = _matmul4([y_a, y_b, y_c, y_d], w_out_b, l, 1024, 1024)
        x, h = _ln_residual(y, x, jnp.stack([m[:, 2], m[:, 4], m[:, 3]], axis=1), ln1_g[l], ln1_b[l], lay, False)

        z = _matmul(h, w_up_b, l, jnp.bfloat16, 1024, 512, D_MODEL)
        g = _ffn_gate(z, ffn_conv_w[l], ffn_conv_b[l], lay)
        y = _matmul(g, w_dn_b, l, jnp.float32, 1024, 1024, FF_PAD // 4)
        if l + 1 < DEPTH:
            mn = mods[l + 1]
            x, h = _ln_residual(y, x, jnp.stack([m[:, 5], mn[:, 1], mn[:, 0]], axis=1), ln2_g[l], ln2_b[l], lay, False)
        else:
            x_ctx, x_lat = _ln_residual(y, x, jnp.stack([m[:, 5], m[:, 5], m[:, 5]], axis=1), ln2_g[l], ln2_b[l],
                                        lay, True)

    new_state_rwkv = jnp.stack(new_states, axis=1).astype(x_prompt.dtype)
    return (x_ctx.reshape(bc, tc, d), x_lat.reshape(bl, tl, d), new_state_rwkv)
```

```python
import functools
import math

import numpy as np
import jax
import jax.numpy as jnp
from jax import lax
from jax.experimental import pallas as pl
from jax.experimental.pallas import tpu as pltpu

D_MODEL = 4096
DEPTH = 2
GRID_W = 64
W_A = D_MODEL // 4
W_B = D_MODEL // 4
W_C = D_MODEL // 4
W_D = D_MODEL - W_A - W_B - W_C
HEAD_A = 64
H_A = W_A // HEAD_A
D_DECAY_LORA = max(32, int(round(W_A ** 0.5 * 1.8 / 32)) * 32)
D_AAA_LORA = max(32, int(round(W_A ** 0.5 * 1.8 / 32)) * 32)
D_GATE_LORA = max(32, int(round(W_A ** 0.8 * 0.6 / 32)) * 32)
COLS_A = 3 * W_A + D_DECAY_LORA + D_AAA_LORA + D_GATE_LORA
DECAY_SCALE = math.exp(-0.5)
GN_EPS = 64e-5
POOL_WINDOWS = (2, 4, 8, 16)
POOL_GROUP = W_B // len(POOL_WINDOWS)
CONV_K = 31
CHUNK = 128
H_D = 8
HEAD_D = W_D // H_D
IN_COLS = COLS_A + W_B + 2 * W_C + 2 * W_D
D_FF = ((8 * D_MODEL // 3 + 255) // 256) * 256
FFN_K = 3
ALPHA = (2.0 * DEPTH) ** 0.25
LN_EPS = 1e-5
N_MOD = 6
IN_PAD = ((IN_COLS + 511) // 512) * 512
FF_PAD = ((D_FF + 1023) // 1024) * 1024

LANES = 128
SUBLANES = 8
VMEM_LIMIT_BYTES = 56 * 1024 * 1024

SCAN_TB = 256
SCAN_SUB = 32
SCAN_PARTS = 3
N_KEYVEC = 5
N_STEPVEC = 4
SCAN_TRIP = 4

ROW_TILE = 256
FF_CHUNK = 256
ADA_TN = 1024
CONV_HALO = 16
POOL_HALO = 8
CONV_ROWS = 32
COL_B, COL_CA, COL_CG, COL_DU, COL_DV, LORA_COL = 3, 4, 5, 6, 7, 8
LORA_W = D_DECAY_LORA + D_AAA_LORA + D_GATE_LORA
LORA_PAD = IN_PAD - LORA_COL * W_A
N_PREP_OUT = 13


def _mm_kernel(x_ref, w_ref, o_ref, *acc, nk):
    part = jnp.dot(x_ref[...], w_ref[...], preferred_element_type=jnp.float32)
    if nk == 1:
        o_ref[...] = part.astype(o_ref.dtype)
    else:
        k = pl.program_id(2)
        acc_ref = acc[0]

        @pl.when(k == 0)
        def _():
            acc_ref[...] = part

        @pl.when(k > 0)
        def _():
            acc_ref[...] += part

        @pl.when(k == nk - 1)
        def _():
            o_ref[...] = acc_ref[...].astype(o_ref.dtype)


def _matmul(x, w_all, layer, out_dtype, tm, tn, tk):
    m, kdim = x.shape
    _, _, n = w_all.shape
    assert m % tm == 0 and n % tn == 0 and kdim % tk == 0
    nk = kdim // tk
    return pl.pallas_call(
        functools.partial(_mm_kernel, nk=nk),
        grid=(m // tm, n // tn, nk),
        in_specs=[
            pl.BlockSpec((tm, tk), lambda i, j, k: (i, k)),
            pl.BlockSpec((None, tk, tn), lambda i, j, k: (layer, k, j)),
        ],
        out_specs=pl.BlockSpec((tm, tn), lambda i, j, k: (i, j)),
        out_shape=jax.ShapeDtypeStruct((m, n), out_dtype),
        name="dense_mm",
        scratch_shapes=[pltpu.VMEM((tm, tn), jnp.float32)] if nk > 1 else [],
        compiler_params=pltpu.CompilerParams(
            dimension_semantics=("parallel", "parallel", "arbitrary"),
            vmem_limit_bytes=VMEM_LIMIT_BYTES,
        ),
    )(x, w_all)


def _sel_table():
    c = np.arange(2 * LANES)[:, None]
    n = np.arange(2 * LANES)[None, :]
    e = np.zeros((SCAN_SUB // 2, 2 * LANES, 2 * LANES), np.float32)
    for p in range(SCAN_SUB // 2):
        t = 2 * p + n // LANES
        e[p] = (((c % LANES) < SCAN_PARTS * SCAN_SUB) & ((c % SCAN_SUB) == t)
                & ((c // LANES) == ((n % LANES) // HEAD_A)))
    return jnp.asarray(e, jnp.bfloat16)


def _split3_rows(x):
    hi = x.astype(jnp.bfloat16).astype(jnp.float32)
    r1 = x - hi
    mid = r1.astype(jnp.bfloat16).astype(jnp.float32)
    lo = r1 - mid
    return jnp.concatenate([hi, mid, lo, jnp.zeros_like(x)], axis=0)


def _scan_kernel(*refs, nblk):
    key_refs = (refs[0:5], refs[6:11])
    v_refs = (refs[5], refs[11])
    e_ref, s0_ref = refs[12], refs[13]
    y_refs = (refs[14], refs[15])
    sfin_ref, st_ref, lhs_ref = refs[16], refs[17], refs[18]
    tb = pl.program_id(2)

    @pl.when(tb == 0)
    def _():
        st_ref[...] = s0_ref[0, :, 0]

    lane = lax.broadcasted_iota(jnp.int32, (SCAN_SUB, LANES), 1)
    first_head = lane < HEAD_A

    half = N_KEYVEC * HEAD_A
    npairs = SCAN_SUB // 2

    def build_lhs(slot, offs):
        for c in range(2):
            xs = [ref[pl.ds(offs[c], SCAN_SUB), :] for ref in key_refs[c]]
            for pi, (i1, i2) in enumerate(((0, 1), (2, 3), (4, None))):
                x1 = xs[i1]
                xr1 = pltpu.roll(x1, HEAD_A, axis=1)
                if i2 is None:
                    comb_a, comb_b = x1, xr1
                else:
                    x2 = xs[i2]
                    xr2 = pltpu.roll(x2, HEAD_A, axis=1)
                    comb_a = jnp.where(first_head, x1, xr2)
                    comb_b = jnp.where(first_head, xr1, x2)
                mt_a = _split3_rows(comb_a).T.astype(jnp.bfloat16)
                mt_b = _split3_rows(comb_b).T.astype(jnp.bfloat16)
                r0 = (c * N_KEYVEC + 2 * pi) * HEAD_A
                lhs_ref[slot, r0:r0 + HEAD_A, 0:LANES] = mt_a[0:HEAD_A]
                lhs_ref[slot, r0:r0 + HEAD_A, LANES:2 * LANES] = mt_b[0:HEAD_A]
                if i2 is not None:
                    lhs_ref[slot, r0 + HEAD_A:r0 + 2 * HEAD_A, 0:LANES] = mt_a[HEAD_A:2 * HEAD_A]
                    lhs_ref[slot, r0 + HEAD_A:r0 + 2 * HEAD_A, LANES:2 * LANES] = mt_b[HEAD_A:2 * HEAD_A]

    def sub_offsets(q):
        return (pl.multiple_of(q * SCAN_SUB, SCAN_SUB),
                pl.multiple_of(SCAN_TB - (q + 1) * SCAN_SUB, SCAN_SUB))

    def run_sub(slot, offs, hs):
        vs = [v_refs[c][pl.ds(offs[c], SCAN_SUB), :] for c in range(2)]
        ys = ([], [])
        gs = list(hs)
        step_rows = N_STEPVEC * HEAD_A
        for p in range(npairs):
            tiles = (jnp.dot(lhs_ref[slot, 0:step_rows, :], e_ref[p], preferred_element_type=jnp.float32),
                     jnp.dot(lhs_ref[slot, half:half + step_rows, :], e_ref[npairs - 1 - p],
                             preferred_element_type=jnp.float32))
            for s in range(2):
                u = 2 * p + s
                for c in range(2):
                    col = s if c == 0 else 1 - s
                    row = u if c == 0 else SCAN_SUB - 1 - u
                    a_t, b_t, k_t, r_t = (
                        tiles[c][k * HEAD_A:(k + 1) * HEAD_A, col * LANES:(col + 1) * LANES]
                        for k in range(N_STEPVEC))
                    g = gs[c]
                    sa = jnp.sum(g * a_t, axis=0, keepdims=True)
                    g = g + sa * b_t + vs[c][row:row + 1, :] * k_t
                    ys[c].append(jnp.sum(g * r_t, axis=0, keepdims=True))
                    gs[c] = g
        y_refs[0][pl.ds(offs[0], SCAN_SUB), :] = jnp.concatenate(ys[0], axis=0)
        y_refs[1][pl.ds(offs[1], SCAN_SUB), :] = jnp.concatenate(ys[1][::-1], axis=0)
        gam_f = jnp.dot(lhs_ref[slot, step_rows:half, :], e_ref[npairs - 1],
                        preferred_element_type=jnp.float32)[:, LANES:2 * LANES]
        gam_b = jnp.dot(lhs_ref[slot, half + step_rows:2 * half, :], e_ref[0],
                        preferred_element_type=jnp.float32)[:, 0:LANES]
        return [gs[0] * gam_f, gs[1] * gam_b]

    def trip(g, carry):
        hs = list(carry)
        offs = [sub_offsets(g * SCAN_TRIP + k) for k in range(SCAN_TRIP)]
        for k in range(SCAN_TRIP):
            build_lhs(k, offs[k])
        for k in range(SCAN_TRIP):
            hs = run_sub(k, offs[k], hs)
        return tuple(hs)

    hf, hb = lax.fori_loop(0, SCAN_TB // (SCAN_SUB * SCAN_TRIP), trip, (st_ref[0], st_ref[1]))
    st_ref[0] = hf
    st_ref[1] = hb

    @pl.when(tb == nblk - 1)
    def _():
        sfin_ref[0, 0, 0] = hf
        sfin_ref[0, 1, 0] = hb


def _rwkv_scan(ops, s0, row0, bsz, t_len):
    npair = H_A // 2
    nblk = t_len // SCAN_TB
    blk0 = row0 // SCAN_TB
    s0p = s0.reshape(bsz, 2, npair, 2, HEAD_A, HEAD_A).transpose(0, 1, 2, 5, 3, 4)
    s0p = s0p.reshape(bsz, 2, npair, HEAD_A, 2 * HEAD_A)
    ins = []
    for d in range(2):
        ins += [ops[n + str(d)] for n in ('at', 'bt', 'kt', 'rt', 'gam')] + [ops['v']]
    blk = (SCAN_TB, LANES)
    fwd_in = pl.BlockSpec(blk, lambda b, p, t: (blk0 + b * nblk + t, p))
    bwd_in = pl.BlockSpec(blk, lambda b, p, t: (blk0 + b * nblk + nblk - 1 - t, p))
    fwd_out = pl.BlockSpec(blk, lambda b, p, t: (b * nblk + t, p))
    bwd_out = pl.BlockSpec(blk, lambda b, p, t: (b * nblk + nblk - 1 - t, p))
    st_spec = pl.BlockSpec((1, 2, 1, HEAD_A, LANES), lambda b, p, t: (b, 0, p, 0, 0))
    e_tab = _sel_table()
    in_specs = ([fwd_in] * 6 + [bwd_in] * 6
                + [pl.BlockSpec(e_tab.shape, lambda b, p, t: (0, 0, 0)), st_spec])
    out_shape = [
        jax.ShapeDtypeStruct((bsz * t_len, W_A), jnp.float32),
        jax.ShapeDtypeStruct((bsz * t_len, W_A), jnp.float32),
        jax.ShapeDtypeStruct((bsz, 2, npair, HEAD_A, LANES), jnp.float32),
    ]
    yf, yb, sfin = pl.pallas_call(
        functools.partial(_scan_kernel, nblk=nblk),
        grid=(bsz, npair, nblk),
        in_specs=in_specs,
        out_specs=[fwd_out, bwd_out, st_spec],
        out_shape=out_shape,
        name="rwkv_scan",
        scratch_shapes=[pltpu.VMEM((2, HEAD_A, LANES), jnp.float32),
                        pltpu.VMEM((SCAN_TRIP, 2 * N_KEYVEC * HEAD_A, 2 * LANES), jnp.bfloat16)],
        compiler_params=pltpu.CompilerParams(
            dimension_semantics=("parallel", "parallel", "arbitrary"),
            vmem_limit_bytes=VMEM_LIMIT_BYTES,
        ),
    )(*ins, e_tab, s0p)
    sfin = sfin.reshape(bsz, 2, npair, HEAD_A, 2, HEAD_A).transpose(0, 1, 2, 4, 5, 3)
    return yf, yb, sfin.reshape(bsz, 2, H_A, HEAD_A, HEAD_A)


class TokenLayout:
    def __init__(self, bc, tc, bl, tl):
        assert tc % ROW_TILE == 0 and tl % ROW_TILE == 0
        self.n_rows = bc * tc + bl * tl
        self.n_tiles = self.n_rows // ROW_TILE
        self.ctx_tiles = bc * tc // ROW_TILE
        self.ctx_seq_tiles = tc // ROW_TILE
        self.lat_seq_tiles = tl // ROW_TILE

    def group(self, i):
        return jnp.where(i < self.ctx_tiles, 0, 1 + (i - self.ctx_tiles) // self.lat_seq_tiles)

    def pos_in_seq(self, i):
        return jnp.where(i < self.ctx_tiles, i % self.ctx_seq_tiles, (i - self.ctx_tiles) % self.lat_seq_tiles)

    def seq_tiles(self, i):
        return jnp.where(i < self.ctx_tiles, self.ctx_seq_tiles, self.lat_seq_tiles)


def _modulate_kernel(x_ref, m_ref, h_ref):
    x = x_ref[...]
    h_ref[...] = (x * (1 + m_ref[0, 0:1, :]) + m_ref[0, 1:2, :]).astype(h_ref.dtype)


def _modulate(x, m2, lay):
    n, d = x.shape
    return pl.pallas_call(
        _modulate_kernel,
        grid=(lay.n_tiles,),
        in_specs=[pl.BlockSpec((ROW_TILE, d), lambda i: (i, 0)),
                  pl.BlockSpec((1, 2, d), lambda i: (lay.group(i), 0, 0))],
        out_specs=pl.BlockSpec((ROW_TILE, d), lambda i: (i, 0)),
        out_shape=jax.ShapeDtypeStruct((n, d), jnp.bfloat16),
        name="modulate",
        compiler_params=pltpu.CompilerParams(dimension_semantics=("parallel",),
                                             vmem_limit_bytes=VMEM_LIMIT_BYTES),
    )(x, m2)


def _ln_residual_kernel(y_ref, x_ref, m_ref, g_ref, b_ref, *o_refs, lay, split):
    v = ALPHA * x_ref[...] + m_ref[0, 0:1, :] * y_ref[...]
    vc = v - jnp.mean(v, -1, keepdims=True)
    var = jnp.mean(vc * vc, -1, keepdims=True)
    xn = (vc * lax.rsqrt(var + LN_EPS)) * g_ref[...] + b_ref[...]
    if split:
        is_ctx = pl.program_id(0) < lay.ctx_tiles

        @pl.when(is_ctx)
        def _():
            o_refs[0][...] = xn

        @pl.when(jnp.logical_not(is_ctx))
        def _():
            o_refs[1][...] = xn
    else:
        o_refs[0][...] = xn
        o_refs[1][...] = (xn * (1 + m_ref[0, 1:2, :]) + m_ref[0, 2:3, :]).astype(jnp.bfloat16)


def _ln_residual(y, x, m3, g, b, lay, split):
    n, d = x.shape
    row = pl.BlockSpec((ROW_TILE, d), lambda i: (i, 0))
    vec = pl.BlockSpec((1, d), lambda i: (0, 0))
    if split:
        n_ctx = lay.ctx_tiles * ROW_TILE
        out_shape = [jax.ShapeDtypeStruct((n_ctx, d), jnp.float32),
                     jax.ShapeDtypeStruct((n - n_ctx, d), jnp.float32)]
        out_specs = [pl.BlockSpec((ROW_TILE, d), lambda i: (jnp.minimum(i, lay.ctx_tiles - 1), 0)),
                     pl.BlockSpec((ROW_TILE, d), lambda i: (jnp.maximum(i - lay.ctx_tiles, 0), 0))]
    else:
        out_shape = [jax.ShapeDtypeStruct((n, d), jnp.float32), jax.ShapeDtypeStruct((n, d), jnp.bfloat16)]
        out_specs = [row, row]
    return pl.pallas_call(
        functools.partial(_ln_residual_kernel, lay=lay, split=split),
        grid=(lay.n_tiles,),
        in_specs=[row, row, pl.BlockSpec((1, 3, d), lambda i: (lay.group(i), 0, 0)), vec, vec],
        out_specs=out_specs,
        out_shape=out_shape,
        name="ln_residual",
        compiler_params=pltpu.CompilerParams(dimension_semantics=("arbitrary",),
                                             vmem_limit_bytes=VMEM_LIMIT_BYTES),
    )(y, x, m3, g.reshape(1, d), b.reshape(1, d))


def _ffn_gate_kernel(zc_ref, zp_ref, zn_ref, w_ref, b_ref, o_ref, *, lay):
    i = pl.program_id(0)
    pos = lay.pos_in_seq(i)
    has_prev = pos > 0
    has_next = pos < lay.seq_tiles(i) - 1
    row = lax.broadcasted_iota(jnp.int32, (ROW_TILE, FF_CHUNK), 0)
    n_chunks = D_FF // FF_CHUNK

    def conv(col):
        sl = pl.ds(col, FF_CHUNK)
        z = zc_ref[:, sl].astype(jnp.float32)
        zprev_row = jnp.where(has_prev, zp_ref[SUBLANES - 1:SUBLANES, sl].astype(jnp.float32), 0.0)
        znext_row = jnp.where(has_next, zn_ref[0:1, sl].astype(jnp.float32), 0.0)
        zprev = jnp.where(row == 0, zprev_row, pltpu.roll(z, 1, axis=0))
        znext = jnp.where(row == ROW_TILE - 1, znext_row, pltpu.roll(z, ROW_TILE - 1, axis=0))
        return zprev * w_ref[0:1, sl] + b_ref[0:1, sl] + z * w_ref[1:2, sl] + znext * w_ref[2:3, sl]

    def body(j, carry):
        ca = pl.multiple_of(j * FF_CHUNK, FF_CHUNK)
        cb = pl.multiple_of(D_FF + j * FF_CHUNK, FF_CHUNK)
        a = conv(ca)
        bv = conv(cb)
        o_ref[:, pl.ds(ca, FF_CHUNK)] = (a * jax.nn.sigmoid(a) * bv).astype(o_ref.dtype)
        return carry

    lax.fori_loop(0, n_chunks, body, 0)
    o_ref[:, D_FF:FF_PAD] = jnp.zeros((ROW_TILE, FF_PAD - D_FF), o_ref.dtype)


def _ffn_gate(z, conv_w, conv_b, lay):
    n, c2 = z.shape
    rt8 = ROW_TILE // SUBLANES
    last8 = n // SUBLANES - 1
    return pl.pallas_call(
        functools.partial(_ffn_gate_kernel, lay=lay),
        grid=(lay.n_tiles,),
        in_specs=[pl.BlockSpec((ROW_TILE, c2), lambda i: (i, 0)),
                  pl.BlockSpec((SUBLANES, c2), lambda i: (jnp.maximum(i * rt8 - 1, 0), 0)),
                  pl.BlockSpec((SUBLANES, c2), lambda i: (jnp.minimum((i + 1) * rt8, last8), 0)),
                  pl.BlockSpec((3, c2), lambda i: (0, 0)),
                  pl.BlockSpec((1, c2), lambda i: (0, 0))],
        out_specs=pl.BlockSpec((ROW_TILE, FF_PAD), lambda i: (i, 0)),
        out_shape=jax.ShapeDtypeStruct((n, FF_PAD), jnp.bfloat16),
        name="ffn_gate",
        compiler_params=pltpu.CompilerParams(dimension_semantics=("parallel",),
                                             vmem_limit_bytes=VMEM_LIMIT_BYTES),
    )(z, z, z, conv_w, conv_b.reshape(1, c2))


def _cparams():
    return pltpu.CompilerParams(dimension_semantics=("parallel",), vmem_limit_bytes=VMEM_LIMIT_BYTES)


def _halo_specs(n_rows, halo, width, col):
    per_tile = ROW_TILE // halo
    last = n_rows // halo - 1
    prev = pl.BlockSpec((halo, width), lambda i: (jnp.maximum(i * per_tile - 1, 0), col))
    nxt = pl.BlockSpec((halo, width), lambda i: (jnp.minimum((i + 1) * per_tile, last), col))
    return prev, nxt


def _seq_edges(lay):
    i = pl.program_id(0)
    pos = lay.pos_in_seq(i)
    return pos, pos > 0, pos < lay.seq_tiles(i) - 1


def _row_layer_norm(x, g, b):
    xc = x - jnp.mean(x, -1, keepdims=True)
    var = jnp.mean(xc * xc, -1, keepdims=True)
    return (xc * lax.rsqrt(var + LN_EPS)) * g + b


def _conformer_kernel(a_ref, g_ref, ap_ref, gp_ref, an_ref, gn_ref, w_ref, b_ref, lg_ref, lb_ref,
                      o_ref, hext_ref, y_ref, *, lay):
    _, has_prev, has_next = _seq_edges(lay)
    glu = lambda a, g: a * jax.nn.sigmoid(g)
    hext_ref[0:CONV_HALO] = jnp.where(has_prev, glu(ap_ref[...], gp_ref[...]), 0.0)
    hext_ref[CONV_HALO:CONV_HALO + ROW_TILE] = glu(a_ref[...], g_ref[...])
    hext_ref[CONV_HALO + ROW_TILE:] = jnp.where(has_next, glu(an_ref[...], gn_ref[...]), 0.0)
    first = CONV_HALO - CONV_K // 2
    for rc in range(ROW_TILE // CONV_ROWS):
        base = first + rc * CONV_ROWS
        acc = hext_ref[base:base + CONV_ROWS] * w_ref[0:1] + b_ref[...]
        for j in range(1, CONV_K):
            acc = acc + hext_ref[base + j:base + j + CONV_ROWS] * w_ref[j:j + 1]
        y_ref[rc * CONV_ROWS:(rc + 1) * CONV_ROWS] = acc
    y = _row_layer_norm(y_ref[...], lg_ref[...], lb_ref[...])
    o_ref[...] = (y * jax.nn.sigmoid(y)).astype(o_ref.dtype)


def _conformer_group(f, conv_w, conv_b, ln_g, ln_b, lay):
    n = f.shape[0]
    cur = lambda col: pl.BlockSpec((ROW_TILE, W_C), lambda i: (i, col))
    pa, na = _halo_specs(n, CONV_HALO, W_C, COL_CA)
    pg, ng = _halo_specs(n, CONV_HALO, W_C, COL_CG)
    vec = pl.BlockSpec((1, W_C), lambda i: (0, 0))
    return pl.pallas_call(
        functools.partial(_conformer_kernel, lay=lay),
        grid=(lay.n_tiles,),
        in_specs=[cur(COL_CA), cur(COL_CG), pa, pg, na, ng,
                  pl.BlockSpec((CONV_K, W_C), lambda i: (0, 0)), vec, vec, vec],
        out_specs=pl.BlockSpec((ROW_TILE, W_C), lambda i: (i, 0)),
        out_shape=jax.ShapeDtypeStruct((n, W_C), jnp.bfloat16),
        name="conformer_group",
        scratch_shapes=[pltpu.VMEM((ROW_TILE + 2 * CONV_HALO, W_C), jnp.float32),
                        pltpu.VMEM((ROW_TILE, W_C), jnp.float32)],
        compiler_params=_cparams(),
    )(f, f, f, f, f, f, conv_w, conv_b.reshape(1, W_C), ln_g.reshape(1, W_C), ln_b.reshape(1, W_C))


def _gmlp_kernel(u_ref, v_ref, lg_ref, lb_ref, ws_ref, bias_ref, o_ref):
    v = _row_layer_norm(jax.nn.gelu(v_ref[...]), lg_ref[...], lb_ref[...]).astype(jnp.bfloat16)
    for ch in range(ROW_TILE // CHUNK):
        rows = slice(ch * CHUNK, (ch + 1) * CHUNK)
        for h in range(H_D):
            cols = slice(h * HEAD_D, (h + 1) * HEAD_D)
            s = jnp.dot(ws_ref[h], v[rows, cols], preferred_element_type=jnp.float32) + bias_ref[:, cols]
            o_ref[rows, cols] = (jax.nn.gelu(u_ref[rows, cols]) * s).astype(o_ref.dtype)


def _gmlp_group(f, ln_g, ln_b, ws, bs, lay):
    n = f.shape[0]
    vec = pl.BlockSpec((1, W_D), lambda i: (0, 0))
    bias = jnp.repeat(bs.T, HEAD_D, axis=1)
    return pl.pallas_call(
        _gmlp_kernel,
        grid=(lay.n_tiles,),
        in_specs=[pl.BlockSpec((ROW_TILE, W_D), lambda i: (i, COL_DU)),
                  pl.BlockSpec((ROW_TILE, W_D), lambda i: (i, COL_DV)),
                  vec, vec,
                  pl.BlockSpec((H_D, CHUNK, CHUNK), lambda i: (0, 0, 0)),
                  pl.BlockSpec((CHUNK, W_D), lambda i: (0, 0))],
        out_specs=pl.BlockSpec((ROW_TILE, W_D), lambda i: (i, 0)),
        out_shape=jax.ShapeDtypeStruct((n, W_D), jnp.bfloat16),
        name="gmlp_group",
        compiler_params=_cparams(),
    )(f, f, ln_g.reshape(1, W_D), ln_b.reshape(1, W_D), ws.astype(jnp.bfloat16), bias)


def _pool_kernel(x_ref, xp_ref, xn_ref, w_ref, b_ref, sc_ref, o_ref, hext_ref, *, lay):
    pos, has_prev, has_next = _seq_edges(lay)
    hext_ref[0:POOL_HALO] = jnp.where(has_prev, xp_ref[...], 0.0)
    hext_ref[POOL_HALO:POOL_HALO + ROW_TILE] = x_ref[...]
    hext_ref[POOL_HALO + ROW_TILE:] = jnp.where(has_next, xn_ref[...], 0.0)
    t = pos * ROW_TILE + lax.broadcasted_iota(jnp.int32, (ROW_TILE, POOL_GROUP), 0)
    t_len = lay.seq_tiles(pl.program_id(0)) * ROW_TILE
    for gi, win in enumerate(POOL_WINDOWS):
        cols = slice(gi * POOL_GROUP, (gi + 1) * POOL_GROUP)
        half = win // 2
        ssum = hext_ref[POOL_HALO - half:POOL_HALO - half + ROW_TILE, cols]
        for dlt in range(-half + 1, half):
            ssum = ssum + hext_ref[POOL_HALO + dlt:POOL_HALO + dlt + ROW_TILE, cols]
        cnt = (jnp.minimum(t + half, t_len) - jnp.maximum(t - half, 0)).astype(jnp.float32)
        pooled = (ssum / cnt - x_ref[:, cols]).astype(jnp.bfloat16)
        y = jnp.dot(pooled, w_ref[gi], preferred_element_type=jnp.float32) + b_ref[:, cols]
        o_ref[:, cols] = (y * sc_ref[:, cols]).astype(o_ref.dtype)


def _pool_group(f, pool_w, pool_b, pool_scale, lay):
    n = f.shape[0]
    prev, nxt = _halo_specs(n, POOL_HALO, W_B, COL_B)
    vec = pl.BlockSpec((1, W_B), lambda i: (0, 0))
    return pl.pallas_call(
        functools.partial(_pool_kernel, lay=lay),
        grid=(lay.n_tiles,),
        in_specs=[pl.BlockSpec((ROW_TILE, W_B), lambda i: (i, COL_B)), prev, nxt,
                  pl.BlockSpec(pool_w.shape, lambda i: (0, 0, 0)), vec, vec],
        out_specs=pl.BlockSpec((ROW_TILE, W_B), lambda i: (i, 0)),
        out_shape=jax.ShapeDtypeStruct((n, W_B), jnp.bfloat16),
        name="pool_group",
        scratch_shapes=[pltpu.VMEM((ROW_TILE + 2 * POOL_HALO, W_B), jnp.float32)],
        compiler_params=_cparams(),
    )(f, f, f, pool_w.astype(jnp.bfloat16), pool_b.reshape(1, W_B), pool_scale.reshape(1, W_B))


def _mm4_kernel(y0, y1, y2, y3, w_ref, o_ref):
    wk = w_ref.shape[0] // 4
    acc = jnp.dot(y0[...], w_ref[0:wk, :], preferred_element_type=jnp.float32)
    for k, y in enumerate((y1, y2, y3), start=1):
        acc = acc + jnp.dot(y[...], w_ref[k * wk:(k + 1) * wk, :], preferred_element_type=jnp.float32)
    o_ref[...] = acc


def _matmul4(ys, w_all, layer, tm, tn):
    m = ys[0].shape[0]
    _, kdim, n = w_all.shape
    yspec = pl.BlockSpec((tm, kdim // 4), lambda i, j: (i, 0))
    return pl.pallas_call(
        _mm4_kernel,
        grid=(m // tm, n // tn),
        in_specs=[yspec] * 4 + [pl.BlockSpec((None, kdim, tn), lambda i, j: (layer, 0, j))],
        out_specs=pl.BlockSpec((tm, tn), lambda i, j: (i, j)),
        out_shape=jax.ShapeDtypeStruct((m, n), jnp.float32),
        name="dense_mm4",
        compiler_params=pltpu.CompilerParams(dimension_semantics=("parallel", "parallel"),
                                             vmem_limit_bytes=VMEM_LIMIT_BYTES),
    )(*ys, w_all)


def _head_ones():
    i = np.arange(LANES)
    return jnp.asarray((i[:, None] // HEAD_A) == (i[None, :] // HEAD_A), jnp.bfloat16)


def _head_sum(x, ones):
    hi = x.astype(jnp.bfloat16)
    lo = (x - hi.astype(jnp.float32)).astype(jnp.bfloat16)
    outs = []
    for s in range(x.shape[1] // LANES):
        cols = slice(s * LANES, (s + 1) * LANES)
        outs.append(jnp.dot(hi[:, cols], ones, preferred_element_type=jnp.float32)
                    + jnp.dot(lo[:, cols], ones, preferred_element_type=jnp.float32))
    return jnp.concatenate(outs, axis=1)


def _dot_split(a, w_hi, w_lo):
    a_hi = a.astype(jnp.bfloat16)
    a_lo = (a - a_hi.astype(jnp.float32)).astype(jnp.bfloat16)
    return (jnp.dot(a_hi, w_hi, preferred_element_type=jnp.float32)
            + jnp.dot(a_lo, w_hi, preferred_element_type=jnp.float32)
            + jnp.dot(a_hi, w_lo, preferred_element_type=jnp.float32))


def _token_shift(x, prev_row, next_row, mu_p, mu_n):
    rows = x.shape[0]
    row = lax.broadcasted_iota(jnp.int32, x.shape, 0)
    xp = jnp.where(row == 0, prev_row, pltpu.roll(x, 1, axis=0))
    xn = jnp.where(row == rows - 1, next_row, pltpu.roll(x, rows - 1, axis=0))
    return x + mu_p * (xp - x) + mu_n * (xn - x)


PREP_OUT_NAMES = ('at0', 'bt0', 'kt0', 'rt0', 'gam0', 'at1', 'bt1', 'kt1', 'rt1', 'gam1', 'v', 'g', 'bonus')


def _block_cumprod(w, pos, reverse):
    rows = w.shape[0]
    x = w
    s = 1
    while s < SCAN_SUB:
        if reverse:
            x = x * jnp.where(pos < SCAN_SUB - s, pltpu.roll(x, rows - s, axis=0), 1.0)
        else:
            x = x * jnp.where(pos >= s, pltpu.roll(x, s, axis=0), 1.0)
        s *= 2
    return x


def _block_prev(x, pos, reverse):
    rows = x.shape[0]
    if reverse:
        return jnp.where(pos == SCAN_SUB - 1, 1.0, pltpu.roll(x, rows - 1, axis=0))
    return jnp.where(pos == 0, 1.0, pltpu.roll(x, 1, axis=0))


def _rwkv_prep_kernel(x_ref, xp_ref, xn_ref, l_ref, lp_ref, ln_ref, mu_ref, mul_ref, whi_ref, wlo_ref,
                      vec_ref, ones_ref, *out_refs, lay):
    _, has_prev, has_next = _seq_edges(lay)
    edge = lambda ref, r, ok: jnp.where(ok, ref[r:r + 1, :], 0.0)
    x = _token_shift(x_ref[...], edge(xp_ref, SUBLANES - 1, has_prev), edge(xn_ref, 0, has_next),
                     mu_ref[0:1, :], mu_ref[1:2, :])
    lo = _token_shift(l_ref[...], edge(lp_ref, SUBLANES - 1, has_prev), edge(ln_ref, 0, has_next),
                      mul_ref[0:1, :], mul_ref[1:2, :])
    r = x[:, 0:W_A]
    k = x[:, W_A:2 * W_A]
    v = x[:, 2 * W_A:3 * W_A]
    col = lax.broadcasted_iota(jnp.int32, lo.shape, 1)
    act = jnp.where(col < D_DECAY_LORA, jnp.tanh(lo),
                    jnp.where(col < D_DECAY_LORA + D_AAA_LORA, lo, jax.nn.sigmoid(lo)))
    ones = ones_ref[...]
    pos = lax.broadcasted_iota(jnp.int32, r.shape, 0) % SCAN_SUB
    kk = k * vec_ref[0:1, :]
    kk = kk * lax.rsqrt(_head_sum(kk * kk, ones) + 1e-12)
    outs = {'v': v, 'g': _dot_split(act, whi_ref[4], wlo_ref[4])}
    bonus = None
    for d in range(2):
        w = jnp.exp(-DECAY_SCALE * jax.nn.sigmoid(vec_ref[3 + d:4 + d, :] + _dot_split(act, whi_ref[d], wlo_ref[d])))
        a = jax.nn.sigmoid(vec_ref[5 + d:6 + d, :] + _dot_split(act, whi_ref[2 + d], wlo_ref[2 + d]))
        kd = k * (1 + (a - 1) * vec_ref[1:2, :])
        term = _head_sum(r * kd * vec_ref[2:3, :], ones) * v
        bonus = term if bonus is None else bonus + term
        gam = _block_cumprod(w, pos, reverse=(d == 1))
        inv = 1.0 / gam
        outs['at%d' % d] = -kk * _block_prev(gam, pos, reverse=(d == 1))
        outs['bt%d' % d] = kk * a * inv
        outs['kt%d' % d] = kd * inv
        outs['rt%d' % d] = r * gam
        outs['gam%d' % d] = gam
    outs['bonus'] = bonus
    for ref, name in zip(out_refs, PREP_OUT_NAMES):
        ref[...] = outs[name]


def _rwkv_prep(x_rkv, x_lora, lora_col, P, lay):
    n = x_rkv.shape[0]
    o = 3 * W_A
    pr, nr = _halo_specs(n, SUBLANES, o, 0)
    plr, nlr = _halo_specs(n, SUBLANES, LORA_PAD, lora_col)
    full = lambda a: pl.BlockSpec(a.shape, lambda i: (0,) * a.ndim)
    mu = jnp.stack([P['mu_prev'][0:o], P['mu_next'][0:o]])
    pad = LORA_PAD - LORA_W
    mul = jnp.stack([jnp.pad(P['mu_prev'][o:], (0, pad)), jnp.pad(P['mu_next'][o:], (0, pad))])
    z = lambda rows: jnp.zeros((rows, W_A), jnp.float32)
    wcat = jnp.stack([
        jnp.concatenate([P['w_up'][0], z(LORA_PAD - D_DECAY_LORA)]),
        jnp.concatenate([P['w_up'][1], z(LORA_PAD - D_DECAY_LORA)]),
        jnp.concatenate([z(D_DECAY_LORA), P['a_up'][0], z(LORA_PAD - D_DECAY_LORA - D_AAA_LORA)]),
        jnp.concatenate([z(D_DECAY_LORA), P['a_up'][1], z(LORA_PAD - D_DECAY_LORA - D_AAA_LORA)]),
        jnp.concatenate([z(D_DECAY_LORA + D_AAA_LORA), P['g_up'], z(pad)]),
    ])
    w_hi = wcat.astype(jnp.bfloat16)
    w_lo = (wcat - w_hi.astype(jnp.float32)).astype(jnp.bfloat16)
    vec = jnp.stack([P['k_k'], P['k_a'], P['r_k'].reshape(W_A), P['w0'][0], P['w0'][1], P['a0'][0], P['a0'][1],
                     jnp.zeros((W_A,), jnp.float32)])
    ones = _head_ones()
    row_out = pl.BlockSpec((ROW_TILE, W_A), lambda i: (i, 0))
    outs = pl.pallas_call(
        functools.partial(_rwkv_prep_kernel, lay=lay),
        grid=(lay.n_tiles,),
        in_specs=[pl.BlockSpec((ROW_TILE, o), lambda i: (i, 0)), pr, nr,
                  pl.BlockSpec((ROW_TILE, LORA_PAD), lambda i: (i, lora_col)), plr, nlr,
                  full(mu), full(mul), full(w_hi), full(w_lo), full(vec), full(ones)],
        out_specs=[row_out] * N_PREP_OUT,
        out_shape=[jax.ShapeDtypeStruct((n, W_A), jnp.float32)] * N_PREP_OUT,
        name="rwkv_prep",
        compiler_params=_cparams(),
    )(x_rkv, x_rkv, x_rkv, x_lora, x_lora, x_lora, mu, mul, w_hi, w_lo, vec, ones)
    return dict(zip(PREP_OUT_NAMES, outs))


def _rwkv_post_kernel(yfc_ref, ybc_ref, yfl_ref, ybl_ref, bonus_ref, g_ref, vec_ref, ones_ref, o_ref, *, lay):
    ones = ones_ref[...]
    is_ctx = pl.program_id(0) < lay.ctx_tiles
    y = jnp.where(is_ctx, yfc_ref[...] + ybc_ref[...], yfl_ref[...] + ybl_ref[...])
    yc = y - _head_sum(y, ones) * (1.0 / HEAD_A)
    var = _head_sum(yc * yc, ones) * (1.0 / HEAD_A)
    y = yc * lax.rsqrt(var + GN_EPS)
    y = y * vec_ref[0:1, :] + vec_ref[1:2, :] + bonus_ref[...]
    o_ref[...] = (y * g_ref[...]).astype(o_ref.dtype)


def _rwkv_post(y_ctx, y_lat, bonus, g, ln_g, ln_b, lay):
    n = bonus.shape[0]
    row = pl.BlockSpec((ROW_TILE, W_A), lambda i: (i, 0))
    ctx_row = pl.BlockSpec((ROW_TILE, W_A), lambda i: (jnp.minimum(i, lay.ctx_tiles - 1), 0))
    lat_row = pl.BlockSpec((ROW_TILE, W_A), lambda i: (jnp.maximum(i - lay.ctx_tiles, 0), 0))
    vec = jnp.stack([ln_g, ln_b] + [jnp.zeros_like(ln_g)] * 6)
    ones = _head_ones()
    full = lambda a: pl.BlockSpec(a.shape, lambda i: (0,) * a.ndim)
    return pl.pallas_call(
        functools.partial(_rwkv_post_kernel, lay=lay),
        grid=(lay.n_tiles,),
        in_specs=[ctx_row, ctx_row, lat_row, lat_row, row, row, full(vec), full(ones)],
        out_specs=row,
        out_shape=jax.ShapeDtypeStruct((n, W_A), jnp.bfloat16),
        name="rwkv_post",
        compiler_params=_cparams(),
    )(*y_ctx, *y_lat, bonus, g, vec, ones)


def _grid_transpose(z, rows, cols):
    b, t, ch = z.shape
    return z.reshape(b, rows, cols, ch).transpose(0, 2, 1, 3).reshape(b, t, ch)


def _rwkv_group(f, P, lay, dims, s_ctx0, s_lat0, transposed):
    bc, tc, bl, tl = dims
    nc = bc * tc
    rows = tl // GRID_W
    if transposed:
        def permuted(cols):
            lat = _grid_transpose(f[nc:, cols].reshape(bl, tl, -1), rows, GRID_W)
            return jnp.concatenate([f[:nc, cols], lat.reshape(bl * tl, -1)], axis=0)
        x_rkv = permuted(slice(0, 3 * W_A))
        x_lora = permuted(slice(LORA_COL * W_A, LORA_COL * W_A + LORA_PAD))
        lora_col = 0
    else:
        x_rkv, x_lora, lora_col = f, f, LORA_COL * W_A // LORA_PAD
    ops = _rwkv_prep(x_rkv, x_lora, lora_col, P, lay)
    yf_c, yb_c, s_fin = _rwkv_scan(ops, s_ctx0, 0, bc, tc)
    yf_l, yb_l, _ = _rwkv_scan(ops, s_lat0, nc, bl, tl)
    out = _rwkv_post((yf_c, yb_c), (yf_l, yb_l), ops['bonus'], ops['g'], P['rwkv_ln_g'], P['rwkv_ln_b'], lay)
    if transposed:
        lat = _grid_transpose(out[nc:].reshape(bl, tl, W_A), GRID_W, rows)
        out = jnp.concatenate([out[:nc], lat.reshape(bl * tl, W_A)], axis=0)
    return out, s_fin


def _modulation_kernel(c_ref, w_ref, b_ref, o_ref):
    c = c_ref[...]
    w = w_ref[...]
    w_hi = w.astype(jnp.bfloat16)
    w_lo = (w - w_hi.astype(jnp.float32)).astype(jnp.bfloat16)
    o_ref[...] = _dot_split(c * jax.nn.sigmoid(c), w_hi, w_lo) + b_ref[...]


def _modulation(cvec, w_ada, b_ada):
    n_c, d = cvec.shape
    n_layers, _, n_out = w_ada.shape
    c_pad = jnp.pad(cvec, ((0, SUBLANES - n_c), (0, 0)))
    m = pl.pallas_call(
        _modulation_kernel,
        grid=(n_layers, n_out // ADA_TN),
        in_specs=[pl.BlockSpec((SUBLANES, d), lambda l, j: (0, 0)),
                  pl.BlockSpec((None, d, ADA_TN), lambda l, j: (l, 0, j)),
                  pl.BlockSpec((None, 1, ADA_TN), lambda l, j: (l, 0, j))],
        out_specs=pl.BlockSpec((None, SUBLANES, ADA_TN), lambda l, j: (l, 0, j)),
        out_shape=jax.ShapeDtypeStruct((n_layers, SUBLANES, n_out), jnp.float32),
        name="modulation",
        compiler_params=pltpu.CompilerParams(dimension_semantics=("parallel", "parallel"),
                                             vmem_limit_bytes=VMEM_LIMIT_BYTES),
    )(c_pad, w_ada, b_ada.reshape(n_layers, 1, n_out))
    return [m[l, :n_c].reshape(n_c, N_MOD, D_MODEL) for l in range(n_layers)]


def _pack_w_in(w_in):
    o = 3 * W_A
    parts = [w_in[..., 0:o], w_in[..., COLS_A:], w_in[..., o:COLS_A]]
    parts = [p.astype(jnp.bfloat16) for p in parts]
    parts.append(jnp.zeros(w_in.shape[:-1] + (IN_PAD - IN_COLS,), jnp.bfloat16))
    return jnp.concatenate(parts, axis=-1)


def kernel(x_prompt, x_sample, c, state_rwkv, c_ctx, w_ada, b_ada, w_in, mu_prev, mu_next, rwkv_g_up, rwkv_w0, rwkv_w_up, rwkv_a0, rwkv_a_up, rwkv_k_k, rwkv_k_a, rwkv_r_k, rwkv_ln_g, rwkv_ln_b, pool_w, pool_b, pool_scale, conf_conv_w, conf_conv_b, conf_ln_g, conf_ln_b, gmlp_ln_g, gmlp_ln_b, gmlp_ws, gmlp_bs, w_out, ln1_g, ln1_b, ffn_w_up, ffn_conv_w, ffn_conv_b, ffn_w_down, ln2_g, ln2_b):
    bc, tc, d = x_prompt.shape
    bl, tl, _ = x_sample.shape
    nc = bc * tc
    lay = TokenLayout(bc, tc, bl, tl)
    s_ctx0 = jnp.zeros((bc, 2, H_A, HEAD_A, HEAD_A), jnp.float32)
    cvec = jnp.concatenate([c_ctx[None, :], c], axis=0)
    mods = _modulation(cvec, w_ada, b_ada)
    w_in_p = _pack_w_in(w_in)
    w_out_b = w_out.astype(jnp.bfloat16)
    w_up_b = ffn_w_up.astype(jnp.bfloat16)
    w_dn_b = jnp.concatenate([ffn_w_down.astype(jnp.bfloat16),
                              jnp.zeros((DEPTH, FF_PAD - D_FF, d), jnp.bfloat16)], axis=1)
    x = jnp.concatenate([x_prompt.reshape(nc, d), x_sample.reshape(bl * tl, d)], axis=0)
    h = _modulate(x, jnp.stack([mods[0][:, 1], mods[0][:, 0]], axis=1), lay)
    new_states = []
    for l in range(DEPTH):
        P = {
            'mu_prev': mu_prev[l], 'mu_next': mu_next[l], 'g_up': rwkv_g_up[l],
            'w0': rwkv_w0[l], 'w_up': rwkv_w_up[l], 'a0': rwkv_a0[l], 'a_up': rwkv_a_up[l],
            'k_k': rwkv_k_k[l], 'k_a': rwkv_k_a[l], 'r_k': rwkv_r_k[l],
            'rwkv_ln_g': rwkv_ln_g[l], 'rwkv_ln_b': rwkv_ln_b[l],
        }

        m = mods[l]

        f = _matmul(h, w_in_p, l, jnp.float32, 2048, 512, D_MODEL)
        y_a, s_fin = _rwkv_group(f, P, lay, (bc, tc, bl, tl), s_ctx0, state_rwkv[:, l], l % 2 == 1)
        new_states.append(s_fin)
        y_b = _pool_group(f, pool_w[l], pool_b[l], pool_scale[l], lay)
        y_c = _conformer_group(f, conf_conv_w[l], conf_conv_b[l], conf_ln_g[l], conf_ln_b[l], lay)
        y_d = _gmlp_group(f, gmlp_ln_g[l], gmlp_ln_b[l], gmlp_ws[l], gmlp_bs[l], lay)
        y = _matmul4([y_a, y_b, y_c, y_d], w_out_b, l, 1024, 1024)
        x, h = _ln_residual(y, x, jnp.stack([m[:, 2], m[:, 4], m[:, 3]], axis=1), ln1_g[l], ln1_b[l], lay, False)

        z = _matmul(h, w_up_b, l, jnp.bfloat16, 2048, 512, D_MODEL)
        g = _ffn_gate(z, ffn_conv_w[l], ffn_conv_b[l], lay)
        y = _matmul(g, w_dn_b, l, jnp.float32, 1024, 1024, FF_PAD // 4)
        if l + 1 < DEPTH:
            mn = mods[l + 1]
            x, h = _ln_residual(y, x, jnp.stack([m[:, 5], mn[:, 1], mn[:, 0]], axis=1), ln2_g[l], ln2_b[l], lay, False)
        else:
            x_ctx, x_lat = _ln_residual(y, x, jnp.stack([m[:, 5], m[:, 5], m[:, 5]], axis=1), ln2_g[l], ln2_b[l],
                                        lay, True)

    new_state_rwkv = jnp.stack(new_states, axis=1).astype(x_prompt.dtype)
    return (x_ctx.reshape(bc, tc, d), x_lat.reshape(bl, tl, d), new_state_rwkv)
```

```python
import functools
import math

import numpy as np
import jax
import jax.numpy as jnp
from jax import lax
from jax.experimental import pallas as pl
from jax.experimental.pallas import tpu as pltpu

D_MODEL = 4096
DEPTH = 2
GRID_W = 64
W_A = D_MODEL // 4
W_B = D_MODEL // 4
W_C = D_MODEL // 4
W_D = D_MODEL - W_A - W_B - W_C
HEAD_A = 64
H_A = W_A // HEAD_A
D_DECAY_LORA = max(32, int(round(W_A ** 0.5 * 1.8 / 32)) * 32)
D_AAA_LORA = max(32, int(round(W_A ** 0.5 * 1.8 / 32)) * 32)
D_GATE_LORA = max(32, int(round(W_A ** 0.8 * 0.6 / 32)) * 32)
COLS_A = 3 * W_A + D_DECAY_LORA + D_AAA_LORA + D_GATE_LORA
DECAY_SCALE = math.exp(-0.5)
GN_EPS = 64e-5
POOL_WINDOWS = (2, 4, 8, 16)
POOL_GROUP = W_B // len(POOL_WINDOWS)
CONV_K = 31
CHUNK = 128
H_D = 8
HEAD_D = W_D // H_D
IN_COLS = COLS_A + W_B + 2 * W_C + 2 * W_D
D_FF = ((8 * D_MODEL // 3 + 255) // 256) * 256
FFN_K = 3
ALPHA = (2.0 * DEPTH) ** 0.25
LN_EPS = 1e-5
N_MOD = 6
IN_PAD = ((IN_COLS + 511) // 512) * 512
FF_PAD = ((D_FF + 1023) // 1024) * 1024

LANES = 128
SUBLANES = 8
VMEM_LIMIT_BYTES = 56 * 1024 * 1024

SCAN_TB = 256
SCAN_SUB = 32
SCAN_PARTS = 3
N_KEYVEC = 5
N_STEPVEC = 4
SCAN_TRIP = 4

ROW_TILE = 256
FF_CHUNK = 256
ADA_TN = 1024
CONV_HALO = 16
POOL_HALO = 8
CONV_ROWS = 32
COL_B, COL_CA, COL_CG, COL_DU, COL_DV, LORA_COL = 3, 4, 5, 6, 7, 8
LORA_W = D_DECAY_LORA + D_AAA_LORA + D_GATE_LORA
LORA_PAD = IN_PAD - LORA_COL * W_A
N_PREP_OUT = 13


def _mm_kernel(x_ref, w_ref, o_ref, *acc, nk):
    part = jnp.dot(x_ref[...], w_ref[...], preferred_element_type=jnp.float32)
    if nk == 1:
        o_ref[...] = part.astype(o_ref.dtype)
    else:
        k = pl.program_id(2)
        acc_ref = acc[0]

        @pl.when(k == 0)
        def _():
            acc_ref[...] = part

        @pl.when(k > 0)
        def _():
            acc_ref[...] += part

        @pl.when(k == nk - 1)
        def _():
            o_ref[...] = acc_ref[...].astype(o_ref.dtype)


def _matmul(x, w_all, layer, out_dtype, tm, tn, tk):
    m, kdim = x.shape
    _, _, n = w_all.shape
    assert m % tm == 0 and n % tn == 0 and kdim % tk == 0
    nk = kdim // tk
    return pl.pallas_call(
        functools.partial(_mm_kernel, nk=nk),
        grid=(m // tm, n // tn, nk),
        in_specs=[
            pl.BlockSpec((tm, tk), lambda i, j, k: (i, k)),
            pl.BlockSpec((None, tk, tn), lambda i, j, k: (layer, k, j)),
        ],
        out_specs=pl.BlockSpec((tm, tn), lambda i, j, k: (i, j)),
        out_shape=jax.ShapeDtypeStruct((m, n), out_dtype),
        name="dense_mm",
        scratch_shapes=[pltpu.VMEM((tm, tn), jnp.float32)] if nk > 1 else [],
        compiler_params=pltpu.CompilerParams(
            dimension_semantics=("parallel", "parallel", "arbitrary"),
            vmem_limit_bytes=VMEM_LIMIT_BYTES,
        ),
    )(x, w_all)


def _sel_table():
    c = np.arange(2 * LANES)[:, None]
    n = np.arange(2 * LANES)[None, :]
    e = np.zeros((SCAN_SUB // 2, 2 * LANES, 2 * LANES), np.float32)
    for p in range(SCAN_SUB // 2):
        t = 2 * p + n // LANES
        e[p] = (((c % LANES) < SCAN_PARTS * SCAN_SUB) & ((c % SCAN_SUB) == t)
                & ((c // LANES) == ((n % LANES) // HEAD_A)))
    return jnp.asarray(e, jnp.bfloat16)


def _split3_rows(x):
    hi = x.astype(jnp.bfloat16).astype(jnp.float32)
    r1 = x - hi
    mid = r1.astype(jnp.bfloat16).astype(jnp.float32)
    lo = r1 - mid
    return jnp.concatenate([hi, mid, lo, jnp.zeros_like(x)], axis=0)


def _scan_kernel(*refs, nblk):
    key_refs = (refs[0:5], refs[6:11])
    v_refs = (refs[5], refs[11])
    e_ref, s0_ref = refs[12], refs[13]
    y_refs = (refs[14], refs[15])
    sfin_ref, st_ref, lhs_ref = refs[16], refs[17], refs[18]
    tb = pl.program_id(2)

    @pl.when(tb == 0)
    def _():
        st_ref[...] = s0_ref[0, :, 0]

    lane = lax.broadcasted_iota(jnp.int32, (SCAN_SUB, LANES), 1)
    first_head = lane < HEAD_A

    half = N_KEYVEC * HEAD_A
    npairs = SCAN_SUB // 2

    def build_lhs(slot, offs):
        for c in range(2):
            xs = [ref[pl.ds(offs[c], SCAN_SUB), :] for ref in key_refs[c]]
            for pi, (i1, i2) in enumerate(((0, 1), (2, 3), (4, None))):
                x1 = xs[i1]
                xr1 = pltpu.roll(x1, HEAD_A, axis=1)
                if i2 is None:
                    comb_a, comb_b = x1, xr1
                else:
                    x2 = xs[i2]
                    xr2 = pltpu.roll(x2, HEAD_A, axis=1)
                    comb_a = jnp.where(first_head, x1, xr2)
                    comb_b = jnp.where(first_head, xr1, x2)
                mt_a = _split3_rows(comb_a).T.astype(jnp.bfloat16)
                mt_b = _split3_rows(comb_b).T.astype(jnp.bfloat16)
                r0 = (c * N_KEYVEC + 2 * pi) * HEAD_A
                lhs_ref[slot, r0:r0 + HEAD_A, 0:LANES] = mt_a[0:HEAD_A]
                lhs_ref[slot, r0:r0 + HEAD_A, LANES:2 * LANES] = mt_b[0:HEAD_A]
                if i2 is not None:
                    lhs_ref[slot, r0 + HEAD_A:r0 + 2 * HEAD_A, 0:LANES] = mt_a[HEAD_A:2 * HEAD_A]
                    lhs_ref[slot, r0 + HEAD_A:r0 + 2 * HEAD_A, LANES:2 * LANES] = mt_b[HEAD_A:2 * HEAD_A]

    def sub_offsets(q):
        return (pl.multiple_of(q * SCAN_SUB, SCAN_SUB),
                pl.multiple_of(SCAN_TB - (q + 1) * SCAN_SUB, SCAN_SUB))

    def run_sub(slot, offs, hs):
        vs = [v_refs[c][pl.ds(offs[c], SCAN_SUB), :] for c in range(2)]
        ys = ([], [])
        gs = list(hs)
        step_rows = N_STEPVEC * HEAD_A
        for p in range(npairs):
            tiles = (jnp.dot(lhs_ref[slot, 0:step_rows, :], e_ref[p], preferred_element_type=jnp.float32),
                     jnp.dot(lhs_ref[slot, half:half + step_rows, :], e_ref[npairs - 1 - p],
                             preferred_element_type=jnp.float32))
            for s in range(2):
                u = 2 * p + s
                for c in range(2):
                    col = s if c == 0 else 1 - s
                    row = u if c == 0 else SCAN_SUB - 1 - u
                    a_t, b_t, k_t, r_t = (
                        tiles[c][k * HEAD_A:(k + 1) * HEAD_A, col * LANES:(col + 1) * LANES]
                        for k in range(N_STEPVEC))
                    g = gs[c]
                    sa = jnp.sum(g * a_t, axis=0, keepdims=True)
                    g = g + sa * b_t + vs[c][row:row + 1, :] * k_t
                    ys[c].append(jnp.sum(g * r_t, axis=0, keepdims=True))
                    gs[c] = g
        y_refs[0][pl.ds(offs[0], SCAN_SUB), :] = jnp.concatenate(ys[0], axis=0)
        y_refs[1][pl.ds(offs[1], SCAN_SUB), :] = jnp.concatenate(ys[1][::-1], axis=0)
        gam_f = jnp.dot(lhs_ref[slot, step_rows:half, :], e_ref[npairs - 1],
                        preferred_element_type=jnp.float32)[:, LANES:2 * LANES]
        gam_b = jnp.dot(lhs_ref[slot, half + step_rows:2 * half, :], e_ref[0],
                        preferred_element_type=jnp.float32)[:, 0:LANES]
        return [gs[0] * gam_f, gs[1] * gam_b]

    def trip(g, carry):
        hs = list(carry)
        offs = [sub_offsets(g * SCAN_TRIP + k) for k in range(SCAN_TRIP)]
        for k in range(SCAN_TRIP):
            build_lhs(k, offs[k])
        for k in range(SCAN_TRIP):
            hs = run_sub(k, offs[k], hs)
        return tuple(hs)

    hf, hb = lax.fori_loop(0, SCAN_TB // (SCAN_SUB * SCAN_TRIP), trip, (st_ref[0], st_ref[1]))
    st_ref[0] = hf
    st_ref[1] = hb

    @pl.when(tb == nblk - 1)
    def _():
        sfin_ref[0, 0, 0] = hf
        sfin_ref[0, 1, 0] = hb


def _rwkv_scan(ops, s0, row0, bsz, t_len):
    npair = H_A // 2
    nblk = t_len // SCAN_TB
    blk0 = row0 // SCAN_TB
    s0p = s0.reshape(bsz, 2, npair, 2, HEAD_A, HEAD_A).transpose(0, 1, 2, 5, 3, 4)
    s0p = s0p.reshape(bsz, 2, npair, HEAD_A, 2 * HEAD_A)
    ins = []
    for d in range(2):
        ins += [ops[n + str(d)] for n in ('at', 'bt', 'kt', 'rt', 'gam')] + [ops['v']]
    blk = (SCAN_TB, LANES)
    fwd_in = pl.BlockSpec(blk, lambda b, p, t: (blk0 + b * nblk + t, p))
    bwd_in = pl.BlockSpec(blk, lambda b, p, t: (blk0 + b * nblk + nblk - 1 - t, p))
    fwd_out = pl.BlockSpec(blk, lambda b, p, t: (b * nblk + t, p))
    bwd_out = pl.BlockSpec(blk, lambda b, p, t: (b * nblk + nblk - 1 - t, p))
    st_spec = pl.BlockSpec((1, 2, 1, HEAD_A, LANES), lambda b, p, t: (b, 0, p, 0, 0))
    e_tab = _sel_table()
    in_specs = ([fwd_in] * 6 + [bwd_in] * 6
                + [pl.BlockSpec(e_tab.shape, lambda b, p, t: (0, 0, 0)), st_spec])
    out_shape = [
        jax.ShapeDtypeStruct((bsz * t_len, W_A), jnp.float32),
        jax.ShapeDtypeStruct((bsz * t_len, W_A), jnp.float32),
        jax.ShapeDtypeStruct((bsz, 2, npair, HEAD_A, LANES), jnp.float32),
    ]
    yf, yb, sfin = pl.pallas_call(
        functools.partial(_scan_kernel, nblk=nblk),
        grid=(bsz, npair, nblk),
        in_specs=in_specs,
        out_specs=[fwd_out, bwd_out, st_spec],
        out_shape=out_shape,
        name="rwkv_scan",
        scratch_shapes=[pltpu.VMEM((2, HEAD_A, LANES), jnp.float32),
                        pltpu.VMEM((SCAN_TRIP, 2 * N_KEYVEC * HEAD_A, 2 * LANES), jnp.bfloat16)],
        compiler_params=pltpu.CompilerParams(
            dimension_semantics=("parallel", "parallel", "arbitrary"),
            vmem_limit_bytes=VMEM_LIMIT_BYTES,
        ),
    )(*ins, e_tab, s0p)
    sfin = sfin.reshape(bsz, 2, npair, HEAD_A, 2, HEAD_A).transpose(0, 1, 2, 4, 5, 3)
    return yf, yb, sfin.reshape(bsz, 2, H_A, HEAD_A, HEAD_A)


class TokenLayout:
    def __init__(self, bc, tc, bl, tl):
        assert tc % ROW_TILE == 0 and tl % ROW_TILE == 0
        self.n_rows = bc * tc + bl * tl
        self.n_tiles = self.n_rows // ROW_TILE
        self.ctx_tiles = bc * tc // ROW_TILE
        self.ctx_seq_tiles = tc // ROW_TILE
        self.lat_seq_tiles = tl // ROW_TILE

    def group(self, i):
        return jnp.where(i < self.ctx_tiles, 0, 1 + (i - self.ctx_tiles) // self.lat_seq_tiles)

    def pos_in_seq(self, i):
        return jnp.where(i < self.ctx_tiles, i % self.ctx_seq_tiles, (i - self.ctx_tiles) % self.lat_seq_tiles)

    def seq_tiles(self, i):
        return jnp.where(i < self.ctx_tiles, self.ctx_seq_tiles, self.lat_seq_tiles)


def _modulate_kernel(x_ref, m_ref, h_ref):
    x = x_ref[...]
    h_ref[...] = (x * (1 + m_ref[0, 0:1, :]) + m_ref[0, 1:2, :]).astype(h_ref.dtype)


def _modulate(x, m2, lay):
    n, d = x.shape
    return pl.pallas_call(
        _modulate_kernel,
        grid=(lay.n_tiles,),
        in_specs=[pl.BlockSpec((ROW_TILE, d), lambda i: (i, 0)),
                  pl.BlockSpec((1, 2, d), lambda i: (lay.group(i), 0, 0))],
        out_specs=pl.BlockSpec((ROW_TILE, d), lambda i: (i, 0)),
        out_shape=jax.ShapeDtypeStruct((n, d), jnp.bfloat16),
        name="modulate",
        compiler_params=pltpu.CompilerParams(dimension_semantics=("parallel",),
                                             vmem_limit_bytes=VMEM_LIMIT_BYTES),
    )(x, m2)


def _ln_residual_kernel(y_ref, x_ref, m_ref, g_ref, b_ref, *o_refs, lay, split):
    v = ALPHA * x_ref[...] + m_ref[0, 0:1, :] * y_ref[...]
    vc = v - jnp.mean(v, -1, keepdims=True)
    var = jnp.mean(vc * vc, -1, keepdims=True)
    xn = (vc * lax.rsqrt(var + LN_EPS)) * g_ref[...] + b_ref[...]
    if split:
        is_ctx = pl.program_id(0) < lay.ctx_tiles

        @pl.when(is_ctx)
        def _():
            o_refs[0][...] = xn

        @pl.when(jnp.logical_not(is_ctx))
        def _():
            o_refs[1][...] = xn
    else:
        o_refs[0][...] = xn
        o_refs[1][...] = (xn * (1 + m_ref[0, 1:2, :]) + m_ref[0, 2:3, :]).astype(jnp.bfloat16)


def _ln_residual(y, x, m3, g, b, lay, split):
    n, d = x.shape
    row = pl.BlockSpec((ROW_TILE, d), lambda i: (i, 0))
    vec = pl.BlockSpec((1, d), lambda i: (0, 0))
    if split:
        n_ctx = lay.ctx_tiles * ROW_TILE
        out_shape = [jax.ShapeDtypeStruct((n_ctx, d), jnp.float32),
                     jax.ShapeDtypeStruct((n - n_ctx, d), jnp.float32)]
        out_specs = [pl.BlockSpec((ROW_TILE, d), lambda i: (jnp.minimum(i, lay.ctx_tiles - 1), 0)),
                     pl.BlockSpec((ROW_TILE, d), lambda i: (jnp.maximum(i - lay.ctx_tiles, 0), 0))]
    else:
        out_shape = [jax.ShapeDtypeStruct((n, d), jnp.float32), jax.ShapeDtypeStruct((n, d), jnp.bfloat16)]
        out_specs = [row, row]
    return pl.pallas_call(
        functools.partial(_ln_residual_kernel, lay=lay, split=split),
        grid=(lay.n_tiles,),
        in_specs=[row, row, pl.BlockSpec((1, 3, d), lambda i: (lay.group(i), 0, 0)), vec, vec],
        out_specs=out_specs,
        out_shape=out_shape,
        name="ln_residual",
        compiler_params=pltpu.CompilerParams(dimension_semantics=("arbitrary",),
                                             vmem_limit_bytes=VMEM_LIMIT_BYTES),
    )(y, x, m3, g.reshape(1, d), b.reshape(1, d))


def _ffn_gate_kernel(zc_ref, zp_ref, zn_ref, w_ref, b_ref, o_ref, *, lay):
    i = pl.program_id(0)
    pos = lay.pos_in_seq(i)
    has_prev = pos > 0
    has_next = pos < lay.seq_tiles(i) - 1
    row = lax.broadcasted_iota(jnp.int32, (ROW_TILE, FF_CHUNK), 0)
    n_chunks = D_FF // FF_CHUNK

    def conv(col):
        sl = pl.ds(col, FF_CHUNK)
        z = zc_ref[:, sl].astype(jnp.float32)
        zprev_row = jnp.where(has_prev, zp_ref[SUBLANES - 1:SUBLANES, sl].astype(jnp.float32), 0.0)
        znext_row = jnp.where(has_next, zn_ref[0:1, sl].astype(jnp.float32), 0.0)
        zprev = jnp.where(row == 0, zprev_row, pltpu.roll(z, 1, axis=0))
        znext = jnp.where(row == ROW_TILE - 1, znext_row, pltpu.roll(z, ROW_TILE - 1, axis=0))
        return zprev * w_ref[0:1, sl] + b_ref[0:1, sl] + z * w_ref[1:2, sl] + znext * w_ref[2:3, sl]

    def body(j, carry):
        ca = pl.multiple_of(j * FF_CHUNK, FF_CHUNK)
        cb = pl.multiple_of(D_FF + j * FF_CHUNK, FF_CHUNK)
        a = conv(ca)
        bv = conv(cb)
        h = 0.5 * a
        o_ref[:, pl.ds(ca, FF_CHUNK)] = ((h + h * jnp.tanh(h)) * bv).astype(o_ref.dtype)
        return carry

    lax.fori_loop(0, n_chunks, body, 0)
    o_ref[:, D_FF:FF_PAD] = jnp.zeros((ROW_TILE, FF_PAD - D_FF), o_ref.dtype)


def _ffn_gate(z, conv_w, conv_b, lay):
    n, c2 = z.shape
    rt8 = ROW_TILE // SUBLANES
    last8 = n // SUBLANES - 1
    return pl.pallas_call(
        functools.partial(_ffn_gate_kernel, lay=lay),
        grid=(lay.n_tiles,),
        in_specs=[pl.BlockSpec((ROW_TILE, c2), lambda i: (i, 0)),
                  pl.BlockSpec((SUBLANES, c2), lambda i: (jnp.maximum(i * rt8 - 1, 0), 0)),
                  pl.BlockSpec((SUBLANES, c2), lambda i: (jnp.minimum((i + 1) * rt8, last8), 0)),
                  pl.BlockSpec((3, c2), lambda i: (0, 0)),
                  pl.BlockSpec((1, c2), lambda i: (0, 0))],
        out_specs=pl.BlockSpec((ROW_TILE, FF_PAD), lambda i: (i, 0)),
        out_shape=jax.ShapeDtypeStruct((n, FF_PAD), jnp.bfloat16),
        name="ffn_gate",
        compiler_params=pltpu.CompilerParams(dimension_semantics=("parallel",),
                                             vmem_limit_bytes=VMEM_LIMIT_BYTES),
    )(z, z, z, conv_w, conv_b.reshape(1, c2))


def _cparams():
    return pltpu.CompilerParams(dimension_semantics=("parallel",), vmem_limit_bytes=VMEM_LIMIT_BYTES)


def _halo_specs(n_rows, halo, width, col):
    per_tile = ROW_TILE // halo
    last = n_rows // halo - 1
    prev = pl.BlockSpec((halo, width), lambda i: (jnp.maximum(i * per_tile - 1, 0), col))
    nxt = pl.BlockSpec((halo, width), lambda i: (jnp.minimum((i + 1) * per_tile, last), col))
    return prev, nxt


def _seq_edges(lay):
    i = pl.program_id(0)
    pos = lay.pos_in_seq(i)
    return pos, pos > 0, pos < lay.seq_tiles(i) - 1


def _row_layer_norm(x, g, b):
    xc = x - jnp.mean(x, -1, keepdims=True)
    var = jnp.mean(xc * xc, -1, keepdims=True)
    return (xc * lax.rsqrt(var + LN_EPS)) * g + b


def _conformer_kernel(a_ref, g_ref, ap_ref, gp_ref, an_ref, gn_ref, w_ref, b_ref, lg_ref, lb_ref,
                      o_ref, hext_ref, y_ref, *, lay):
    _, has_prev, has_next = _seq_edges(lay)
    glu = lambda a, g: a * jax.nn.sigmoid(g)
    hext_ref[0:CONV_HALO] = jnp.where(has_prev, glu(ap_ref[...], gp_ref[...]), 0.0)
    hext_ref[CONV_HALO:CONV_HALO + ROW_TILE] = glu(a_ref[...], g_ref[...])
    hext_ref[CONV_HALO + ROW_TILE:] = jnp.where(has_next, glu(an_ref[...], gn_ref[...]), 0.0)
    first = CONV_HALO - CONV_K // 2
    for rc in range(ROW_TILE // CONV_ROWS):
        base = first + rc * CONV_ROWS
        acc = hext_ref[base:base + CONV_ROWS] * w_ref[0:1] + b_ref[...]
        for j in range(1, CONV_K):
            acc = acc + hext_ref[base + j:base + j + CONV_ROWS] * w_ref[j:j + 1]
        y_ref[rc * CONV_ROWS:(rc + 1) * CONV_ROWS] = acc
    y = _row_layer_norm(y_ref[...], lg_ref[...], lb_ref[...])
    o_ref[...] = (y * jax.nn.sigmoid(y)).astype(o_ref.dtype)


def _conformer_group(f, conv_w, conv_b, ln_g, ln_b, lay):
    n = f.shape[0]
    cur = lambda col: pl.BlockSpec((ROW_TILE, W_C), lambda i: (i, col))
    pa, na = _halo_specs(n, CONV_HALO, W_C, COL_CA)
    pg, ng = _halo_specs(n, CONV_HALO, W_C, COL_CG)
    vec = pl.BlockSpec((1, W_C), lambda i: (0, 0))
    return pl.pallas_call(
        functools.partial(_conformer_kernel, lay=lay),
        grid=(lay.n_tiles,),
        in_specs=[cur(COL_CA), cur(COL_CG), pa, pg, na, ng,
                  pl.BlockSpec((CONV_K, W_C), lambda i: (0, 0)), vec, vec, vec],
        out_specs=pl.BlockSpec((ROW_TILE, W_C), lambda i: (i, 0)),
        out_shape=jax.ShapeDtypeStruct((n, W_C), jnp.bfloat16),
        name="conformer_group",
        scratch_shapes=[pltpu.VMEM((ROW_TILE + 2 * CONV_HALO, W_C), jnp.float32),
                        pltpu.VMEM((ROW_TILE, W_C), jnp.float32)],
        compiler_params=_cparams(),
    )(f, f, f, f, f, f, conv_w, conv_b.reshape(1, W_C), ln_g.reshape(1, W_C), ln_b.reshape(1, W_C))


def _gmlp_kernel(u_ref, v_ref, lg_ref, lb_ref, ws_ref, bias_ref, o_ref):
    v = _row_layer_norm(jax.nn.gelu(v_ref[...]), lg_ref[...], lb_ref[...]).astype(jnp.bfloat16)
    for ch in range(ROW_TILE // CHUNK):
        rows = slice(ch * CHUNK, (ch + 1) * CHUNK)
        for h in range(H_D):
            cols = slice(h * HEAD_D, (h + 1) * HEAD_D)
            s = jnp.dot(ws_ref[h], v[rows, cols], preferred_element_type=jnp.float32) + bias_ref[:, cols]
            o_ref[rows, cols] = (jax.nn.gelu(u_ref[rows, cols]) * s).astype(o_ref.dtype)


def _gmlp_group(f, ln_g, ln_b, ws, bs, lay):
    n = f.shape[0]
    vec = pl.BlockSpec((1, W_D), lambda i: (0, 0))
    bias = jnp.repeat(bs.T, HEAD_D, axis=1)
    return pl.pallas_call(
        _gmlp_kernel,
        grid=(lay.n_tiles,),
        in_specs=[pl.BlockSpec((ROW_TILE, W_D), lambda i: (i, COL_DU)),
                  pl.BlockSpec((ROW_TILE, W_D), lambda i: (i, COL_DV)),
                  vec, vec,
                  pl.BlockSpec((H_D, CHUNK, CHUNK), lambda i: (0, 0, 0)),
                  pl.BlockSpec((CHUNK, W_D), lambda i: (0, 0))],
        out_specs=pl.BlockSpec((ROW_TILE, W_D), lambda i: (i, 0)),
        out_shape=jax.ShapeDtypeStruct((n, W_D), jnp.bfloat16),
        name="gmlp_group",
        compiler_params=_cparams(),
    )(f, f, ln_g.reshape(1, W_D), ln_b.reshape(1, W_D), ws.astype(jnp.bfloat16), bias)


def _pool_kernel(x_ref, xp_ref, xn_ref, w_ref, b_ref, sc_ref, o_ref, hext_ref, *, lay):
    pos, has_prev, has_next = _seq_edges(lay)
    hext_ref[0:POOL_HALO] = jnp.where(has_prev, xp_ref[...], 0.0)
    hext_ref[POOL_HALO:POOL_HALO + ROW_TILE] = x_ref[...]
    hext_ref[POOL_HALO + ROW_TILE:] = jnp.where(has_next, xn_ref[...], 0.0)
    t = pos * ROW_TILE + lax.broadcasted_iota(jnp.int32, (ROW_TILE, POOL_GROUP), 0)
    t_len = lay.seq_tiles(pl.program_id(0)) * ROW_TILE
    for gi, win in enumerate(POOL_WINDOWS):
        cols = slice(gi * POOL_GROUP, (gi + 1) * POOL_GROUP)
        half = win // 2
        ssum = hext_ref[POOL_HALO - half:POOL_HALO - half + ROW_TILE, cols]
        for dlt in range(-half + 1, half):
            ssum = ssum + hext_ref[POOL_HALO + dlt:POOL_HALO + dlt + ROW_TILE, cols]
        cnt = (jnp.minimum(t + half, t_len) - jnp.maximum(t - half, 0)).astype(jnp.float32)
        pooled = (ssum / cnt - x_ref[:, cols]).astype(jnp.bfloat16)
        y = jnp.dot(pooled, w_ref[gi], preferred_element_type=jnp.float32) + b_ref[:, cols]
        o_ref[:, cols] = (y * sc_ref[:, cols]).astype(o_ref.dtype)


def _pool_group(f, pool_w, pool_b, pool_scale, lay):
    n = f.shape[0]
    prev, nxt = _halo_specs(n, POOL_HALO, W_B, COL_B)
    vec = pl.BlockSpec((1, W_B), lambda i: (0, 0))
    return pl.pallas_call(
        functools.partial(_pool_kernel, lay=lay),
        grid=(lay.n_tiles,),
        in_specs=[pl.BlockSpec((ROW_TILE, W_B), lambda i: (i, COL_B)), prev, nxt,
                  pl.BlockSpec(pool_w.shape, lambda i: (0, 0, 0)), vec, vec],
        out_specs=pl.BlockSpec((ROW_TILE, W_B), lambda i: (i, 0)),
        out_shape=jax.ShapeDtypeStruct((n, W_B), jnp.bfloat16),
        name="pool_group",
        scratch_shapes=[pltpu.VMEM((ROW_TILE + 2 * POOL_HALO, W_B), jnp.float32)],
        compiler_params=_cparams(),
    )(f, f, f, pool_w.astype(jnp.bfloat16), pool_b.reshape(1, W_B), pool_scale.reshape(1, W_B))


def _mm4_kernel(y0, y1, y2, y3, w_ref, o_ref):
    wk = w_ref.shape[0] // 4
    acc = jnp.dot(y0[...], w_ref[0:wk, :], preferred_element_type=jnp.float32)
    for k, y in enumerate((y1, y2, y3), start=1):
        acc = acc + jnp.dot(y[...], w_ref[k * wk:(k + 1) * wk, :], preferred_element_type=jnp.float32)
    o_ref[...] = acc


def _matmul4(ys, w_all, layer, tm, tn):
    m = ys[0].shape[0]
    _, kdim, n = w_all.shape
    yspec = pl.BlockSpec((tm, kdim // 4), lambda i, j: (i, 0))
    return pl.pallas_call(
        _mm4_kernel,
        grid=(m // tm, n // tn),
        in_specs=[yspec] * 4 + [pl.BlockSpec((None, kdim, tn), lambda i, j: (layer, 0, j))],
        out_specs=pl.BlockSpec((tm, tn), lambda i, j: (i, j)),
        out_shape=jax.ShapeDtypeStruct((m, n), jnp.float32),
        name="dense_mm4",
        compiler_params=pltpu.CompilerParams(dimension_semantics=("parallel", "parallel"),
                                             vmem_limit_bytes=VMEM_LIMIT_BYTES),
    )(*ys, w_all)


def _head_ones():
    i = np.arange(LANES)
    return jnp.asarray((i[:, None] // HEAD_A) == (i[None, :] // HEAD_A), jnp.bfloat16)


def _head_sum(x, ones):
    hi = x.astype(jnp.bfloat16)
    lo = (x - hi.astype(jnp.float32)).astype(jnp.bfloat16)
    outs = []
    for s in range(x.shape[1] // LANES):
        cols = slice(s * LANES, (s + 1) * LANES)
        outs.append(jnp.dot(hi[:, cols], ones, preferred_element_type=jnp.float32)
                    + jnp.dot(lo[:, cols], ones, preferred_element_type=jnp.float32))
    return jnp.concatenate(outs, axis=1)


def _dot_split(a, w_hi, w_lo):
    a_hi = a.astype(jnp.bfloat16)
    a_lo = (a - a_hi.astype(jnp.float32)).astype(jnp.bfloat16)
    return (jnp.dot(a_hi, w_hi, preferred_element_type=jnp.float32)
            + jnp.dot(a_lo, w_hi, preferred_element_type=jnp.float32)
            + jnp.dot(a_hi, w_lo, preferred_element_type=jnp.float32))


def _token_shift(x, prev_row, next_row, mu_p, mu_n):
    rows = x.shape[0]
    row = lax.broadcasted_iota(jnp.int32, x.shape, 0)
    xp = jnp.where(row == 0, prev_row, pltpu.roll(x, 1, axis=0))
    xn = jnp.where(row == rows - 1, next_row, pltpu.roll(x, rows - 1, axis=0))
    return x + mu_p * (xp - x) + mu_n * (xn - x)


PREP_OUT_NAMES = ('at0', 'bt0', 'kt0', 'rt0', 'gam0', 'at1', 'bt1', 'kt1', 'rt1', 'gam1', 'v', 'g', 'bonus')


def _block_cumprod(w, pos, reverse):
    rows = w.shape[0]
    x = w
    s = 1
    while s < SCAN_SUB:
        if reverse:
            x = x * jnp.where(pos < SCAN_SUB - s, pltpu.roll(x, rows - s, axis=0), 1.0)
        else:
            x = x * jnp.where(pos >= s, pltpu.roll(x, s, axis=0), 1.0)
        s *= 2
    return x


def _block_prev(x, pos, reverse):
    rows = x.shape[0]
    if reverse:
        return jnp.where(pos == SCAN_SUB - 1, 1.0, pltpu.roll(x, rows - 1, axis=0))
    return jnp.where(pos == 0, 1.0, pltpu.roll(x, 1, axis=0))


def _rwkv_prep_kernel(x_ref, xp_ref, xn_ref, l_ref, lp_ref, ln_ref, mu_ref, mul_ref, whi_ref, wlo_ref,
                      vec_ref, ones_ref, *out_refs, lay):
    _, has_prev, has_next = _seq_edges(lay)
    edge = lambda ref, r, ok: jnp.where(ok, ref[r:r + 1, :], 0.0)
    x = _token_shift(x_ref[...], edge(xp_ref, SUBLANES - 1, has_prev), edge(xn_ref, 0, has_next),
                     mu_ref[0:1, :], mu_ref[1:2, :])
    lo = _token_shift(l_ref[...], edge(lp_ref, SUBLANES - 1, has_prev), edge(ln_ref, 0, has_next),
                      mul_ref[0:1, :], mul_ref[1:2, :])
    r = x[:, 0:W_A]
    k = x[:, W_A:2 * W_A]
    v = x[:, 2 * W_A:3 * W_A]
    col = lax.broadcasted_iota(jnp.int32, lo.shape, 1)
    act = jnp.where(col < D_DECAY_LORA, jnp.tanh(lo),
                    jnp.where(col < D_DECAY_LORA + D_AAA_LORA, lo, jax.nn.sigmoid(lo)))
    ones = ones_ref[...]
    pos = lax.broadcasted_iota(jnp.int32, r.shape, 0) % SCAN_SUB
    kk = k * vec_ref[0:1, :]
    kk = kk * lax.rsqrt(_head_sum(kk * kk, ones) + 1e-12)
    outs = {'v': v, 'g': _dot_split(act, whi_ref[4], wlo_ref[4])}
    bonus = None
    for d in range(2):
        w = jnp.exp(-DECAY_SCALE * jax.nn.sigmoid(vec_ref[3 + d:4 + d, :] + _dot_split(act, whi_ref[d], wlo_ref[d])))
        a = jax.nn.sigmoid(vec_ref[5 + d:6 + d, :] + _dot_split(act, whi_ref[2 + d], wlo_ref[2 + d]))
        kd = k * (1 + (a - 1) * vec_ref[1:2, :])
        term = _head_sum(r * kd * vec_ref[2:3, :], ones) * v
        bonus = term if bonus is None else bonus + term
        gam = _block_cumprod(w, pos, reverse=(d == 1))
        inv = 1.0 / gam
        outs['at%d' % d] = -kk * _block_prev(gam, pos, reverse=(d == 1))
        outs['bt%d' % d] = kk * a * inv
        outs['kt%d' % d] = kd * inv
        outs['rt%d' % d] = r * gam
        outs['gam%d' % d] = gam
    outs['bonus'] = bonus
    for ref, name in zip(out_refs, PREP_OUT_NAMES):
        ref[...] = outs[name]


def _rwkv_prep(x_rkv, x_lora, lora_col, P, lay):
    n = x_rkv.shape[0]
    o = 3 * W_A
    pr, nr = _halo_specs(n, SUBLANES, o, 0)
    plr, nlr = _halo_specs(n, SUBLANES, LORA_PAD, lora_col)
    full = lambda a: pl.BlockSpec(a.shape, lambda i: (0,) * a.ndim)
    mu = jnp.stack([P['mu_prev'][0:o], P['mu_next'][0:o]])
    pad = LORA_PAD - LORA_W
    mul = jnp.stack([jnp.pad(P['mu_prev'][o:], (0, pad)), jnp.pad(P['mu_next'][o:], (0, pad))])
    z = lambda rows: jnp.zeros((rows, W_A), jnp.float32)
    wcat = jnp.stack([
        jnp.concatenate([P['w_up'][0], z(LORA_PAD - D_DECAY_LORA)]),
        jnp.concatenate([P['w_up'][1], z(LORA_PAD - D_DECAY_LORA)]),
        jnp.concatenate([z(D_DECAY_LORA), P['a_up'][0], z(LORA_PAD - D_DECAY_LORA - D_AAA_LORA)]),
        jnp.concatenate([z(D_DECAY_LORA), P['a_up'][1], z(LORA_PAD - D_DECAY_LORA - D_AAA_LORA)]),
        jnp.concatenate([z(D_DECAY_LORA + D_AAA_LORA), P['g_up'], z(pad)]),
    ])
    w_hi = wcat.astype(jnp.bfloat16)
    w_lo = (wcat - w_hi.astype(jnp.float32)).astype(jnp.bfloat16)
    vec = jnp.stack([P['k_k'], P['k_a'], P['r_k'].reshape(W_A), P['w0'][0], P['w0'][1], P['a0'][0], P['a0'][1],
                     jnp.zeros((W_A,), jnp.float32)])
    ones = _head_ones()
    row_out = pl.BlockSpec((ROW_TILE, W_A), lambda i: (i, 0))
    outs = pl.pallas_call(
        functools.partial(_rwkv_prep_kernel, lay=lay),
        grid=(lay.n_tiles,),
        in_specs=[pl.BlockSpec((ROW_TILE, o), lambda i: (i, 0)), pr, nr,
                  pl.BlockSpec((ROW_TILE, LORA_PAD), lambda i: (i, lora_col)), plr, nlr,
                  full(mu), full(mul), full(w_hi), full(w_lo), full(vec), full(ones)],
        out_specs=[row_out] * N_PREP_OUT,
        out_shape=[jax.ShapeDtypeStruct((n, W_A), jnp.float32)] * N_PREP_OUT,
        name="rwkv_prep",
        compiler_params=_cparams(),
    )(x_rkv, x_rkv, x_rkv, x_lora, x_lora, x_lora, mu, mul, w_hi, w_lo, vec, ones)
    return dict(zip(PREP_OUT_NAMES, outs))


def _rwkv_post_kernel(yfc_ref, ybc_ref, yfl_ref, ybl_ref, bonus_ref, g_ref, vec_ref, ones_ref, o_ref, *, lay):
    ones = ones_ref[...]
    is_ctx = pl.program_id(0) < lay.ctx_tiles
    y = jnp.where(is_ctx, yfc_ref[...] + ybc_ref[...], yfl_ref[...] + ybl_ref[...])
    yc = y - _head_sum(y, ones) * (1.0 / HEAD_A)
    var = _head_sum(yc * yc, ones) * (1.0 / HEAD_A)
    y = yc * lax.rsqrt(var + GN_EPS)
    y = y * vec_ref[0:1, :] + vec_ref[1:2, :] + bonus_ref[...]
    o_ref[...] = (y * g_ref[...]).astype(o_ref.dtype)


def _rwkv_post(y_ctx, y_lat, bonus, g, ln_g, ln_b, lay):
    n = bonus.shape[0]
    row = pl.BlockSpec((ROW_TILE, W_A), lambda i: (i, 0))
    ctx_row = pl.BlockSpec((ROW_TILE, W_A), lambda i: (jnp.minimum(i, lay.ctx_tiles - 1), 0))
    lat_row = pl.BlockSpec((ROW_TILE, W_A), lambda i: (jnp.maximum(i - lay.ctx_tiles, 0), 0))
    vec = jnp.stack([ln_g, ln_b] + [jnp.zeros_like(ln_g)] * 6)
    ones = _head_ones()
    full = lambda a: pl.BlockSpec(a.shape, lambda i: (0,) * a.ndim)
    return pl.pallas_call(
        functools.partial(_rwkv_post_kernel, lay=lay),
        grid=(lay.n_tiles,),
        in_specs=[ctx_row, ctx_row, lat_row, lat_row, row, row, full(vec), full(ones)],
        out_specs=row,
        out_shape=jax.ShapeDtypeStruct((n, W_A), jnp.bfloat16),
        name="rwkv_post",
        compiler_params=_cparams(),
    )(*y_ctx, *y_lat, bonus, g, vec, ones)


def _grid_transpose(z, rows, cols):
    b, t, ch = z.shape
    return z.reshape(b, rows, cols, ch).transpose(0, 2, 1, 3).reshape(b, t, ch)


def _rwkv_group(f, P, lay, dims, s_ctx0, s_lat0, transposed):
    bc, tc, bl, tl = dims
    nc = bc * tc
    rows = tl // GRID_W
    if transposed:
        def permuted(cols):
            lat = _grid_transpose(f[nc:, cols].reshape(bl, tl, -1), rows, GRID_W)
            return jnp.concatenate([f[:nc, cols], lat.reshape(bl * tl, -1)], axis=0)
        x_rkv = permuted(slice(0, 3 * W_A))
        x_lora = permuted(slice(LORA_COL * W_A, LORA_COL * W_A + LORA_PAD))
        lora_col = 0
    else:
        x_rkv, x_lora, lora_col = f, f, LORA_COL * W_A // LORA_PAD
    ops = _rwkv_prep(x_rkv, x_lora, lora_col, P, lay)
    yf_c, yb_c, s_fin = _rwkv_scan(ops, s_ctx0, 0, bc, tc)
    yf_l, yb_l, _ = _rwkv_scan(ops, s_lat0, nc, bl, tl)
    out = _rwkv_post((yf_c, yb_c), (yf_l, yb_l), ops['bonus'], ops['g'], P['rwkv_ln_g'], P['rwkv_ln_b'], lay)
    if transposed:
        lat = _grid_transpose(out[nc:].reshape(bl, tl, W_A), GRID_W, rows)
        out = jnp.concatenate([out[:nc], lat.reshape(bl * tl, W_A)], axis=0)
    return out, s_fin


def _modulation_kernel(c_ref, w_ref, b_ref, o_ref):
    c = c_ref[...]
    w = w_ref[...]
    w_hi = w.astype(jnp.bfloat16)
    w_lo = (w - w_hi.astype(jnp.float32)).astype(jnp.bfloat16)
    o_ref[...] = _dot_split(c * jax.nn.sigmoid(c), w_hi, w_lo) + b_ref[...]


def _modulation(cvec, w_ada, b_ada):
    n_c, d = cvec.shape
    n_layers, _, n_out = w_ada.shape
    c_pad = jnp.pad(cvec, ((0, SUBLANES - n_c), (0, 0)))
    m = pl.pallas_call(
        _modulation_kernel,
        grid=(n_layers, n_out // ADA_TN),
        in_specs=[pl.BlockSpec((SUBLANES, d), lambda l, j: (0, 0)),
                  pl.BlockSpec((None, d, ADA_TN), lambda l, j: (l, 0, j)),
                  pl.BlockSpec((None, 1, ADA_TN), lambda l, j: (l, 0, j))],
        out_specs=pl.BlockSpec((None, SUBLANES, ADA_TN), lambda l, j: (l, 0, j)),
        out_shape=jax.ShapeDtypeStruct((n_layers, SUBLANES, n_out), jnp.float32),
        name="modulation",
        compiler_params=pltpu.CompilerParams(dimension_semantics=("parallel", "parallel"),
                                             vmem_limit_bytes=VMEM_LIMIT_BYTES),
    )(c_pad, w_ada, b_ada.reshape(n_layers, 1, n_out))
    return [m[l, :n_c].reshape(n_c, N_MOD, D_MODEL) for l in range(n_layers)]


def _pack_w_in(w_in):
    o = 3 * W_A
    parts = [w_in[..., 0:o], w_in[..., COLS_A:], w_in[..., o:COLS_A]]
    parts = [p.astype(jnp.bfloat16) for p in parts]
    parts.append(jnp.zeros(w_in.shape[:-1] + (IN_PAD - IN_COLS,), jnp.bfloat16))
    return jnp.concatenate(parts, axis=-1)


def kernel(x_prompt, x_sample, c, state_rwkv, c_ctx, w_ada, b_ada, w_in, mu_prev, mu_next, rwkv_g_up, rwkv_w0, rwkv_w_up, rwkv_a0, rwkv_a_up, rwkv_k_k, rwkv_k_a, rwkv_r_k, rwkv_ln_g, rwkv_ln_b, pool_w, pool_b, pool_scale, conf_conv_w, conf_conv_b, conf_ln_g, conf_ln_b, gmlp_ln_g, gmlp_ln_b, gmlp_ws, gmlp_bs, w_out, ln1_g, ln1_b, ffn_w_up, ffn_conv_w, ffn_conv_b, ffn_w_down, ln2_g, ln2_b):
    bc, tc, d = x_prompt.shape
    bl, tl, _ = x_sample.shape
    nc = bc * tc
    lay = TokenLayout(bc, tc, bl, tl)
    s_ctx0 = jnp.zeros((bc, 2, H_A, HEAD_A, HEAD_A), jnp.float32)
    cvec = jnp.concatenate([c_ctx[None, :], c], axis=0)
    mods = _modulation(cvec, w_ada, b_ada)
    w_in_p = _pack_w_in(w_in)
    w_out_b = w_out.astype(jnp.bfloat16)
    w_up_b = ffn_w_up.astype(jnp.bfloat16)
    w_dn_b = jnp.concatenate([ffn_w_down.astype(jnp.bfloat16),
                              jnp.zeros((DEPTH, FF_PAD - D_FF, d), jnp.bfloat16)], axis=1)
    x = jnp.concatenate([x_prompt.reshape(nc, d), x_sample.reshape(bl * tl, d)], axis=0)
    h = _modulate(x, jnp.stack([mods[0][:, 1], mods[0][:, 0]], axis=1), lay)
    new_states = []
    for l in range(DEPTH):
        P = {
            'mu_prev': mu_prev[l], 'mu_next': mu_next[l], 'g_up': rwkv_g_up[l],
            'w0': rwkv_w0[l], 'w_up': rwkv_w_up[l], 'a0': rwkv_a0[l], 'a_up': rwkv_a_up[l],
            'k_k': rwkv_k_k[l], 'k_a': rwkv_k_a[l], 'r_k': rwkv_r_k[l],
            'rwkv_ln_g': rwkv_ln_g[l], 'rwkv_ln_b': rwkv_ln_b[l],
        }

        m = mods[l]

        f = _matmul(h, w_in_p, l, jnp.float32, 2048, 512, D_MODEL)
        y_a, s_fin = _rwkv_group(f, P, lay, (bc, tc, bl, tl), s_ctx0, state_rwkv[:, l], l % 2 == 1)
        new_states.append(s_fin)
        y_b = _pool_group(f, pool_w[l], pool_b[l], pool_scale[l], lay)
        y_c = _conformer_group(f, conf_conv_w[l], conf_conv_b[l], conf_ln_g[l], conf_ln_b[l], lay)
        y_d = _gmlp_group(f, gmlp_ln_g[l], gmlp_ln_b[l], gmlp_ws[l], gmlp_bs[l], lay)
        y = _matmul4([y_a, y_b, y_c, y_d], w_out_b, l, 1024, 1024)
        x, h = _ln_residual(y, x, jnp.stack([m[:, 2], m[:, 4], m[:, 3]], axis=1), ln1_g[l], ln1_b[l], lay, False)

        z = _matmul(h, w_up_b, l, jnp.bfloat16, 2048, 512, D_MODEL)
        g = _ffn_gate(z, ffn_conv_w[l], ffn_conv_b[l], lay)
        y = _matmul(g, w_dn_b, l, jnp.float32, 1024, 1024, FF_PAD // 4)
        if l + 1 < DEPTH:
            mn = mods[l + 1]
            x, h = _ln_residual(y, x, jnp.stack([m[:, 5], mn[:, 1], mn[:, 0]], axis=1), ln2_g[l], ln2_b[l], lay, False)
        else:
            x_ctx, x_lat = _ln_residual(y, x, jnp.stack([m[:, 5], m[:, 5], m[:, 5]], axis=1), ln2_g[l], ln2_b[l],
                                        lay, True)

    new_state_rwkv = jnp.stack(new_states, axis=1).astype(x_prompt.dtype)
    return (x_ctx.reshape(bc, tc, d), x_lat.reshape(bl, tl, d), new_state_rwkv)
```
